```python
import jax, jax.numpy as jnp
from jax import lax
import numpy as np

D_MODEL = 1024
BATCH = 2
SEQ = 8192
DEPTH = 1
DEC_BATCH = 16
DEC_SEQ = 64
PAST_LEN = 1024

CHUNK = 64
LEFT_CHUNKS = 8
BAND_WINDOW = LEFT_CHUNKS * CHUNK
HEAD_DIM = 64
N_HEADS_A = 8
N_HEADS_B = 8
D_A = N_HEADS_A * HEAD_DIM
D_B = N_HEADS_B * HEAD_DIM
D_MIX = D_A + D_B
REL_CLIP = 128
N_EXPERTS = 32
TOP_K = 4
D_FF = D_MODEL
SWIGLU_LIMIT = 7.0
SWIGLU_ALPHA = 1.702
SB_BLOCK = 128
EPS = 1e-6
SCALE = HEAD_DIM ** -0.5

kernel_name = 'hybrid_chunkband_stickbreak_moe_step'


def _rmsnorm(x, g):
    xf = x.astype(jnp.float32)
    xf = xf * lax.rsqrt(jnp.mean(xf * xf, axis=-1, keepdims=True) + EPS)
    return xf.astype(x.dtype) * g


def _split_heads(xn, w_in):
    B, T, _ = xn.shape
    qkv = xn @ w_in
    qa, ka, va, qb, kb, vb = jnp.split(
        qkv, [D_A, 2 * D_A, 3 * D_A, 3 * D_A + D_B, 3 * D_A + 2 * D_B], axis=-1)
    ha = lambda t: t.reshape(B, T, N_HEADS_A, HEAD_DIM)
    hb = lambda t: t.reshape(B, T, N_HEADS_B, HEAD_DIM)
    return ha(qa), ha(ka), ha(va), hb(qb), hb(kb), hb(vb)


def _rel_bias(table, q_pos, k_pos):
    d = jnp.clip(q_pos[:, None] - k_pos[None, :], -REL_CLIP, REL_CLIP) + REL_CLIP
    return table[:, d]


def _band_attention_prompt(q, k, v, rel_table):
    B, S, H, Dh = q.shape
    nc = S // CHUNK
    band = (LEFT_CHUNKS + 1) * CHUNK
    pad = ((0, 0), (BAND_WINDOW, 0), (0, 0), (0, 0))
    idx = jnp.arange(nc)[:, None] + jnp.arange(LEFT_CHUNKS + 1)[None, :]

    def gather_band(t):
        tc = jnp.pad(t, pad).reshape(B, nc + LEFT_CHUNKS, CHUNK, H, Dh)
        return tc[:, idx].reshape(B, nc, band, H, Dh)

    k_band, v_band = gather_band(k), gather_band(v)
    qc = q.reshape(B, nc, CHUNK, H, Dh)
    bias = _rel_bias(rel_table, BAND_WINDOW + jnp.arange(CHUNK), jnp.arange(band))
    key_pos = jnp.arange(nc)[:, None] * CHUNK - BAND_WINDOW + jnp.arange(band)[None, :]
    s = jnp.einsum('bcqhd,bckhd->bchqk', qc, k_band).astype(jnp.float32) * SCALE + bias[None, None]
    s = jnp.where((key_pos >= 0)[None, :, None, None, :], s, -jnp.inf)
    p = jax.nn.softmax(s, axis=-1).astype(v.dtype)
    o = jnp.einsum('bchqk,bckhd->bcqhd', p, v_band)
    return o.reshape(B, S, H, Dh)


def _band_attention_sample(q, k_new, v_new, k_cache, v_cache, rel_table):
    T = q.shape[1]
    W = k_cache.shape[1]
    k_all = jnp.concatenate([k_cache, k_new], axis=1)
    v_all = jnp.concatenate([v_cache, v_new], axis=1)
    bias = _rel_bias(rel_table, W + jnp.arange(T), jnp.arange(W + T))
    s = jnp.einsum('bqhd,bkhd->bhqk', q, k_all).astype(jnp.float32) * SCALE + bias[None]
    p = jax.nn.softmax(s, axis=-1).astype(v_all.dtype)
    o = jnp.einsum('bhqk,bkhd->bqhd', p, v_all)
    return o, k_all[:, T:], v_all[:, T:]


def _stick_breaking_block(q, k, v, q_pos, k_pos):
    z = jnp.einsum('bqhd,bkhd->bhqk', q, k).astype(jnp.float32) * SCALE
    visible = (k_pos[None, :] < q_pos[:, None])[None, None]
    log_1m = jnp.where(visible, -jax.nn.softplus(z), 0.0)
    after = lax.cumsum(log_1m, axis=3, reverse=True) - log_1m
    a = jnp.where(visible, jnp.exp(jax.nn.log_sigmoid(z) + after), 0.0)
    return jnp.einsum('bhqk,bkhd->bqhd', a.astype(v.dtype), v)


def _stick_breaking_prompt(q, k, v):
    B, S, H, Dh = q.shape
    nb = S // SB_BLOCK
    qb = q.reshape(B, nb, SB_BLOCK, H, Dh).transpose(1, 0, 2, 3, 4)
    k_pos = jnp.arange(S)

    def one_block(args):
        qi, bi = args
        q_pos = bi * SB_BLOCK + jnp.arange(SB_BLOCK)
        return _stick_breaking_block(qi, k, v, q_pos, k_pos)

    o = lax.map(one_block, (qb, jnp.arange(nb)))
    return o.transpose(1, 0, 2, 3, 4).reshape(B, S, H, Dh)


def _stick_breaking_sample(q, k_new, v_new, k_cache, v_cache):
    T = q.shape[1]
    P = k_cache.shape[1]
    k_all = jnp.concatenate([k_cache, k_new], axis=1)
    v_all = jnp.concatenate([v_cache, v_new], axis=1)
    return _stick_breaking_block(q, k_all, v_all, P + jnp.arange(T), jnp.arange(P + T))


def _merge(o_a, o_b, g_a, g_b, w_out):
    B, T = o_a.shape[:2]
    o = jnp.concatenate([_rmsnorm(o_a.reshape(B, T, D_A), g_a),
                         _rmsnorm(o_b.reshape(B, T, D_B), g_b)], axis=-1)
    return o @ w_out


def _moe(x, w_router, b_router, w_gate, b_gate, w_up, b_up, w_down, b_down):
    B, T, D = x.shape
    xt = x.reshape(B * T, D)
    logits = (xt @ w_router + b_router).astype(jnp.float32)
    top_val, top_idx = lax.top_k(logits, TOP_K)
    top_w = jax.nn.softmax(top_val, axis=-1)
    gates = jnp.einsum('nk,nke->ne', top_w,
                       jax.nn.one_hot(top_idx, N_EXPERTS, dtype=jnp.float32)).astype(x.dtype)
    y = jnp.zeros_like(xt)
    for e in range(N_EXPERTS):
        g = jnp.minimum(xt @ w_gate[e] + b_gate[e], SWIGLU_LIMIT)
        u = jnp.clip(xt @ w_up[e] + b_up[e], -SWIGLU_LIMIT, SWIGLU_LIMIT)
        h = g * jax.nn.sigmoid(SWIGLU_ALPHA * g) * (u + 1.0)
        y = y + gates[:, e:e + 1] * (h @ w_down[e] + b_down[e])
    return y.reshape(B, T, D)


def setup_inputs(seed: int = 0) -> dict:
    key = jax.random.key(seed)
    ks = jax.random.split(key, 24)
    w_a = min(BAND_WINDOW, PAST_LEN)
    f32 = jnp.float32
    nrm = lambda k, shape, s: jax.random.normal(k, shape, f32) * s
    return {
        'x_prompt': nrm(ks[0], (BATCH, SEQ, D_MODEL), 1.0),
        'x_sample': nrm(ks[1], (DEC_BATCH, DEC_SEQ, D_MODEL), 1.0),
        'cache_a_k': nrm(ks[2], (DEPTH, DEC_BATCH, w_a, N_HEADS_A, HEAD_DIM), 1.0),
        'cache_a_v': nrm(ks[3], (DEPTH, DEC_BATCH, w_a, N_HEADS_A, HEAD_DIM), 1.0),
        'cache_b_k': nrm(ks[4], (DEPTH, DEC_BATCH, PAST_LEN, N_HEADS_B, HEAD_DIM), 1.0),
        'cache_b_v': nrm(ks[5], (DEPTH, DEC_BATCH, PAST_LEN, N_HEADS_B, HEAD_DIM), 1.0),
        'norm_attn': 1.0 + nrm(ks[6], (DEPTH, D_MODEL), 0.05),
        'w_in': nrm(ks[7], (DEPTH, D_MODEL, 3 * D_MIX), D_MODEL ** -0.5),
        'rel_bias': nrm(ks[8], (DEPTH, N_HEADS_A, 2 * REL_CLIP + 1), 0.2),
        'g_out_a': 1.0 + nrm(ks[9], (DEPTH, D_A), 0.05),
        'g_out_b': 1.0 + nrm(ks[10], (DEPTH, D_B), 0.05),
        'w_out': nrm(ks[11], (DEPTH, D_MIX, D_MODEL), D_MIX ** -0.5),
        'norm_ffn': 1.0 + nrm(ks[12], (DEPTH, D_MODEL), 0.05),
        'w_router': nrm(ks[13], (DEPTH, D_MODEL, N_EXPERTS), D_MODEL ** -0.5),
        'b_router': nrm(ks[14], (DEPTH, N_EXPERTS), 0.01),
        'w_gate': nrm(ks[15], (DEPTH, N_EXPERTS, D_MODEL, D_FF), D_MODEL ** -0.5),
        'b_gate': nrm(ks[16], (DEPTH, N_EXPERTS, D_FF), 0.02),
        'w_up': nrm(ks[17], (DEPTH, N_EXPERTS, D_MODEL, D_FF), D_MODEL ** -0.5),
        'b_up': nrm(ks[18], (DEPTH, N_EXPERTS, D_FF), 0.02),
        'w_down': nrm(ks[19], (DEPTH, N_EXPERTS, D_FF, D_MODEL), D_FF ** -0.5),
        'b_down': nrm(ks[20], (DEPTH, N_EXPERTS, D_MODEL), 0.02),
        'norm_final': 1.0 + nrm(ks[21], (D_MODEL,), 0.05),
    }


def reference(x_prompt, x_sample, cache_a_k, cache_a_v, cache_b_k, cache_b_v,
              norm_attn, w_in, rel_bias, g_out_a, g_out_b, w_out, norm_ffn,
              w_router, b_router, w_gate, b_gate, w_up, b_up, w_down, b_down, norm_final):
    xp, xs = x_prompt, x_sample
    S = xp.shape[1]
    w_p = min(BAND_WINDOW, S)
    ak_p, av_p, bk_p, bv_p, ak_s, av_s, bk_s, bv_s = [], [], [], [], [], [], [], []
    for l in range(DEPTH):
        qa, ka, va, qb, kb, vb = _split_heads(_rmsnorm(xp, norm_attn[l]), w_in[l])
        o_a = _band_attention_prompt(qa, ka, va, rel_bias[l])
        o_b = _stick_breaking_prompt(qb, kb, vb)
        xp = xp + _merge(o_a, o_b, g_out_a[l], g_out_b[l], w_out[l])
        xp = xp + _moe(_rmsnorm(xp, norm_ffn[l]), w_router[l], b_router[l], w_gate[l], b_gate[l],
                       w_up[l], b_up[l], w_down[l], b_down[l])
        ak_p.append(ka[:, S - w_p:])
        av_p.append(va[:, S - w_p:])
        bk_p.append(kb)
        bv_p.append(vb)
        qa, ka, va, qb, kb, vb = _split_heads(_rmsnorm(xs, norm_attn[l]), w_in[l])
        o_a, nak, nav = _band_attention_sample(qa, ka, va, cache_a_k[l], cache_a_v[l], rel_bias[l])
        o_b = _stick_breaking_sample(qb, kb, vb, cache_b_k[l], cache_b_v[l])
        xs = xs + _merge(o_a, o_b, g_out_a[l], g_out_b[l], w_out[l])
        xs = xs + _moe(_rmsnorm(xs, norm_ffn[l]), w_router[l], b_router[l], w_gate[l], b_gate[l],
                       w_up[l], b_up[l], w_down[l], b_down[l])
        ak_s.append(nak)
        av_s.append(nav)
        bk_s.append(kb)
        bv_s.append(vb)
    y_prompt = _rmsnorm(xp, norm_final)
    y_sample = _rmsnorm(xs, norm_final)
    new_a_k_prompt = jnp.stack(ak_p)
    new_a_v_prompt = jnp.stack(av_p)
    new_b_k_prompt = jnp.stack(bk_p)
    new_b_v_prompt = jnp.stack(bv_p)
    new_a_k_sample = jnp.stack(ak_s)
    new_a_v_sample = jnp.stack(av_s)
    new_b_k_sample = jnp.stack(bk_s)
    new_b_v_sample = jnp.stack(bv_s)
    return (y_prompt, y_sample, new_a_k_prompt, new_a_v_prompt, new_b_k_prompt, new_b_v_prompt,
            new_a_k_sample, new_a_v_sample, new_b_k_sample, new_b_v_sample)
```

```python
import functools

import jax
import jax.numpy as jnp
from jax import lax
from jax.experimental import pallas as pl
from jax.experimental.pallas import tpu as pltpu

F32 = jnp.float32
BF16 = jnp.bfloat16

CHUNK = 64
LEFT_CHUNKS = 8
BAND_WINDOW = LEFT_CHUNKS * CHUNK
BAND = BAND_WINDOW + CHUNK
HEAD_DIM = 64
HEADS_PER_SLAB = 2
LANES = 128
REL_CLIP = 128
TOP_K = 4
SWIGLU_LIMIT = 7.0
SWIGLU_ALPHA = 1.702
EPS = 1e-6
SCALE = HEAD_DIM ** -0.5

ROW_TILE = 512
SB_TQ = 256
SB_TK = 256
VMEM_LIMIT = 56 * 1024 * 1024


def _params(*sem):
    return pltpu.CompilerParams(dimension_semantics=sem, vmem_limit_bytes=VMEM_LIMIT)


def _rms(x, g):
    ms = jnp.mean(x * x, axis=-1, keepdims=True)
    return (x * lax.rsqrt(ms + EPS)) * g


def _qkv_kernel(x_ref, g_ref, w_ref, qkv_ref, ka_ref, va_ref, kb_ref, vb_ref):
    xb = _rms(x_ref[...], g_ref[...]).astype(BF16)
    d = ka_ref.shape[-1]
    f32_out = {1: ka_ref, 2: va_ref, 4: kb_ref, 5: vb_ref}
    for c in range(6):
        r = jnp.dot(xb, w_ref[:, c * d:(c + 1) * d], preferred_element_type=F32)
        if c in f32_out:
            f32_out[c][...] = r
        else:
            r = r * SCALE
        qkv_ref[:, c * d:(c + 1) * d] = r.astype(BF16)


def _qkv_proj(x2d, g, w_bf):
    n, dm = x2d.shape
    d3 = w_bf.shape[1]
    d = d3 // 6
    tm = min(ROW_TILE, n)
    row = lambda i: (i, 0)
    fixed = lambda i: (0, 0)
    return pl.pallas_call(
        _qkv_kernel,
        grid=(n // tm,),
        in_specs=[pl.BlockSpec((tm, dm), row),
                  pl.BlockSpec((1, dm), fixed),
                  pl.BlockSpec((dm, d3), fixed)],
        out_specs=[pl.BlockSpec((tm, d3), row)] + [pl.BlockSpec((tm, d), row)] * 4,
        out_shape=[jax.ShapeDtypeStruct((n, d3), BF16)]
        + [jax.ShapeDtypeStruct((n, d), F32)] * 4,
        compiler_params=_params("parallel"),
        name="qkv_proj",
    )(x2d, g.reshape(1, dm), w_bf)


def _bias_kernel(tab_ref, o_ref):
    h = pl.program_id(0)
    i = lax.broadcasted_iota(jnp.int32, (CHUNK, BAND), 0)
    j = lax.broadcasted_iota(jnp.int32, (CHUNK, BAND), 1)
    idx = jnp.clip(BAND_WINDOW + i - j, -REL_CLIP, REL_CLIP) + REL_CLIP
    n_tab = 2 * REL_CLIP + 1

    def body(t, acc):
        return jnp.where(idx == t, tab_ref[h, t], acc)

    init = jnp.full((CHUNK, BAND), tab_ref[h, n_tab - 1], F32)
    o_ref[...] = lax.fori_loop(0, n_tab - 1, body, init)


def _band_bias(table):
    nh = table.shape[0]
    return pl.pallas_call(
        _bias_kernel,
        grid=(nh,),
        in_specs=[pl.BlockSpec(memory_space=pltpu.SMEM)],
        out_specs=pl.BlockSpec((None, CHUNK, BAND), lambda h: (h, 0, 0)),
        out_shape=jax.ShapeDtypeStruct((nh, CHUNK, BAND), F32),
        compiler_params=_params("arbitrary"),
        name="band_bias",
    )(table)


def _band_kernel(q_ref, kp_ref, kc_ref, vp_ref, vc_ref, bias_ref, o_ref, *, mask_first):
    j = pl.program_id(2)
    tq = q_ref.shape[0]
    lane = lax.broadcasted_iota(jnp.int32, (1, LANES), 1)
    col = lax.broadcasted_iota(jnp.int32, (CHUNK, BAND), 1)
    kp = kp_ref[...].astype(BF16)
    vp = vp_ref[...].astype(BF16)
    kc = kc_ref[...]
    vc = vc_ref[...]
    for c in range(tq // CHUNK):
        r0 = c * CHUNK
        qc = q_ref[r0:r0 + CHUNK, :]
        kwin = jnp.concatenate([kp[r0:], kc[:r0 + CHUNK]], axis=0)
        vwin = jnp.concatenate([vp[r0:], vc[:r0 + CHUNK]], axis=0)
        outs = []
        for h in range(HEADS_PER_SLAB):
            in_head = (lane >= h * HEAD_DIM) & (lane < (h + 1) * HEAD_DIM)
            qm = jnp.where(in_head, qc, jnp.zeros_like(qc))
            s = lax.dot_general(qm, kwin, (((1,), (1,)), ((), ())),
                                preferred_element_type=F32) + bias_ref[h]
            if mask_first:
                first_valid = jnp.where(j == 0, BAND_WINDOW - r0, 0)
                s = jnp.where(col >= first_valid, s, -jnp.inf)
            m = jnp.max(s, axis=-1, keepdims=True)
            p = jnp.exp(s - m)
            l = jnp.sum(p, axis=-1, keepdims=True)
            pv = jnp.dot(p.astype(BF16), vwin, preferred_element_type=F32)
            outs.append(pv / l)
        o_ref[r0:r0 + CHUNK, :] = jnp.where(lane < HEAD_DIM, outs[0], outs[1])


def _band_attention(q_arr, q_col0, kprev_arr, kprev_col0, kcur_arr, kcur_col0,
                    vprev_arr, vprev_col0, vcur_arr, vcur_col0, bias, tq, mask_first):
    b, t, _ = q_arr.shape
    nslab = bias.shape[0] // HEADS_PER_SLAB
    prev_rows = lambda bi, hp, j: jnp.maximum(j - 1, 0)
    blk = lambda rows: (None, rows, LANES)
    return pl.pallas_call(
        functools.partial(_band_kernel, mask_first=mask_first),
        grid=(b, nslab, t // tq),
        in_specs=[
            pl.BlockSpec(blk(tq), lambda bi, hp, j: (bi, j, q_col0 + hp)),
            pl.BlockSpec(blk(BAND_WINDOW), lambda bi, hp, j: (bi, prev_rows(bi, hp, j), kprev_col0 + hp)),
            pl.BlockSpec(blk(tq), lambda bi, hp, j: (bi, j, kcur_col0 + hp)),
            pl.BlockSpec(blk(BAND_WINDOW), lambda bi, hp, j: (bi, prev_rows(bi, hp, j), vprev_col0 + hp)),
            pl.BlockSpec(blk(tq), lambda bi, hp, j: (bi, j, vcur_col0 + hp)),
            pl.BlockSpec((HEADS_PER_SLAB, CHUNK, BAND), lambda bi, hp, j: (hp, 0, 0)),
        ],
        out_specs=pl.BlockSpec(blk(tq), lambda bi, hp, j: (bi, j, hp)),
        out_shape=jax.ShapeDtypeStruct((b, t, nslab * LANES), F32),
        compiler_params=_params("parallel", "parallel", "arbitrary"),
        name="band_attention",
    )(q_arr, kprev_arr, kcur_arr, vprev_arr, vcur_arr, bias)


def _sb_kernel(q_ref, k_ref, v_ref, o_ref, acc_ref, run_ref, *, q_base):
    i = pl.program_id(2)
    tq = q_ref.shape[0]
    tk = SB_TK
    qpos0 = q_base + i * tq
    diag = qpos0 // tk
    lane = lax.broadcasted_iota(jnp.int32, (1, LANES), 1)
    q = q_ref[...]
    qm = [jnp.where((lane >= h * HEAD_DIM) & (lane < (h + 1) * HEAD_DIM), q, jnp.zeros_like(q))
          for h in range(HEADS_PER_SLAB)]
    jj = lax.broadcasted_iota(jnp.int32, (tk, tk), 0)
    ss = lax.broadcasted_iota(jnp.int32, (tk, tk), 1)
    neg_upper = jnp.where(jj > ss, -1.0, 0.0).astype(BF16)

    acc_ref[...] = jnp.zeros_like(acc_ref)
    run_ref[...] = jnp.zeros_like(run_ref)

    def block(kb, masked):
        k0 = pl.multiple_of(kb * tk, tk)
        kblk = k_ref[pl.ds(k0, tk), :]
        vblk = v_ref[pl.ds(k0, tk), :]
        if masked:
            qpos = qpos0 + lax.broadcasted_iota(jnp.int32, (tq, tk), 0)
            kpos = k0 + lax.broadcasted_iota(jnp.int32, (tq, tk), 1)
            vis = kpos < qpos
        for h in range(HEADS_PER_SLAB):
            z = lax.dot_general(qm[h], kblk, (((1,), (1,)), ((), ())),
                                preferred_element_type=F32)
            sp = jnp.maximum(z, 0.0) + jnp.log(1.0 + jnp.exp(-jnp.abs(z)))
            if masked:
                sp = jnp.where(vis, sp, 0.0)
            hi = sp.astype(BF16)
            lo = (sp - hi.astype(F32)).astype(BF16)
            after = (jnp.dot(hi, neg_upper, preferred_element_type=F32)
                     + jnp.dot(lo, neg_upper, preferred_element_type=F32))
            run = run_ref[h]
            a = jnp.exp((z - sp) + after + run)
            if masked:
                a = jnp.where(vis, a, 0.0)
            acc_ref[h] += jnp.dot(a.astype(BF16), vblk, preferred_element_type=F32)
            run_ref[h] = run + (after[:, 0:1] - sp[:, 0:1])

    block(diag, True)

    def body(n, carry):
        block(diag - 1 - n, False)
        return carry

    lax.fori_loop(0, diag, body, 0)
    o_ref[...] = jnp.where(lane < HEAD_DIM, acc_ref[0], acc_ref[1])


def _sb_attention(q_arr, q_col0, k_arr, k_col0, v_arr, v_col0, nslab, tq, q_base):
    b, t, _ = q_arr.shape
    tkeys = k_arr.shape[1]
    assert q_base % SB_TK == 0 and SB_TK % tq == 0 and tkeys % SB_TK == 0
    assert q_base + t <= tkeys
    return pl.pallas_call(
        functools.partial(_sb_kernel, q_base=q_base),
        grid=(b, nslab, t // tq),
        in_specs=[
            pl.BlockSpec((None, tq, LANES), lambda bi, hp, i: (bi, i, q_col0 + hp)),
            pl.BlockSpec((None, tkeys, LANES), lambda bi, hp, i: (bi, 0, k_col0 + hp)),
            pl.BlockSpec((None, tkeys, LANES), lambda bi, hp, i: (bi, 0, v_col0 + hp)),
        ],
        out_specs=pl.BlockSpec((None, tq, LANES), lambda bi, hp, i: (bi, i, hp)),
        out_shape=jax.ShapeDtypeStruct((b, t, nslab * LANES), F32),
        scratch_shapes=[pltpu.VMEM((HEADS_PER_SLAB, tq, LANES), F32),
                        pltpu.VMEM((HEADS_PER_SLAB, tq, 1), F32)],
        compiler_params=_params("parallel", "parallel", "arbitrary"),
        name="stick_breaking",
    )(q_arr, k_arr, v_arr)


def _split_bf16(x):
    hi = x.astype(BF16)
    return hi, (x - hi.astype(F32)).astype(BF16)


def _merge_kernel(x_ref, oa_ref, ob_ref, ga_ref, gb_ref, wo_ref, gf_ref, wr_ref, br_ref,
                  x1_ref, xn_ref, gates_ref):
    o = jnp.concatenate([_rms(oa_ref[...], ga_ref[...]), _rms(ob_ref[...], gb_ref[...])],
                        axis=-1).astype(BF16)
    x1 = x_ref[...] + jnp.dot(o, wo_ref[...], preferred_element_type=F32)
    x1_ref[...] = x1
    xn = _rms(x1, gf_ref[...])
    xn_ref[...] = xn.astype(BF16)
    xh, xl = _split_bf16(xn)
    wh, wl = _split_bf16(wr_ref[...])
    logits = (jnp.dot(xh, wh, preferred_element_type=F32)
              + jnp.dot(xl, wh, preferred_element_type=F32)
              + jnp.dot(xh, wl, preferred_element_type=F32)) + br_ref[...]
    ne = logits.shape[-1]
    lane = lax.broadcasted_iota(jnp.int32, logits.shape, 1)
    work = logits
    top = None
    denom = None
    gates = jnp.zeros_like(logits)
    for _ in range(TOP_K):
        m = jnp.max(work, axis=-1, keepdims=True)
        first = jnp.min(jnp.where(work == m, lane, ne), axis=-1, keepdims=True)
        sel = lane == first
        if top is None:
            top = m
        e = jnp.exp(m - top)
        denom = e if denom is None else denom + e
        gates = jnp.where(sel, e, gates)
        work = jnp.where(sel, -jnp.inf, work)
    gates_ref[...] = gates / denom


def _merge(x2d, oa, ob, ga, gb, wo_bf, gf, w_router, b_router):
    n, dm = x2d.shape
    da = oa.shape[1]
    ne = w_router.shape[1]
    tm = min(ROW_TILE, n)
    row = lambda i: (i, 0)
    fixed = lambda i: (0, 0)
    return pl.pallas_call(
        _merge_kernel,
        grid=(n // tm,),
        in_specs=[pl.BlockSpec((tm, dm), row),
                  pl.BlockSpec((tm, da), row),
                  pl.BlockSpec((tm, da), row),
                  pl.BlockSpec((1, da), fixed),
                  pl.BlockSpec((1, da), fixed),
                  pl.BlockSpec((2 * da, dm), fixed),
                  pl.BlockSpec((1, dm), fixed),
                  pl.BlockSpec((dm, ne), fixed),
                  pl.BlockSpec((1, ne), fixed)],
        out_specs=[pl.BlockSpec((tm, dm), row),
                   pl.BlockSpec((tm, dm), row),
                   pl.BlockSpec((tm, ne), row)],
        out_shape=[jax.ShapeDtypeStruct((n, dm), F32),
                   jax.ShapeDtypeStruct((n, dm), BF16),
                   jax.ShapeDtypeStruct((n, ne), F32)],
        compiler_params=_params("parallel"),
        name="merge_router",
    )(x2d, oa, ob, ga.reshape(1, da), gb.reshape(1, da), wo_bf, gf.reshape(1, dm),
      w_router, b_router.reshape(1, ne))


def _moe_kernel(xn_ref, x1_ref, gates_ref, wg_ref, bg_ref, wu_ref, bu_ref, wd_ref, bd_ref,
                gfin_ref, y_ref, acc_ref):
    e = pl.program_id(1)

    @pl.when(e == 0)
    def _():
        acc_ref[...] = jnp.zeros_like(acc_ref)

    x = xn_ref[...]
    g = jnp.minimum(jnp.dot(x, wg_ref[...], preferred_element_type=F32) + bg_ref[...], SWIGLU_LIMIT)
    u = jnp.clip(jnp.dot(x, wu_ref[...], preferred_element_type=F32) + bu_ref[...],
                 -SWIGLU_LIMIT, SWIGLU_LIMIT)
    h = g * jax.nn.sigmoid(SWIGLU_ALPHA * g) * (u + 1.0)
    y = jnp.dot(h.astype(BF16), wd_ref[...], preferred_element_type=F32) + bd_ref[...]
    gates = gates_ref[...]
    lane = lax.broadcasted_iota(jnp.int32, gates.shape, 1)
    gate = jnp.sum(jnp.where(lane == e, gates, 0.0), axis=-1, keepdims=True)
    acc_ref[...] += gate * y

    @pl.when(e == pl.num_programs(1) - 1)
    def _():
        y_ref[...] = _rms(x1_ref[...] + acc_ref[...], gfin_ref[...])


def _moe_dense(xn, x1, gates, wg, bg, wu, bu, wd, bd, gfin):
    n, dm = xn.shape
    ne, _, dff = wg.shape
    tm = min(ROW_TILE, n)
    row = lambda i, e: (i, 0)
    exp3 = lambda i, e: (e, 0, 0)
    return pl.pallas_call(
        _moe_kernel,
        grid=(n // tm, ne),
        in_specs=[pl.BlockSpec((tm, dm), row),
                  pl.BlockSpec((tm, dm), row),
                  pl.BlockSpec((tm, ne), row),
                  pl.BlockSpec((None, dm, dff), exp3),
                  pl.BlockSpec((None, 1, dff), exp3),
                  pl.BlockSpec((None, dm, dff), exp3),
                  pl.BlockSpec((None, 1, dff), exp3),
                  pl.BlockSpec((None, dff, dm), exp3),
                  pl.BlockSpec((None, 1, dm), exp3),
                  pl.BlockSpec((1, dm), lambda i, e: (0, 0))],
        out_specs=pl.BlockSpec((tm, dm), row),
        out_shape=jax.ShapeDtypeStruct((n, dm), F32),
        scratch_shapes=[pltpu.VMEM((tm, dm), F32)],
        compiler_params=_params("parallel", "arbitrary"),
        name="moe",
    )(xn, x1, gates, wg, bg.reshape(ne, 1, dff), wu, bu.reshape(ne, 1, dff),
      wd, bd.reshape(ne, 1, dm), gfin.reshape(1, dm))


def kernel(x_prompt, x_sample, cache_a_k, cache_a_v, cache_b_k, cache_b_v, norm_attn, w_in,
           rel_bias, g_out_a, g_out_b, w_out, norm_ffn, w_router, b_router, w_gate, b_gate,
           w_up, b_up, w_down, b_down, norm_final):
    depth = w_in.shape[0]
    assert depth == 1, "the head-merged MoE/final-norm kernel assumes a single layer"
    bp, sp, dm = x_prompt.shape
    bs, ts, _ = x_sample.shape
    past = cache_b_k.shape[2]
    wa = cache_a_k.shape[2]
    nh_a, hd = cache_a_k.shape[3], cache_a_k.shape[4]
    nh_b = cache_b_k.shape[3]
    da, db = nh_a * hd, nh_b * hd
    assert da == db and hd == HEAD_DIM and wa == BAND_WINDOW and ts == CHUNK
    slabs = da // LANES
    QA, KA, VA, QB, KB, VB = (c * slabs for c in range(6))
    l = 0

    xp2 = x_prompt.reshape(bp * sp, dm)
    xs2 = x_sample.reshape(bs * ts, dm)
    w_in_bf = w_in[l].astype(BF16)
    qkv_p, ka_p, va_p, kb_p, vb_p = _qkv_proj(xp2, norm_attn[l], w_in_bf)
    qkv_s, ka_s, va_s, kb_s, vb_s = _qkv_proj(xs2, norm_attn[l], w_in_bf)
    qkv_p3 = qkv_p.reshape(bp, sp, 6 * da)
    qkv_s3 = qkv_s.reshape(bs, ts, 6 * da)

    bias = _band_bias(rel_bias[l])

    oa_p = _band_attention(qkv_p3, QA, qkv_p3, KA, qkv_p3, KA, qkv_p3, VA, qkv_p3, VA,
                           bias, tq=BAND_WINDOW, mask_first=True)
    ob_p = _sb_attention(qkv_p3, QB, qkv_p3, KB, qkv_p3, VB, slabs, tq=SB_TQ, q_base=0)

    ck_a = cache_a_k[l].reshape(bs, wa, da)
    cv_a = cache_a_v[l].reshape(bs, wa, da)
    oa_s = _band_attention(qkv_s3, QA, ck_a, 0, qkv_s3, KA, cv_a, 0, qkv_s3, VA,
                           bias, tq=CHUNK, mask_first=False)
    tk_all = -(-(past + ts) // SB_TK) * SB_TK
    pad = jnp.zeros((bs, tk_all - past - ts, db), BF16)
    kb_all = jnp.concatenate([cache_b_k[l].reshape(bs, past, db).astype(BF16),
                              qkv_s3[:, :, KB * LANES:KB * LANES + db], pad], axis=1)
    vb_all = jnp.concatenate([cache_b_v[l].reshape(bs, past, db).astype(BF16),
                              qkv_s3[:, :, VB * LANES:VB * LANES + db], pad], axis=1)
    ob_s = _sb_attention(qkv_s3, QB, kb_all, 0, vb_all, 0, slabs, tq=ts, q_base=past)

    x_all = jnp.concatenate([xp2, xs2], axis=0)
    oa_all = jnp.concatenate([oa_p.reshape(bp * sp, da), oa_s.reshape(bs * ts, da)], axis=0)
    ob_all = jnp.concatenate([ob_p.reshape(bp * sp, db), ob_s.reshape(bs * ts, db)], axis=0)
    x1, xn2, gates = _merge(x_all, oa_all, ob_all, g_out_a[l], g_out_b[l],
                            w_out[l].astype(BF16), norm_ffn[l], w_router[l], b_router[l])
    y = _moe_dense(xn2, x1, gates, w_gate[l].astype(BF16), b_gate[l], w_up[l].astype(BF16),
                   b_up[l], w_down[l].astype(BF16), b_down[l], norm_final)

    y_prompt = y[:bp * sp].reshape(bp, sp, dm)
    y_sample = y[bp * sp:].reshape(bs, ts, dm)
    wp = min(BAND_WINDOW, sp)
    heads_a = lambda t, b_, n_: t.reshape(b_, n_, nh_a, hd)
    heads_b = lambda t, b_, n_: t.reshape(b_, n_, nh_b, hd)
    new_a_k_prompt = heads_a(ka_p, bp, sp)[:, sp - wp:][None]
    new_a_v_prompt = heads_a(va_p, bp, sp)[:, sp - wp:][None]
    new_b_k_prompt = heads_b(kb_p, bp, sp)[None]
    new_b_v_prompt = heads_b(vb_p, bp, sp)[None]
    new_a_k_sample = jnp.concatenate([cache_a_k[l], heads_a(ka_s, bs, ts)], axis=1)[:, ts:][None]
    new_a_v_sample = jnp.concatenate([cache_a_v[l], heads_a(va_s, bs, ts)], axis=1)[:, ts:][None]
    new_b_k_sample = heads_b(kb_s, bs, ts)[None]
    new_b_v_sample = heads_b(vb_s, bs, ts)[None]
    return (y_prompt, y_sample, new_a_k_prompt, new_a_v_prompt, new_b_k_prompt, new_b_v_prompt,
            new_a_k_sample, new_a_v_sample, new_b_k_sample, new_b_v_sample)
```

```python
import functools

import jax
import jax.numpy as jnp
from jax import lax
from jax.experimental import pallas as pl
from jax.experimental.pallas import tpu as pltpu

F32 = jnp.float32
BF16 = jnp.bfloat16
I32 = jnp.int32

CHUNK = 64
LEFT_CHUNKS = 8
BAND_WINDOW = LEFT_CHUNKS * CHUNK
BAND = BAND_WINDOW + CHUNK
HEAD_DIM = 64
HEADS_PER_SLAB = 2
LANES = 128
REL_CLIP = 128
TOP_K = 4
SWIGLU_LIMIT = 7.0
SWIGLU_ALPHA = 1.702
EPS = 1e-6
SCALE = HEAD_DIM ** -0.5

ROW_TILE = 512
BAND_TQ = 256
BAND_KEYS = BAND_TQ + BAND_WINDOW
SB_TQ = 512
SB_TK = 256
SB_UNROLL = 4
MOE_TILE = 512
ROUTE_TILE = 256
VMEM_LIMIT = 56 * 1024 * 1024


def _params(*sem):
    return pltpu.CompilerParams(dimension_semantics=sem, vmem_limit_bytes=VMEM_LIMIT)


def _rms(x, g):
    ms = jnp.mean(x * x, axis=-1, keepdims=True)
    return (x * lax.rsqrt(ms + EPS)) * g


def _head_mask(h):
    lane = lax.broadcasted_iota(I32, (1, LANES), 1)
    return (lane >= h * HEAD_DIM) & (lane < (h + 1) * HEAD_DIM)


def _qkv_kernel(x_ref, g_ref, w_ref, qkv_ref, ka_ref, va_ref, kb_ref, vb_ref):
    xb = _rms(x_ref[...], g_ref[...]).astype(BF16)
    d = ka_ref.shape[-1]
    f32_out = {1: ka_ref, 2: va_ref, 4: kb_ref, 5: vb_ref}
    for c in range(6):
        r = jnp.dot(xb, w_ref[:, c * d:(c + 1) * d], preferred_element_type=F32)
        if c in f32_out:
            f32_out[c][...] = r
        else:
            r = r * SCALE
        qkv_ref[:, c * d:(c + 1) * d] = r.astype(BF16)


def _qkv_proj(x2d, g, w_bf):
    n, dm = x2d.shape
    d3 = w_bf.shape[1]
    d = d3 // 6
    tm = min(ROW_TILE, n)
    row = lambda i: (i, 0)
    fixed = lambda i: (0, 0)
    return pl.pallas_call(
        _qkv_kernel,
        grid=(n // tm,),
        in_specs=[pl.BlockSpec((tm, dm), row),
                  pl.BlockSpec((1, dm), fixed),
                  pl.BlockSpec((dm, d3), fixed)],
        out_specs=[pl.BlockSpec((tm, d3), row)] + [pl.BlockSpec((tm, d), row)] * 4,
        out_shape=[jax.ShapeDtypeStruct((n, d3), BF16)]
        + [jax.ShapeDtypeStruct((n, d), F32)] * 4,
        compiler_params=_params("parallel"),
        name="qkv_proj",
    )(x2d, g.reshape(1, dm), w_bf)


def _bias_kernel(tab_ref, o_ref):
    h = pl.program_id(0)
    i = lax.broadcasted_iota(I32, (CHUNK, BAND), 0)
    j = lax.broadcasted_iota(I32, (CHUNK, BAND), 1)
    idx = jnp.clip(BAND_WINDOW + i - j, -REL_CLIP, REL_CLIP) + REL_CLIP
    n_tab = 2 * REL_CLIP + 1

    def body(t, acc):
        return jnp.where(idx == t, tab_ref[h, t], acc)

    init = jnp.full((CHUNK, BAND), tab_ref[h, n_tab - 1], F32)
    o_ref[...] = lax.fori_loop(0, n_tab - 1, body, init)


def _band_bias(table):
    nh = table.shape[0]
    return pl.pallas_call(
        _bias_kernel,
        grid=(nh,),
        in_specs=[pl.BlockSpec(memory_space=pltpu.SMEM)],
        out_specs=pl.BlockSpec((None, CHUNK, BAND), lambda h: (h, 0, 0)),
        out_shape=jax.ShapeDtypeStruct((nh, CHUNK, BAND), F32),
        compiler_params=_params("arbitrary"),
        name="band_bias",
    )(table)


def _tile_bias_kernel(b_ref, o_ref, *, n_var):
    v = pl.program_id(1)
    o_ref[...] = jnp.full(o_ref.shape, -jnp.inf, F32)
    for c in range(BAND_TQ // CHUNK):
        o_ref[c * CHUNK:(c + 1) * CHUNK, c * CHUNK:c * CHUNK + BAND] = b_ref[...]
    col = lax.broadcasted_iota(I32, o_ref.shape, 1)
    hidden = (n_var - 1 - v) * BAND_TQ
    o_ref[...] = jnp.where(col >= hidden, o_ref[...], -jnp.inf)


def _band_tile_bias(bias):
    nh = bias.shape[0]
    n_var = BAND_WINDOW // BAND_TQ + 1
    return pl.pallas_call(
        functools.partial(_tile_bias_kernel, n_var=n_var),
        grid=(nh, n_var),
        in_specs=[pl.BlockSpec((None, CHUNK, BAND), lambda h, v: (h, 0, 0))],
        out_specs=pl.BlockSpec((None, None, BAND_TQ, BAND_KEYS), lambda h, v: (h, v, 0, 0)),
        out_shape=jax.ShapeDtypeStruct((nh, n_var, BAND_TQ, BAND_KEYS), F32),
        compiler_params=_params("parallel", "arbitrary"),
        name="band_tile_bias",
    ), n_var


def _band_tile_kernel(q_ref, *refs):
    n_kb = BAND_KEYS // BAND_TQ
    k_refs, v_refs = refs[:n_kb], refs[n_kb:2 * n_kb]
    bias_ref, o_ref = refs[2 * n_kb], refs[2 * n_kb + 1]
    q = q_ref[...]
    kcat = jnp.concatenate([r[...] for r in k_refs], axis=0)
    vcat = jnp.concatenate([r[...] for r in v_refs], axis=0)
    outs = []
    for h in range(HEADS_PER_SLAB):
        qm = jnp.where(_head_mask(h), q, jnp.zeros_like(q))
        s = lax.dot_general(qm, kcat, (((1,), (1,)), ((), ())),
                            preferred_element_type=F32) + bias_ref[h]
        m = jnp.max(s, axis=-1, keepdims=True)
        p = jnp.exp(s - m)
        l = jnp.sum(p, axis=-1, keepdims=True)
        pv = jnp.dot(p.astype(BF16), vcat, preferred_element_type=F32)
        outs.append(pv / l)
    o_ref[...] = jnp.where(_head_mask(0), outs[0], outs[1])


def _band_attention_prompt(qkv3, q_col0, k_col0, v_col0, tile_bias, n_var):
    b, t, _ = qkv3.shape
    nslab = tile_bias.shape[0] // HEADS_PER_SLAB
    n_kb = BAND_KEYS // BAND_TQ
    blk = (None, BAND_TQ, LANES)

    def kv_spec(col0, back):
        return pl.BlockSpec(blk, lambda hp, bi, j: (bi, jnp.maximum(j - back, 0), col0 + hp))

    in_specs = [pl.BlockSpec(blk, lambda hp, bi, j: (bi, j, q_col0 + hp))]
    in_specs += [kv_spec(k_col0, n_kb - 1 - i) for i in range(n_kb)]
    in_specs += [kv_spec(v_col0, n_kb - 1 - i) for i in range(n_kb)]
    in_specs += [pl.BlockSpec((HEADS_PER_SLAB, None, BAND_TQ, BAND_KEYS),
                              lambda hp, bi, j: (hp, jnp.minimum(j, n_var - 1), 0, 0))]
    return pl.pallas_call(
        _band_tile_kernel,
        grid=(nslab, b, t // BAND_TQ),
        in_specs=in_specs,
        out_specs=pl.BlockSpec(blk, lambda hp, bi, j: (bi, j, hp)),
        out_shape=jax.ShapeDtypeStruct((b, t, nslab * LANES), F32),
        compiler_params=_params("parallel", "parallel", "arbitrary"),
        name="band_attention_prompt",
    )(qkv3, *([qkv3] * (2 * n_kb)), tile_bias)


def _band_chunk_kernel(q_ref, kp_ref, kc_ref, vp_ref, vc_ref, bias_ref, o_ref):
    qc = q_ref[...]
    kwin = jnp.concatenate([kp_ref[...].astype(BF16), kc_ref[...]], axis=0)
    vwin = jnp.concatenate([vp_ref[...].astype(BF16), vc_ref[...]], axis=0)
    outs = []
    for h in range(HEADS_PER_SLAB):
        qm = jnp.where(_head_mask(h), qc, jnp.zeros_like(qc))
        s = lax.dot_general(qm, kwin, (((1,), (1,)), ((), ())),
                            preferred_element_type=F32) + bias_ref[h]
        m = jnp.max(s, axis=-1, keepdims=True)
        p = jnp.exp(s - m)
        l = jnp.sum(p, axis=-1, keepdims=True)
        pv = jnp.dot(p.astype(BF16), vwin, preferred_element_type=F32)
        outs.append(pv / l)
    o_ref[...] = jnp.where(_head_mask(0), outs[0], outs[1])


def _band_attention_sample(qkv3, q_col0, k_col0, v_col0, cache_k, cache_v, bias):
    b, t, _ = qkv3.shape
    nslab = bias.shape[0] // HEADS_PER_SLAB
    new = lambda col0: pl.BlockSpec((None, CHUNK, LANES), lambda bi, hp: (bi, 0, col0 + hp))
    old = pl.BlockSpec((None, BAND_WINDOW, LANES), lambda bi, hp: (bi, 0, hp))
    return pl.pallas_call(
        _band_chunk_kernel,
        grid=(b, nslab),
        in_specs=[new(q_col0), old, new(k_col0), old, new(v_col0),
                  pl.BlockSpec((HEADS_PER_SLAB, CHUNK, BAND), lambda bi, hp: (hp, 0, 0))],
        out_specs=pl.BlockSpec((None, CHUNK, LANES), lambda bi, hp: (bi, 0, hp)),
        out_shape=jax.ShapeDtypeStruct((b, t, nslab * LANES), F32),
        compiler_params=_params("parallel", "parallel"),
        name="band_attention_sample",
    )(qkv3, cache_k, qkv3, cache_v, qkv3, bias)


def _sb_kernel(q_ref, k_ref, v_ref, o_ref, qm_ref, nu_ref, d_ref, tot_ref, acc_ref, run_ref,
               *, q_base):
    i = pl.program_id(2)
    tq = q_ref.shape[0]
    tk = SB_TK
    qpos0 = q_base + i * tq
    n_full = qpos0 // tk
    n_masked = max(1, tq // tk)

    q = q_ref[...]
    for h in range(HEADS_PER_SLAB):
        qm_ref[h] = jnp.where(_head_mask(h), q, jnp.zeros_like(q))
    jj = lax.broadcasted_iota(I32, (tk, tk), 0)
    ss = lax.broadcasted_iota(I32, (tk, tk), 1)
    nu_ref[...] = jnp.where(jj >= ss, -1.0, 0.0).astype(BF16)
    acc_ref[...] = jnp.zeros_like(acc_ref)
    run_ref[...] = jnp.zeros_like(run_ref)

    def stage_a(kb, masked):
        k0 = pl.multiple_of(kb * tk, tk)
        kblk = k_ref[pl.ds(k0, tk), :]
        if masked:
            qpos = qpos0 + lax.broadcasted_iota(I32, (tq, tk), 0)
            kpos = k0 + lax.broadcasted_iota(I32, (tq, tk), 1)
            vis = kpos < qpos
        for h in range(HEADS_PER_SLAB):
            z = lax.dot_general(qm_ref[h], kblk, (((1,), (1,)), ((), ())),
                                preferred_element_type=F32)
            neg_abs = lax.bitcast_convert_type(
                lax.bitcast_convert_type(z, jnp.uint32) | jnp.uint32(0x80000000), F32)
            sp = jnp.maximum(z, 0.0) + jnp.log(1.0 + jnp.exp(neg_abs))
            if masked:
                sp = jnp.where(vis, sp, 0.0)
            after = jnp.dot(sp.astype(BF16), nu_ref[...], preferred_element_type=F32)
            d = z + after
            if masked:
                d = jnp.where(vis, d, -jnp.inf)
            d_ref[h] = d
            tot_ref[h] = after[:, 0:1]

    def stage_b(kb):
        k0 = pl.multiple_of(kb * tk, tk)
        vblk = v_ref[pl.ds(k0, tk), :]
        for h in range(HEADS_PER_SLAB):
            run = run_ref[h]
            a = jnp.exp(d_ref[h] + run)
            acc_ref[h] += jnp.dot(a.astype(BF16), vblk, preferred_element_type=F32)
            run_ref[h] = run + tot_ref[h]

    newest = n_full + n_masked - 1
    stage_a(newest, True)
    for m in range(1, n_masked):
        stage_b(newest - m + 1)
        stage_a(newest - m, True)

    def step(kb):
        stage_b(kb + 1)
        stage_a(kb, False)

    def body(n, carry):
        for u in range(SB_UNROLL):
            step(n_full - 1 - SB_UNROLL * n - u)
        return carry

    lax.fori_loop(0, n_full // SB_UNROLL, body, 0)
    left = n_full % SB_UNROLL
    for u in range(SB_UNROLL - 1):
        @pl.when(left > u)
        def _():
            step(left - 1 - u)

    stage_b(0)
    o_ref[...] = jnp.where(_head_mask(0), acc_ref[0], acc_ref[1])


def _sb_attention(q_arr, q_col0, k_arr, k_col0, v_arr, v_col0, nslab, tq, q_base):
    b, t, _ = q_arr.shape
    tkeys = k_arr.shape[1]
    assert q_base % SB_TK == 0 and tkeys % SB_TK == 0 and t % tq == 0
    assert tq % SB_TK == 0 or SB_TK % tq == 0
    assert q_base + t <= tkeys
    return pl.pallas_call(
        functools.partial(_sb_kernel, q_base=q_base),
        grid=(b, nslab, t // tq),
        in_specs=[
            pl.BlockSpec((None, tq, LANES), lambda bi, hp, i: (bi, i, q_col0 + hp)),
            pl.BlockSpec((None, tkeys, LANES), lambda bi, hp, i: (bi, 0, k_col0 + hp)),
            pl.BlockSpec((None, tkeys, LANES), lambda bi, hp, i: (bi, 0, v_col0 + hp)),
        ],
        out_specs=pl.BlockSpec((None, tq, LANES), lambda bi, hp, i: (bi, i, hp)),
        out_shape=jax.ShapeDtypeStruct((b, t, nslab * LANES), F32),
        scratch_shapes=[pltpu.VMEM((HEADS_PER_SLAB, tq, LANES), BF16),
                        pltpu.VMEM((SB_TK, SB_TK), BF16),
                        pltpu.VMEM((HEADS_PER_SLAB, tq, SB_TK), F32),
                        pltpu.VMEM((HEADS_PER_SLAB, tq, 1), F32),
                        pltpu.VMEM((HEADS_PER_SLAB, tq, LANES), F32),
                        pltpu.VMEM((HEADS_PER_SLAB, tq, 1), F32)],
        compiler_params=_params("parallel", "parallel", "arbitrary"),
        name="stick_breaking",
    )(q_arr, k_arr, v_arr)


def _split_bf16(x):
    hi = x.astype(BF16)
    return hi, (x - hi.astype(F32)).astype(BF16)


def _merge_kernel(x_ref, oa_ref, ob_ref, ga_ref, gb_ref, wo_ref, gf_ref, wr_ref, br_ref,
                  x1_ref, xn_ref, idx_ref, w_ref, rank_ref, cnt_ref, seen_ref):
    @pl.when(pl.program_id(0) == 0)
    def _():
        seen_ref[...] = jnp.zeros_like(seen_ref)

    o = jnp.concatenate([_rms(oa_ref[...], ga_ref[...]), _rms(ob_ref[...], gb_ref[...])],
                        axis=-1).astype(BF16)
    x1 = x_ref[...] + jnp.dot(o, wo_ref[...], preferred_element_type=F32)
    x1_ref[...] = x1
    xn = _rms(x1, gf_ref[...])
    xn_ref[...] = xn
    xh, xl = _split_bf16(xn)
    wh, wl = _split_bf16(wr_ref[...])
    logits = (jnp.dot(xh, wh, preferred_element_type=F32)
              + jnp.dot(xl, wh, preferred_element_type=F32)
              + jnp.dot(xh, wl, preferred_element_type=F32)) + br_ref[...]
    tm, ne = logits.shape
    lane = lax.broadcasted_iota(I32, (tm, ne), 1)
    slot = lax.broadcasted_iota(I32, (tm, TOP_K), 1)
    work = logits
    top = None
    denom = None
    sels = []
    idx = jnp.zeros((tm, TOP_K), I32)
    wts = jnp.zeros((tm, TOP_K), F32)
    for r in range(TOP_K):
        m = jnp.max(work, axis=-1, keepdims=True)
        first = jnp.min(jnp.where(work == m, lane, ne), axis=-1, keepdims=True)
        sel = lane == first
        if top is None:
            top = m
        e = jnp.exp(m - top)
        denom = e if denom is None else denom + e
        idx = jnp.where(slot == r, first, idx)
        wts = jnp.where(slot == r, e, wts)
        sels.append(sel)
        work = jnp.where(sel, -jnp.inf, work)
    idx_ref[...] = idx
    w_ref[...] = wts / denom
    chosen = sels[0] | sels[1] | sels[2] | sels[3]
    onehot = jnp.where(chosen, 1.0, 0.0)
    rr = lax.broadcasted_iota(I32, (tm, tm), 0)
    cc = lax.broadcasted_iota(I32, (tm, tm), 1)
    lower = jnp.where(cc < rr, 1.0, 0.0).astype(BF16)
    before = jnp.dot(lower, onehot.astype(BF16), preferred_element_type=F32) + seen_ref[...]
    rank = jnp.zeros((tm, TOP_K), I32)
    for r in range(TOP_K):
        rk = jnp.sum(jnp.where(sels[r], before, 0.0), axis=-1, keepdims=True)
        rank = jnp.where(slot == r, rk.astype(I32), rank)
    rank_ref[...] = rank
    seen = seen_ref[...] + jnp.sum(onehot, axis=0, keepdims=True)
    seen_ref[...] = seen
    cnt_ref[...] = seen.astype(I32)


def _merge(x2d, oa, ob, ga, gb, wo_bf, gf, w_router, b_router):
    n, dm = x2d.shape
    da = oa.shape[1]
    ne = w_router.shape[1]
    tm = min(ROW_TILE, n)
    row = lambda i: (i, 0)
    fixed = lambda i: (0, 0)
    return pl.pallas_call(
        _merge_kernel,
        grid=(n // tm,),
        in_specs=[pl.BlockSpec((tm, dm), row),
                  pl.BlockSpec((tm, da), row),
                  pl.BlockSpec((tm, da), row),
                  pl.BlockSpec((1, da), fixed),
                  pl.BlockSpec((1, da), fixed),
                  pl.BlockSpec((2 * da, dm), fixed),
                  pl.BlockSpec((1, dm), fixed),
                  pl.BlockSpec((dm, ne), fixed),
                  pl.BlockSpec((1, ne), fixed)],
        out_specs=[pl.BlockSpec((tm, dm), row),
                   pl.BlockSpec((tm, dm), row),
                   pl.BlockSpec((tm, TOP_K), row),
                   pl.BlockSpec((tm, TOP_K), row),
                   pl.BlockSpec((tm, TOP_K), row),
                   pl.BlockSpec((1, ne), fixed)],
        out_shape=[jax.ShapeDtypeStruct((n, dm), F32),
                   jax.ShapeDtypeStruct((n, dm), F32),
                   jax.ShapeDtypeStruct((n, TOP_K), I32),
                   jax.ShapeDtypeStruct((n, TOP_K), F32),
                   jax.ShapeDtypeStruct((n, TOP_K), I32),
                   jax.ShapeDtypeStruct((1, ne), I32)],
        scratch_shapes=[pltpu.VMEM((1, ne), F32)],
        compiler_params=_params("arbitrary"),
        name="merge_router",
    )(x2d, oa, ob, ga.reshape(1, da), gb.reshape(1, da), wo_bf, gf.reshape(1, dm),
      w_router, b_router.reshape(1, ne))


def _row_copy(src_ref, src_row, dst_ref, dst_row, sem):
    return pltpu.make_async_copy(src_ref.at[pl.ds(src_row, 1), :],
                                 dst_ref.at[pl.ds(dst_row, 1), :], sem)


def _dispatch_kernel(pos_ref, x_ref, init_ref, xs_ref, sem):
    del init_ref
    tm = x_ref.shape[0]
    base = pl.program_id(0) * (tm * TOP_K)

    def start(n, carry):
        for r in range(TOP_K):
            _row_copy(x_ref, n, xs_ref, pos_ref[base + n * TOP_K + r], sem).start()
        return carry

    lax.fori_loop(0, tm, start, 0)

    def wait(n, carry):
        for r in range(TOP_K):
            _row_copy(x_ref, 0, xs_ref, 0, sem).wait()
        return carry

    lax.fori_loop(0, tm, wait, 0)


def _dispatch(pos_flat, xn, n_slots):
    n, dm = xn.shape
    tm = min(ROUTE_TILE, n)
    init = jnp.zeros((n_slots, dm), xn.dtype)
    return pl.pallas_call(
        _dispatch_kernel,
        grid_spec=pltpu.PrefetchScalarGridSpec(
            num_scalar_prefetch=1,
            grid=(n // tm,),
            in_specs=[pl.BlockSpec((tm, dm), lambda i, pos: (i, 0)),
                      pl.BlockSpec(memory_space=pl.ANY)],
            out_specs=pl.BlockSpec(memory_space=pl.ANY),
            scratch_shapes=[pltpu.SemaphoreType.DMA]),
        out_shape=jax.ShapeDtypeStruct((n_slots, dm), xn.dtype),
        input_output_aliases={2: 0},
        compiler_params=_params("arbitrary"),
        name="moe_dispatch",
    )(pos_flat, xn, init)


def _ffn_kernel(te_ref, tv_ref, xs_ref, wg_ref, bg_ref, wu_ref, bu_ref, wd_ref, bd_ref,
                y_ref, wgb_ref, wub_ref, wdb_ref):
    t = pl.program_id(0)
    new_expert = (t == 0) | (te_ref[t] != te_ref[jnp.maximum(t - 1, 0)])

    @pl.when(new_expert)
    def _():
        wgb_ref[...] = wg_ref[...].astype(BF16)
        wub_ref[...] = wu_ref[...].astype(BF16)
        wdb_ref[...] = wd_ref[...].astype(BF16)

    @pl.when(tv_ref[t] != 0)
    def _():
        x = xs_ref[...].astype(BF16)
        g = jnp.minimum(jnp.dot(x, wgb_ref[...], preferred_element_type=F32) + bg_ref[...],
                        SWIGLU_LIMIT)
        u = jnp.clip(jnp.dot(x, wub_ref[...], preferred_element_type=F32) + bu_ref[...],
                     -SWIGLU_LIMIT, SWIGLU_LIMIT)
        h = g * jax.nn.sigmoid(SWIGLU_ALPHA * g) * (u + 1.0)
        y_ref[...] = jnp.dot(h.astype(BF16), wdb_ref[...], preferred_element_type=F32) + bd_ref[...]

    @pl.when(tv_ref[t] == 0)
    def _():
        y_ref[...] = jnp.zeros_like(y_ref)


def _expert_ffn(tile_expert, tile_valid, xs, wg, bg, wu, bu, wd, bd):
    n_slots, dm = xs.shape
    ne, _, dff = wg.shape
    tm = MOE_TILE
    row = lambda t, te, tv: (t, 0)
    exp3 = lambda t, te, tv: (te[t], 0, 0)
    return pl.pallas_call(
        _ffn_kernel,
        grid_spec=pltpu.PrefetchScalarGridSpec(
            num_scalar_prefetch=2,
            grid=(n_slots // tm,),
            in_specs=[pl.BlockSpec((tm, dm), row),
                      pl.BlockSpec((None, dm, dff), exp3),
                      pl.BlockSpec((None, 1, dff), exp3),
                      pl.BlockSpec((None, dm, dff), exp3),
                      pl.BlockSpec((None, 1, dff), exp3),
                      pl.BlockSpec((None, dff, dm), exp3),
                      pl.BlockSpec((None, 1, dm), exp3)],
            out_specs=pl.BlockSpec((tm, dm), row),
            scratch_shapes=[pltpu.VMEM((dm, dff), BF16),
                            pltpu.VMEM((dm, dff), BF16),
                            pltpu.VMEM((dff, dm), BF16)]),
        out_shape=jax.ShapeDtypeStruct((n_slots, dm), F32),
        compiler_params=_params("arbitrary"),
        name="moe_ffn",
    )(tile_expert, tile_valid, xs, wg, bg.reshape(ne, 1, dff), wu, bu.reshape(ne, 1, dff),
      wd, bd.reshape(ne, 1, dm))


def _combine_kernel(pos_ref, ys_ref, w_ref, x1_ref, g_ref, o_ref, buf_ref, sem):
    tm = x1_ref.shape[0]
    base = pl.program_id(0) * (tm * TOP_K)

    def start(n, carry):
        for r in range(TOP_K):
            _row_copy(ys_ref, pos_ref[base + n * TOP_K + r], buf_ref.at[r], n, sem).start()
        return carry

    lax.fori_loop(0, tm, start, 0)

    def wait(n, carry):
        for r in range(TOP_K):
            _row_copy(ys_ref, 0, buf_ref.at[r], 0, sem).wait()
        return carry

    lax.fori_loop(0, tm, wait, 0)
    w = w_ref[...]
    slot = lax.broadcasted_iota(I32, w.shape, 1)
    y = x1_ref[...]
    for r in range(TOP_K):
        wr = jnp.sum(jnp.where(slot == r, w, 0.0), axis=-1, keepdims=True)
        y = y + wr * buf_ref[r]
    o_ref[...] = _rms(y, g_ref[...])


def _combine(pos_flat, ys, wts, x1, gfin):
    n, dm = x1.shape
    tm = min(ROUTE_TILE, n)
    return pl.pallas_call(
        _combine_kernel,
        grid_spec=pltpu.PrefetchScalarGridSpec(
            num_scalar_prefetch=1,
            grid=(n // tm,),
            in_specs=[pl.BlockSpec(memory_space=pl.ANY),
                      pl.BlockSpec((tm, TOP_K), lambda i, pos: (i, 0)),
                      pl.BlockSpec((tm, dm), lambda i, pos: (i, 0)),
                      pl.BlockSpec((1, dm), lambda i, pos: (0, 0))],
            out_specs=pl.BlockSpec((tm, dm), lambda i, pos: (i, 0)),
            scratch_shapes=[pltpu.VMEM((TOP_K, tm, dm), F32),
                            pltpu.SemaphoreType.DMA]),
        out_shape=jax.ShapeDtypeStruct((n, dm), F32),
        compiler_params=_params("arbitrary"),
        name="moe_combine",
    )(pos_flat, ys, wts, x1, gfin.reshape(1, dm))


def _moe(xn, x1, idx, wts, rank, counts, wg, bg, wu, bu, wd, bd, gfin):
    n, dm = xn.shape
    ne = wg.shape[0]
    tm = MOE_TILE
    n_tiles = (n * TOP_K + ne * (tm - 1)) // tm
    cnt = counts.reshape(ne)
    tiles_per = (cnt + tm - 1) // tm
    tile_end = jnp.cumsum(tiles_per)
    start_row = (tile_end - tiles_per) * tm
    pos = (start_row[idx] + rank).reshape(n * TOP_K).astype(I32)
    tiles = jnp.arange(n_tiles, dtype=I32)
    tile_expert = jnp.minimum(jnp.sum((tiles[:, None] >= tile_end[None, :]).astype(I32), axis=1),
                              ne - 1).astype(I32)
    tile_valid = (tiles < tile_end[ne - 1]).astype(I32)
    xs = _dispatch(pos, xn, n_tiles * tm)
    ys = _expert_ffn(tile_expert, tile_valid, xs, wg, bg, wu, bu, wd, bd)
    return _combine(pos, ys, wts, x1, gfin)


def kernel(x_prompt, x_sample, cache_a_k, cache_a_v, cache_b_k, cache_b_v, norm_attn, w_in,
           rel_bias, g_out_a, g_out_b, w_out, norm_ffn, w_router, b_router, w_gate, b_gate,
           w_up, b_up, w_down, b_down, norm_final):
    depth = w_in.shape[0]
    assert depth == 1, "the MoE combine applies the final norm: single layer only"
    bp, sp, dm = x_prompt.shape
    bs, ts, _ = x_sample.shape
    past = cache_b_k.shape[2]
    wa = cache_a_k.shape[2]
    nh_a, hd = cache_a_k.shape[3], cache_a_k.shape[4]
    nh_b = cache_b_k.shape[3]
    da, db = nh_a * hd, nh_b * hd
    assert da == db and hd == HEAD_DIM and wa == BAND_WINDOW and ts == CHUNK
    slabs = da // LANES
    QA, KA, VA, QB, KB, VB = (c * slabs for c in range(6))
    l = 0

    xp2 = x_prompt.reshape(bp * sp, dm)
    xs2 = x_sample.reshape(bs * ts, dm)
    w_in_bf = w_in[l].astype(BF16)
    qkv_p, ka_p, va_p, kb_p, vb_p = _qkv_proj(xp2, norm_attn[l], w_in_bf)
    qkv_s, ka_s, va_s, kb_s, vb_s = _qkv_proj(xs2, norm_attn[l], w_in_bf)
    qkv_p3 = qkv_p.reshape(bp, sp, 6 * da)
    qkv_s3 = qkv_s.reshape(bs, ts, 6 * da)

    bias = _band_bias(rel_bias[l])
    tile_bias_call, n_var = _band_tile_bias(bias)
    tile_bias = tile_bias_call(bias)

    oa_p = _band_attention_prompt(qkv_p3, QA, KA, VA, tile_bias, n_var)
    ob_p = _sb_attention(qkv_p3, QB, qkv_p3, KB, qkv_p3, VB, slabs, tq=min(SB_TQ, sp), q_base=0)

    oa_s = _band_attention_sample(qkv_s3, QA, KA, VA, cache_a_k[l].reshape(bs, wa, da),
                                  cache_a_v[l].reshape(bs, wa, da), bias)
    tk_all = -(-(past + ts) // SB_TK) * SB_TK
    pad = jnp.zeros((bs, tk_all - past - ts, db), BF16)
    kb_all = jnp.concatenate([cache_b_k[l].reshape(bs, past, db).astype(BF16),
                              qkv_s3[:, :, KB * LANES:KB * LANES + db], pad], axis=1)
    vb_all = jnp.concatenate([cache_b_v[l].reshape(bs, past, db).astype(BF16),
                              qkv_s3[:, :, VB * LANES:VB * LANES + db], pad], axis=1)
    ob_s = _sb_attention(qkv_s3, QB, kb_all, 0, vb_all, 0, slabs, tq=ts, q_base=past)

    x_all = jnp.concatenate([xp2, xs2], axis=0)
    oa_all = jnp.concatenate([oa_p.reshape(bp * sp, da), oa_s.reshape(bs * ts, da)], axis=0)
    ob_all = jnp.concatenate([ob_p.reshape(bp * sp, db), ob_s.reshape(bs * ts, db)], axis=0)
    x1, xn2, idx, wts, rank, counts = _merge(
        x_all, oa_all, ob_all, g_out_a[l], g_out_b[l], w_out[l].astype(BF16), norm_ffn[l],
        w_router[l], b_router[l])
    y = _moe(xn2, x1, idx, wts, rank, counts, w_gate[l], b_gate[l], w_up[l], b_up[l],
             w_down[l], b_down[l], norm_final)

    y_prompt = y[:bp * sp].reshape(bp, sp, dm)
    y_sample = y[bp * sp:].reshape(bs, ts, dm)
    wp = min(BAND_WINDOW, sp)
    heads_a = lambda t, b_, n_: t.reshape(b_, n_, nh_a, hd)
    heads_b = lambda t, b_, n_: t.reshape(b_, n_, nh_b, hd)
    new_a_k_prompt = heads_a(ka_p, bp, sp)[:, sp - wp:][None]
    new_a_v_prompt = heads_a(va_p, bp, sp)[:, sp - wp:][None]
    new_b_k_prompt = heads_b(kb_p, bp, sp)[None]
    new_b_v_prompt = heads_b(vb_p, bp, sp)[None]
    new_a_k_sample = jnp.concatenate([cache_a_k[l], heads_a(ka_s, bs, ts)], axis=1)[:, ts:][None]
    new_a_v_sample = jnp.concatenate([cache_a_v[l], heads_a(va_s, bs, ts)], axis=1)[:, ts:][None]
    new_b_k_sample = heads_b(kb_s, bs, ts)[None]
    new_b_v_sample = heads_b(vb_s, bs, ts)[None]
    return (y_prompt, y_sample, new_a_k_prompt, new_a_v_prompt, new_b_k_prompt, new_b_v_prompt,
            new_a_k_sample, new_a_v_sample, new_b_k_sample, new_b_v_sample)
```

```python
import functools

import jax
import jax.numpy as jnp
from jax import lax
from jax.experimental import pallas as pl
from jax.experimental.pallas import tpu as pltpu

F32 = jnp.float32
BF16 = jnp.bfloat16
I32 = jnp.int32

CHUNK = 64
LEFT_CHUNKS = 8
BAND_WINDOW = LEFT_CHUNKS * CHUNK
BAND = BAND_WINDOW + CHUNK
HEAD_DIM = 64
HEADS_PER_SLAB = 2
LANES = 128
REL_CLIP = 128
TOP_K = 4
SWIGLU_LIMIT = 7.0
SWIGLU_ALPHA = 1.702
EPS = 1e-6
SCALE = HEAD_DIM ** -0.5

ROW_TILE = 512
BAND_TQ = 256
BAND_KEYS = BAND_TQ + BAND_WINDOW
BAND_VARIANTS = BAND_WINDOW // BAND_TQ + 1
SB_TQ = 512
SB_TK = 256
SB_UNROLL = 2
SB_DEAD = -104.0
MOE_TILE = 512
ROUTE_TILE = 256
DMA_THREADS = 2
VMEM_LIMIT = 56 * 1024 * 1024


def _params(*sem):
    return pltpu.CompilerParams(dimension_semantics=sem, vmem_limit_bytes=VMEM_LIMIT)


def _rms(x, g):
    ms = jnp.mean(x * x, axis=-1, keepdims=True)
    return (x * lax.rsqrt(ms + EPS)) * g


def _head_mask(h):
    lane = lax.broadcasted_iota(I32, (1, LANES), 1)
    return (lane >= h * HEAD_DIM) & (lane < (h + 1) * HEAD_DIM)


def _qkv_kernel(x_ref, g_ref, w_ref, qkv_ref, ka_ref, va_ref, kb_ref, vb_ref):
    xb = _rms(x_ref[...], g_ref[...]).astype(BF16)
    d = ka_ref.shape[-1]
    f32_out = {1: ka_ref, 2: va_ref, 4: kb_ref, 5: vb_ref}
    for c in range(6):
        r = jnp.dot(xb, w_ref[:, c * d:(c + 1) * d], preferred_element_type=F32)
        if c in f32_out:
            f32_out[c][...] = r
        else:
            r = r * SCALE
        qkv_ref[:, c * d:(c + 1) * d] = r.astype(BF16)


def _qkv_proj(x2d, g, w_bf):
    n, dm = x2d.shape
    d3 = w_bf.shape[1]
    d = d3 // 6
    tm = min(ROW_TILE, n)
    row = lambda i: (i, 0)
    fixed = lambda i: (0, 0)
    return pl.pallas_call(
        _qkv_kernel,
        grid=(n // tm,),
        in_specs=[pl.BlockSpec((tm, dm), row),
                  pl.BlockSpec((1, dm), fixed),
                  pl.BlockSpec((dm, d3), fixed)],
        out_specs=[pl.BlockSpec((tm, d3), row)] + [pl.BlockSpec((tm, d), row)] * 4,
        out_shape=[jax.ShapeDtypeStruct((n, d3), BF16)]
        + [jax.ShapeDtypeStruct((n, d), F32)] * 4,
        compiler_params=_params("parallel"),
        name="qkv_proj",
    )(x2d, g.reshape(1, dm), w_bf)


def _bias_kernel(tab_ref, o_ref):
    h = pl.program_id(0)
    i = lax.broadcasted_iota(I32, (CHUNK, BAND), 0)
    j = lax.broadcasted_iota(I32, (CHUNK, BAND), 1)
    idx = jnp.clip(BAND_WINDOW + i - j, -REL_CLIP, REL_CLIP) + REL_CLIP
    n_tab = 2 * REL_CLIP + 1

    def body(t, acc):
        return jnp.where(idx == t, tab_ref[h, t], acc)

    init = jnp.full((CHUNK, BAND), tab_ref[h, n_tab - 1], F32)
    o_ref[...] = lax.fori_loop(0, n_tab - 1, body, init)


def _band_bias(table):
    nh = table.shape[0]
    return pl.pallas_call(
        _bias_kernel,
        grid=(nh,),
        in_specs=[pl.BlockSpec(memory_space=pltpu.SMEM)],
        out_specs=pl.BlockSpec((None, CHUNK, BAND), lambda h: (h, 0, 0)),
        out_shape=jax.ShapeDtypeStruct((nh, CHUNK, BAND), F32),
        compiler_params=_params("arbitrary"),
        name="band_bias",
    )(table)


def _tile_bias_kernel(b_ref, o_ref):
    v = pl.program_id(1)
    o_ref[...] = jnp.full(o_ref.shape, -jnp.inf, F32)
    for c in range(BAND_TQ // CHUNK):
        o_ref[c * CHUNK:(c + 1) * CHUNK, c * CHUNK:c * CHUNK + BAND] = b_ref[...]
    col = lax.broadcasted_iota(I32, o_ref.shape, 1)
    hidden = (BAND_VARIANTS - 1 - v) * BAND_TQ
    o_ref[...] = jnp.where(col >= hidden, o_ref[...], -jnp.inf)


def _band_tile_bias(bias):
    nh = bias.shape[0]
    return pl.pallas_call(
        _tile_bias_kernel,
        grid=(nh, BAND_VARIANTS),
        in_specs=[pl.BlockSpec((None, CHUNK, BAND), lambda h, v: (h, 0, 0))],
        out_specs=pl.BlockSpec((None, None, BAND_TQ, BAND_KEYS), lambda h, v: (h, v, 0, 0)),
        out_shape=jax.ShapeDtypeStruct((nh, BAND_VARIANTS, BAND_TQ, BAND_KEYS), F32),
        compiler_params=_params("parallel", "arbitrary"),
        name="band_tile_bias",
    )(bias)


def _band_tile_kernel(q_ref, *refs):
    n_kb = BAND_KEYS // BAND_TQ
    k_refs, v_refs = refs[:n_kb], refs[n_kb:2 * n_kb]
    bias_ref, o_ref = refs[2 * n_kb], refs[2 * n_kb + 1]
    q = q_ref[...]
    kcat = jnp.concatenate([r[...] for r in k_refs], axis=0)
    vcat = jnp.concatenate([r[...] for r in v_refs], axis=0)
    outs = []
    for h in range(HEADS_PER_SLAB):
        qm = jnp.where(_head_mask(h), q, jnp.zeros_like(q))
        s = lax.dot_general(qm, kcat, (((1,), (1,)), ((), ())),
                            preferred_element_type=F32) + bias_ref[h]
        m = jnp.max(s, axis=-1, keepdims=True)
        p = jnp.exp(s - m)
        l = jnp.sum(p, axis=-1, keepdims=True)
        pv = jnp.dot(p.astype(BF16), vcat, preferred_element_type=F32)
        outs.append(pv / l)
    o_ref[...] = jnp.where(_head_mask(0), outs[0], outs[1])


def _band_attention_prompt(qkv3, q_col0, k_col0, v_col0, tile_bias):
    b, t, _ = qkv3.shape
    nslab = tile_bias.shape[0] // HEADS_PER_SLAB
    n_kb = BAND_KEYS // BAND_TQ
    blk = (None, BAND_TQ, LANES)

    def kv_spec(col0, back):
        return pl.BlockSpec(blk, lambda hp, bi, j: (bi, jnp.maximum(j - back, 0), col0 + hp))

    in_specs = [pl.BlockSpec(blk, lambda hp, bi, j: (bi, j, q_col0 + hp))]
    in_specs += [kv_spec(k_col0, n_kb - 1 - i) for i in range(n_kb)]
    in_specs += [kv_spec(v_col0, n_kb - 1 - i) for i in range(n_kb)]
    in_specs += [pl.BlockSpec((HEADS_PER_SLAB, None, BAND_TQ, BAND_KEYS),
                              lambda hp, bi, j: (hp, jnp.minimum(j, BAND_VARIANTS - 1), 0, 0))]
    return pl.pallas_call(
        _band_tile_kernel,
        grid=(nslab, b, t // BAND_TQ),
        in_specs=in_specs,
        out_specs=pl.BlockSpec(blk, lambda hp, bi, j: (bi, j, hp)),
        out_shape=jax.ShapeDtypeStruct((b, t, nslab * LANES), F32),
        compiler_params=_params("parallel", "parallel", "arbitrary"),
        name="band_attention_prompt",
    )(qkv3, *([qkv3] * (2 * n_kb)), tile_bias)


def _band_chunk_kernel(q_ref, kp_ref, kc_ref, vp_ref, vc_ref, bias_ref, o_ref):
    qc = q_ref[...]
    kwin = jnp.concatenate([kp_ref[...].astype(BF16), kc_ref[...]], axis=0)
    vwin = jnp.concatenate([vp_ref[...].astype(BF16), vc_ref[...]], axis=0)
    outs = []
    for h in range(HEADS_PER_SLAB):
        qm = jnp.where(_head_mask(h), qc, jnp.zeros_like(qc))
        s = lax.dot_general(qm, kwin, (((1,), (1,)), ((), ())),
                            preferred_element_type=F32) + bias_ref[h]
        m = jnp.max(s, axis=-1, keepdims=True)
        p = jnp.exp(s - m)
        l = jnp.sum(p, axis=-1, keepdims=True)
        pv = jnp.dot(p.astype(BF16), vwin, preferred_element_type=F32)
        outs.append(pv / l)
    o_ref[...] = jnp.where(_head_mask(0), outs[0], outs[1])


def _band_attention_sample(qkv3, q_col0, k_col0, v_col0, cache_k, cache_v, bias):
    b, t, _ = qkv3.shape
    nslab = bias.shape[0] // HEADS_PER_SLAB
    new = lambda col0: pl.BlockSpec((None, CHUNK, LANES), lambda bi, hp: (bi, 0, col0 + hp))
    old = pl.BlockSpec((None, BAND_WINDOW, LANES), lambda bi, hp: (bi, 0, hp))
    return pl.pallas_call(
        _band_chunk_kernel,
        grid=(b, nslab),
        in_specs=[new(q_col0), old, new(k_col0), old, new(v_col0),
                  pl.BlockSpec((HEADS_PER_SLAB, CHUNK, BAND), lambda bi, hp: (hp, 0, 0))],
        out_specs=pl.BlockSpec((None, CHUNK, LANES), lambda bi, hp: (bi, 0, hp)),
        out_shape=jax.ShapeDtypeStruct((b, t, nslab * LANES), F32),
        compiler_params=_params("parallel", "parallel"),
        name="band_attention_sample",
    )(qkv3, cache_k, qkv3, cache_v, qkv3, bias)


def _sb_kernel(q_ref, kt_ref, vt_ref, kf_ref, vf_ref, o_ref,
               qm_ref, nu_ref, d_ref, tot_ref, acc_ref, run_ref, *, q_base):
    i = pl.program_id(2)
    tq = q_ref.shape[0]
    tile_keys = kt_ref.shape[0]
    tk = SB_TK
    qpos0 = q_base + i * tq
    n_full = qpos0 // tk
    n_masked = -(-tile_keys // tk)

    q = q_ref[...]
    for h in range(HEADS_PER_SLAB):
        qm_ref[h] = jnp.where(_head_mask(h), q, jnp.zeros_like(q))
    jj = lax.broadcasted_iota(I32, (tk, tk), 0)
    ss = lax.broadcasted_iota(I32, (tk, tk), 1)
    nu_ref[...] = jnp.where(jj >= ss, -1.0, 0.0).astype(BF16)
    acc_ref[...] = jnp.zeros_like(acc_ref)
    run_ref[...] = jnp.zeros_like(run_ref)

    def tile_block(ref, m):
        rows = min(tk, tile_keys - m * tk)
        blk = ref[m * tk:m * tk + rows, :].astype(BF16)
        if rows < tk:
            blk = jnp.concatenate([blk, jnp.zeros((tk - rows, LANES), BF16)], axis=0)
        return blk

    def full_block(ref, kb):
        k0 = pl.multiple_of(kb * tk, tk)
        return ref[pl.ds(k0, tk), :].astype(BF16)

    def stage_a(kblk, tile_m):
        if tile_m is not None:
            row = lax.broadcasted_iota(I32, (tq, tk), 0)
            col = lax.broadcasted_iota(I32, (tq, tk), 1)
            vis = tile_m * tk + col < row
        for h in range(HEADS_PER_SLAB):
            z = lax.dot_general(qm_ref[h], kblk, (((1,), (1,)), ((), ())),
                                preferred_element_type=F32)
            sp = jnp.maximum(z, 0.0) + jnp.log(1.0 + jnp.exp(-jnp.abs(z)))
            if tile_m is not None:
                sp = jnp.where(vis, sp, 0.0)
            after = jnp.dot(sp.astype(BF16), nu_ref[...], preferred_element_type=F32)
            d = z + after
            if tile_m is not None:
                d = jnp.where(vis, d, -jnp.inf)
            d_ref[h] = d
            tot_ref[h] = after[:, 0:1]

    def stage_b(vblk):
        for h in range(HEADS_PER_SLAB):
            run = run_ref[h]
            a = jnp.exp(d_ref[h] + run)
            acc_ref[h] += jnp.dot(a.astype(BF16), vblk, preferred_element_type=F32)
            run_ref[h] = run + tot_ref[h]

    def step(kb):
        stage_b(full_block(vf_ref, kb + 1))
        stage_a(full_block(kf_ref, kb), None)

    def alive():
        r = jnp.maximum(run_ref[0], run_ref[1])
        return (jnp.max(r) >= SB_DEAD).astype(I32)

    stage_a(tile_block(kt_ref, n_masked - 1), n_masked - 1)
    for m in range(n_masked - 2, -1, -1):
        stage_b(tile_block(vt_ref, m + 1))
        stage_a(tile_block(kt_ref, m), m)

    @pl.when(n_full == 0)
    def _():
        stage_b(tile_block(vt_ref, 0))

    @pl.when(n_full > 0)
    def _():
        stage_b(tile_block(vt_ref, 0))
        stage_a(full_block(kf_ref, n_full - 1), None)

        def cond(c):
            kb, live = c
            return (kb >= SB_UNROLL - 1) & (live > 0)

        def body(c):
            kb, _ = c
            for u in range(SB_UNROLL):
                step(kb - u)
            return kb - SB_UNROLL, alive()

        kb, live = lax.while_loop(cond, body, (n_full - 2, alive()))
        for _ in range(SB_UNROLL - 1):
            go = (kb >= 0) & (live > 0)

            @pl.when(go)
            def _():
                step(kb)

            kb = jnp.where(go, kb - 1, kb)
            live = alive()

        @pl.when(live > 0)
        def _():
            stage_b(full_block(vf_ref, kb + 1))

    o_ref[...] = jnp.where(_head_mask(0), acc_ref[0], acc_ref[1])


def _sb_attention(q_arr, q_col0, kt_arr, kt_col0, vt_arr, vt_col0,
                  kf_arr, kf_col0, vf_arr, vf_col0, nslab, tq, q_base):
    b, t, _ = q_arr.shape
    tf = kf_arr.shape[1]
    assert q_base % SB_TK == 0 and t % tq == 0 and tf % SB_TK == 0
    assert tq % SB_TK == 0 or t == tq
    assert q_base + t - tq <= tf
    tile = lambda col0: pl.BlockSpec((None, tq, LANES), lambda bi, hp, i: (bi, i, col0 + hp))
    full = lambda col0: pl.BlockSpec((None, tf, LANES), lambda bi, hp, i: (bi, 0, col0 + hp))
    return pl.pallas_call(
        functools.partial(_sb_kernel, q_base=q_base),
        grid=(b, nslab, t // tq),
        in_specs=[tile(q_col0), tile(kt_col0), tile(vt_col0), full(kf_col0), full(vf_col0)],
        out_specs=pl.BlockSpec((None, tq, LANES), lambda bi, hp, i: (bi, i, hp)),
        out_shape=jax.ShapeDtypeStruct((b, t, nslab * LANES), F32),
        scratch_shapes=[pltpu.VMEM((HEADS_PER_SLAB, tq, LANES), BF16),
                        pltpu.VMEM((SB_TK, SB_TK), BF16),
                        pltpu.VMEM((HEADS_PER_SLAB, tq, SB_TK), F32),
                        pltpu.VMEM((HEADS_PER_SLAB, tq, 1), F32),
                        pltpu.VMEM((HEADS_PER_SLAB, tq, LANES), F32),
                        pltpu.VMEM((HEADS_PER_SLAB, tq, 1), F32)],
        compiler_params=_params("parallel", "parallel", "arbitrary"),
        name="stick_breaking",
    )(q_arr, kt_arr, vt_arr, kf_arr, vf_arr)


def _split_bf16(x):
    hi = x.astype(BF16)
    return hi, (x - hi.astype(F32)).astype(BF16)


def _merge_kernel(x_ref, oa_ref, ob_ref, ga_ref, gb_ref, wo_ref, gf_ref, wr_ref, br_ref,
                  seen0_ref, x1_ref, xn_ref, idx_ref, w_ref, rank_ref, cnt_ref, seen_ref):
    @pl.when(pl.program_id(0) == 0)
    def _():
        seen_ref[...] = seen0_ref[...].astype(F32)

    o = jnp.concatenate([_rms(oa_ref[...], ga_ref[...]), _rms(ob_ref[...], gb_ref[...])],
                        axis=-1).astype(BF16)
    x1 = x_ref[...] + jnp.dot(o, wo_ref[...], preferred_element_type=F32)
    x1_ref[...] = x1
    xn = _rms(x1, gf_ref[...])
    xn_ref[...] = xn
    xh, xl = _split_bf16(xn)
    wh, wl = _split_bf16(wr_ref[...])
    logits = (jnp.dot(xh, wh, preferred_element_type=F32)
              + jnp.dot(xl, wh, preferred_element_type=F32)
              + jnp.dot(xh, wl, preferred_element_type=F32)) + br_ref[...]
    tm, ne = logits.shape
    lane = lax.broadcasted_iota(I32, (tm, ne), 1)
    slot = lax.broadcasted_iota(I32, (tm, TOP_K), 1)
    work = logits
    top = None
    denom = None
    sels = []
    idx = jnp.zeros((tm, TOP_K), I32)
    wts = jnp.zeros((tm, TOP_K), F32)
    for r in range(TOP_K):
        m = jnp.max(work, axis=-1, keepdims=True)
        first = jnp.min(jnp.where(work == m, lane, ne), axis=-1, keepdims=True)
        sel = lane == first
        if top is None:
            top = m
        e = jnp.exp(m - top)
        denom = e if denom is None else denom + e
        idx = jnp.where(slot == r, first, idx)
        wts = jnp.where(slot == r, e, wts)
        sels.append(sel)
        work = jnp.where(sel, -jnp.inf, work)
    idx_ref[...] = idx
    w_ref[...] = wts / denom
    chosen = sels[0] | sels[1] | sels[2] | sels[3]
    onehot = jnp.where(chosen, 1.0, 0.0)
    rr = lax.broadcasted_iota(I32, (tm, tm), 0)
    cc = lax.broadcasted_iota(I32, (tm, tm), 1)
    lower = jnp.where(cc < rr, 1.0, 0.0).astype(BF16)
    before = jnp.dot(lower, onehot.astype(BF16), preferred_element_type=F32) + seen_ref[...]
    rank = jnp.zeros((tm, TOP_K), I32)
    for r in range(TOP_K):
        rk = jnp.sum(jnp.where(sels[r], before, 0.0), axis=-1, keepdims=True)
        rank = jnp.where(slot == r, rk.astype(I32), rank)
    rank_ref[...] = rank
    seen = seen_ref[...] + jnp.sum(onehot, axis=0, keepdims=True)
    seen_ref[...] = seen
    cnt_ref[...] = seen.astype(I32)


def _merge(x2d, oa, ob, ga, gb, wo_bf, gf, w_router, b_router, seen0):
    n, dm = x2d.shape
    da = oa.shape[1]
    ne = w_router.shape[1]
    tm = min(ROW_TILE, n)
    row = lambda i: (i, 0)
    fixed = lambda i: (0, 0)
    return pl.pallas_call(
        _merge_kernel,
        grid=(n // tm,),
        in_specs=[pl.BlockSpec((tm, dm), row),
                  pl.BlockSpec((tm, da), row),
                  pl.BlockSpec((tm, da), row),
                  pl.BlockSpec((1, da), fixed),
                  pl.BlockSpec((1, da), fixed),
                  pl.BlockSpec((2 * da, dm), fixed),
                  pl.BlockSpec((1, dm), fixed),
                  pl.BlockSpec((dm, ne), fixed),
                  pl.BlockSpec((1, ne), fixed),
                  pl.BlockSpec((1, ne), fixed)],
        out_specs=[pl.BlockSpec((tm, dm), row),
                   pl.BlockSpec((tm, dm), row),
                   pl.BlockSpec((tm, TOP_K), row),
                   pl.BlockSpec((tm, TOP_K), row),
                   pl.BlockSpec((tm, TOP_K), row),
                   pl.BlockSpec((1, ne), fixed)],
        out_shape=[jax.ShapeDtypeStruct((n, dm), F32),
                   jax.ShapeDtypeStruct((n, dm), F32),
                   jax.ShapeDtypeStruct((n, TOP_K), I32),
                   jax.ShapeDtypeStruct((n, TOP_K), F32),
                   jax.ShapeDtypeStruct((n, TOP_K), I32),
                   jax.ShapeDtypeStruct((1, ne), I32)],
        scratch_shapes=[pltpu.VMEM((1, ne), F32)],
        compiler_params=_params("arbitrary"),
        name="merge_router",
    )(x2d, oa, ob, ga.reshape(1, da), gb.reshape(1, da), wo_bf, gf.reshape(1, dm),
      w_router, b_router.reshape(1, ne), seen0)


def _row_copy(src_ref, src_row, dst_ref, dst_row, sem):
    return pltpu.make_async_copy(src_ref.at[pl.ds(src_row, 1), :],
                                 dst_ref.at[pl.ds(dst_row, 1), :], sem)


def _dispatch_kernel(pos_ref, xa_ref, xb_ref, init_ref, xs_ref, sem, *, tiles_a):
    del init_ref
    i = pl.program_id(0)
    tm = xa_ref.shape[0]
    base = i * (tm * TOP_K)

    def scatter_rows(x_ref):
        def start(n, carry):
            for r in range(TOP_K):
                _row_copy(x_ref, n, xs_ref, pos_ref[base + n * TOP_K + r], sem).start(
                    priority=r % DMA_THREADS)
            return carry

        lax.fori_loop(0, tm, start, 0)

        def wait(n, carry):
            for r in range(TOP_K):
                _row_copy(x_ref, 0, xs_ref, 0, sem).wait()
            return carry

        lax.fori_loop(0, tm, wait, 0)

    @pl.when(i < tiles_a)
    def _():
        scatter_rows(xa_ref)

    @pl.when(i >= tiles_a)
    def _():
        scatter_rows(xb_ref)


def _dispatch(pos_flat, xa, xb, n_slots):
    (na, dm), nb = xa.shape, xb.shape[0]
    tm = min(ROUTE_TILE, na, nb)
    tiles_a, tiles_b = na // tm, nb // tm
    init = jnp.zeros((n_slots, dm), xa.dtype)
    return pl.pallas_call(
        functools.partial(_dispatch_kernel, tiles_a=tiles_a),
        grid_spec=pltpu.PrefetchScalarGridSpec(
            num_scalar_prefetch=1,
            grid=(tiles_a + tiles_b,),
            in_specs=[pl.BlockSpec((tm, dm), lambda i, pos: (jnp.minimum(i, tiles_a - 1), 0)),
                      pl.BlockSpec((tm, dm), lambda i, pos: (jnp.maximum(i - tiles_a, 0), 0)),
                      pl.BlockSpec(memory_space=pl.ANY)],
            out_specs=pl.BlockSpec(memory_space=pl.ANY),
            scratch_shapes=[pltpu.SemaphoreType.DMA]),
        out_shape=jax.ShapeDtypeStruct((n_slots, dm), xa.dtype),
        input_output_aliases={3: 0},
        compiler_params=_params("arbitrary"),
        name="moe_dispatch",
    )(pos_flat, xa, xb, init)


def _ffn_kernel(te_ref, tv_ref, xs_ref, wg_ref, bg_ref, wu_ref, bu_ref, wd_ref, bd_ref,
                y_ref, wgb_ref, wub_ref, wdb_ref):
    t = pl.program_id(0)
    new_expert = (t == 0) | (te_ref[t] != te_ref[jnp.maximum(t - 1, 0)])

    @pl.when(new_expert)
    def _():
        wgb_ref[...] = wg_ref[...].astype(BF16)
        wub_ref[...] = wu_ref[...].astype(BF16)
        wdb_ref[...] = wd_ref[...].astype(BF16)

    @pl.when(tv_ref[t] != 0)
    def _():
        x = xs_ref[...].astype(BF16)
        g = jnp.minimum(jnp.dot(x, wgb_ref[...], preferred_element_type=F32) + bg_ref[...],
                        SWIGLU_LIMIT)
        u = jnp.clip(jnp.dot(x, wub_ref[...], preferred_element_type=F32) + bu_ref[...],
                     -SWIGLU_LIMIT, SWIGLU_LIMIT)
        h = g * jax.nn.sigmoid(SWIGLU_ALPHA * g) * (u + 1.0)
        y_ref[...] = jnp.dot(h.astype(BF16), wdb_ref[...], preferred_element_type=F32) + bd_ref[...]

    @pl.when(tv_ref[t] == 0)
    def _():
        y_ref[...] = jnp.zeros_like(y_ref)


def _expert_ffn(tile_expert, tile_valid, xs, wg, bg, wu, bu, wd, bd):
    n_slots, dm = xs.shape
    ne, _, dff = wg.shape
    tm = MOE_TILE
    row = lambda t, te, tv: (t, 0)
    exp3 = lambda t, te, tv: (te[t], 0, 0)
    return pl.pallas_call(
        _ffn_kernel,
        grid_spec=pltpu.PrefetchScalarGridSpec(
            num_scalar_prefetch=2,
            grid=(n_slots // tm,),
            in_specs=[pl.BlockSpec((tm, dm), row),
                      pl.BlockSpec((None, dm, dff), exp3),
                      pl.BlockSpec((None, 1, dff), exp3),
                      pl.BlockSpec((None, dm, dff), exp3),
                      pl.BlockSpec((None, 1, dff), exp3),
                      pl.BlockSpec((None, dff, dm), exp3),
                      pl.BlockSpec((None, 1, dm), exp3)],
            out_specs=pl.BlockSpec((tm, dm), row),
            scratch_shapes=[pltpu.VMEM((dm, dff), BF16),
                            pltpu.VMEM((dm, dff), BF16),
                            pltpu.VMEM((dff, dm), BF16)]),
        out_shape=jax.ShapeDtypeStruct((n_slots, dm), F32),
        compiler_params=_params("arbitrary"),
        name="moe_ffn",
    )(tile_expert, tile_valid, xs, wg, bg.reshape(ne, 1, dff), wu, bu.reshape(ne, 1, dff),
      wd, bd.reshape(ne, 1, dm))


def _combine_kernel(pos_ref, ys_ref, w_ref, x1_ref, g_ref, o_ref, buf_ref, sem, *, tile0):
    tm = x1_ref.shape[0]
    base = (tile0 + pl.program_id(0)) * (tm * TOP_K)

    def start(n, carry):
        for r in range(TOP_K):
            _row_copy(ys_ref, pos_ref[base + n * TOP_K + r], buf_ref.at[r], n, sem).start(
                priority=r % DMA_THREADS)
        return carry

    lax.fori_loop(0, tm, start, 0)

    def wait(n, carry):
        for r in range(TOP_K):
            _row_copy(ys_ref, 0, buf_ref.at[r], 0, sem).wait()
        return carry

    lax.fori_loop(0, tm, wait, 0)
    w = w_ref[...]
    slot = lax.broadcasted_iota(I32, w.shape, 1)
    y = x1_ref[...]
    for r in range(TOP_K):
        wr = jnp.sum(jnp.where(slot == r, w, 0.0), axis=-1, keepdims=True)
        y = y + wr * buf_ref[r]
    o_ref[...] = _rms(y, g_ref[...])


def _combine(pos_flat, row0, ys, wts, x1, gfin):
    n, dm = x1.shape
    tm = min(ROUTE_TILE, n)
    assert row0 % tm == 0
    return pl.pallas_call(
        functools.partial(_combine_kernel, tile0=row0 // tm),
        grid_spec=pltpu.PrefetchScalarGridSpec(
            num_scalar_prefetch=1,
            grid=(n // tm,),
            in_specs=[pl.BlockSpec(memory_space=pl.ANY),
                      pl.BlockSpec((tm, TOP_K), lambda i, pos: (i, 0)),
                      pl.BlockSpec((tm, dm), lambda i, pos: (i, 0)),
                      pl.BlockSpec((1, dm), lambda i, pos: (0, 0))],
            out_specs=pl.BlockSpec((tm, dm), lambda i, pos: (i, 0)),
            scratch_shapes=[pltpu.VMEM((TOP_K, tm, dm), F32),
                            pltpu.SemaphoreType.DMA]),
        out_shape=jax.ShapeDtypeStruct((n, dm), F32),
        compiler_params=_params("arbitrary"),
        name="moe_combine",
    )(pos_flat, ys, wts, x1, gfin.reshape(1, dm))


def _moe_plan(counts, idx_groups, rank_groups, n_rows):
    ne = counts.shape[-1]
    tm = MOE_TILE
    n_tiles = (n_rows * TOP_K + ne * (tm - 1)) // tm
    cnt = counts.reshape(ne)
    tiles_per = (cnt + tm - 1) // tm
    tile_end = jnp.cumsum(tiles_per)
    start_row = (tile_end - tiles_per) * tm
    pos = jnp.concatenate([(start_row[i] + r).reshape(-1)
                           for i, r in zip(idx_groups, rank_groups)]).astype(I32)
    tiles = jnp.arange(n_tiles, dtype=I32)
    tile_expert = jnp.minimum(jnp.sum((tiles[:, None] >= tile_end[None, :]).astype(I32), axis=1),
                              ne - 1).astype(I32)
    tile_valid = (tiles < tile_end[ne - 1]).astype(I32)
    return pos, tile_expert, tile_valid, n_tiles * tm


def kernel(x_prompt, x_sample, cache_a_k, cache_a_v, cache_b_k, cache_b_v, norm_attn, w_in,
           rel_bias, g_out_a, g_out_b, w_out, norm_ffn, w_router, b_router, w_gate, b_gate,
           w_up, b_up, w_down, b_down, norm_final):
    depth = w_in.shape[0]
    assert depth == 1, "the MoE combine applies the final norm: single layer only"
    bp, sp, dm = x_prompt.shape
    bs, ts, _ = x_sample.shape
    past = cache_b_k.shape[2]
    wa = cache_a_k.shape[2]
    nh_a, hd = cache_a_k.shape[3], cache_a_k.shape[4]
    nh_b = cache_b_k.shape[3]
    da, db = nh_a * hd, nh_b * hd
    ne = w_router.shape[-1]
    assert da == db and hd == HEAD_DIM and wa == BAND_WINDOW and ts == CHUNK
    slabs = da // LANES
    QA, KA, VA, QB, KB, VB = (c * slabs for c in range(6))
    l = 0
    n_p, n_s = bp * sp, bs * ts

    xp2 = x_prompt.reshape(n_p, dm)
    xs2 = x_sample.reshape(n_s, dm)
    w_in_bf = w_in[l].astype(BF16)
    qkv_p, ka_p, va_p, kb_p, vb_p = _qkv_proj(xp2, norm_attn[l], w_in_bf)
    qkv_s, ka_s, va_s, kb_s, vb_s = _qkv_proj(xs2, norm_attn[l], w_in_bf)
    qkv_p3 = qkv_p.reshape(bp, sp, 6 * da)
    qkv_s3 = qkv_s.reshape(bs, ts, 6 * da)

    bias = _band_bias(rel_bias[l])
    tile_bias = _band_tile_bias(bias)

    oa_p = _band_attention_prompt(qkv_p3, QA, KA, VA, tile_bias)
    ob_p = _sb_attention(qkv_p3, QB, qkv_p3, KB, qkv_p3, VB, qkv_p3, KB, qkv_p3, VB,
                         slabs, tq=min(SB_TQ, sp), q_base=0)

    oa_s = _band_attention_sample(qkv_s3, QA, KA, VA, cache_a_k[l].reshape(bs, wa, da),
                                  cache_a_v[l].reshape(bs, wa, da), bias)
    ob_s = _sb_attention(qkv_s3, QB, qkv_s3, KB, qkv_s3, VB,
                         cache_b_k[l].reshape(bs, past, db), 0,
                         cache_b_v[l].reshape(bs, past, db), 0, slabs, tq=ts, q_base=past)

    w_out_bf = w_out[l].astype(BF16)
    merge = functools.partial(_merge, ga=g_out_a[l], gb=g_out_b[l], wo_bf=w_out_bf, gf=norm_ffn[l],
                              w_router=w_router[l], b_router=b_router[l])
    x1_p, xn_p, idx_p, wts_p, rank_p, cnt_p = merge(
        xp2, oa_p.reshape(n_p, da), ob_p.reshape(n_p, db), seen0=jnp.zeros((1, ne), I32))
    x1_s, xn_s, idx_s, wts_s, rank_s, cnt = merge(
        xs2, oa_s.reshape(n_s, da), ob_s.reshape(n_s, db), seen0=cnt_p)
    pos, tile_expert, tile_valid, n_slots = _moe_plan(cnt, (idx_p, idx_s), (rank_p, rank_s),
                                                      n_p + n_s)
    xs = _dispatch(pos, xn_p, xn_s, n_slots)
    ys = _expert_ffn(tile_expert, tile_valid, xs, w_gate[l], b_gate[l], w_up[l], b_up[l],
                     w_down[l], b_down[l])
    y_prompt = _combine(pos, 0, ys, wts_p, x1_p, norm_final).reshape(bp, sp, dm)
    y_sample = _combine(pos, n_p, ys, wts_s, x1_s, norm_final).reshape(bs, ts, dm)

    wp = min(BAND_WINDOW, sp)
    heads_a = lambda t, b_, n_: t.reshape(b_, n_, nh_a, hd)
    heads_b = lambda t, b_, n_: t.reshape(b_, n_, nh_b, hd)
    new_a_k_prompt = heads_a(ka_p, bp, sp)[:, sp - wp:][None]
    new_a_v_prompt = heads_a(va_p, bp, sp)[:, sp - wp:][None]
    new_b_k_prompt = heads_b(kb_p, bp, sp)[None]
    new_b_v_prompt = heads_b(vb_p, bp, sp)[None]
    new_a_k_sample = jnp.concatenate([cache_a_k[l], heads_a(ka_s, bs, ts)], axis=1)[:, ts:][None]
    new_a_v_sample = jnp.concatenate([cache_a_v[l], heads_a(va_s, bs, ts)], axis=1)[:, ts:][None]
    new_b_k_sample = heads_b(kb_s, bs, ts)[None]
    new_b_v_sample = heads_b(vb_s, bs, ts)[None]
    return (y_prompt, y_sample, new_a_k_prompt, new_a_v_prompt, new_b_k_prompt, new_b_v_prompt,
            new_a_k_sample, new_a_v_sample, new_b_k_sample, new_b_v_sample)
```

```python
import functools

import jax
import jax.numpy as jnp
from jax import lax
from jax.experimental import pallas as pl
from jax.experimental.pallas import tpu as pltpu

F32 = jnp.float32
BF16 = jnp.bfloat16
I32 = jnp.int32

CHUNK = 64
LEFT_CHUNKS = 8
BAND_WINDOW = LEFT_CHUNKS * CHUNK
BAND = BAND_WINDOW + CHUNK
HEAD_DIM = 64
HEADS_PER_SLAB = 2
LANES = 128
REL_CLIP = 128
TOP_K = 4
SWIGLU_LIMIT = 7.0
SWIGLU_ALPHA = 1.702
EPS = 1e-6
SCALE = HEAD_DIM ** -0.5

ROW_TILE = 512
BAND_TQ = 256
BAND_KEYS = BAND_TQ + BAND_WINDOW
BAND_VARIANTS = BAND_WINDOW // BAND_TQ + 1
SB_TQ = 512
SB_TK = 256
SB_UNROLL = 2
SB_DEAD = -104.0
MOE_TILE = 512
SEG = 8
VMEM_LIMIT = 56 * 1024 * 1024


def _params(*sem):
    return pltpu.CompilerParams(dimension_semantics=sem, vmem_limit_bytes=VMEM_LIMIT)


def _rms(x, g):
    ms = jnp.mean(x * x, axis=-1, keepdims=True)
    return (x * lax.rsqrt(ms + EPS)) * g


def _head_mask(h):
    lane = lax.broadcasted_iota(I32, (1, LANES), 1)
    return (lane >= h * HEAD_DIM) & (lane < (h + 1) * HEAD_DIM)


def _qkv_kernel(x_ref, g_ref, w_ref, qkv_ref, ka_ref, va_ref, kb_ref, vb_ref):
    xb = _rms(x_ref[...], g_ref[...]).astype(BF16)
    d = ka_ref.shape[-1]
    f32_out = {1: ka_ref, 2: va_ref, 4: kb_ref, 5: vb_ref}
    for c in range(6):
        r = jnp.dot(xb, w_ref[:, c * d:(c + 1) * d], preferred_element_type=F32)
        if c in f32_out:
            f32_out[c][...] = r
        else:
            r = r * SCALE
        qkv_ref[:, c * d:(c + 1) * d] = r.astype(BF16)


def _qkv_proj(x2d, g, w_bf):
    n, dm = x2d.shape
    d3 = w_bf.shape[1]
    d = d3 // 6
    tm = min(ROW_TILE, n)
    row = lambda i: (i, 0)
    fixed = lambda i: (0, 0)
    return pl.pallas_call(
        _qkv_kernel,
        grid=(n // tm,),
        in_specs=[pl.BlockSpec((tm, dm), row),
                  pl.BlockSpec((1, dm), fixed),
                  pl.BlockSpec((dm, d3), fixed)],
        out_specs=[pl.BlockSpec((tm, d3), row)] + [pl.BlockSpec((tm, d), row)] * 4,
        out_shape=[jax.ShapeDtypeStruct((n, d3), BF16)]
        + [jax.ShapeDtypeStruct((n, d), F32)] * 4,
        compiler_params=_params("parallel"),
        name="qkv_proj",
    )(x2d, g.reshape(1, dm), w_bf)


def _bias_kernel(tab_ref, o_ref):
    h = pl.program_id(0)
    i = lax.broadcasted_iota(I32, (CHUNK, BAND), 0)
    j = lax.broadcasted_iota(I32, (CHUNK, BAND), 1)
    idx = jnp.clip(BAND_WINDOW + i - j, -REL_CLIP, REL_CLIP) + REL_CLIP
    n_tab = 2 * REL_CLIP + 1

    def body(t, acc):
        return jnp.where(idx == t, tab_ref[h, t], acc)

    init = jnp.full((CHUNK, BAND), tab_ref[h, n_tab - 1], F32)
    o_ref[...] = lax.fori_loop(0, n_tab - 1, body, init)


def _band_bias(table):
    nh = table.shape[0]
    return pl.pallas_call(
        _bias_kernel,
        grid=(nh,),
        in_specs=[pl.BlockSpec(memory_space=pltpu.SMEM)],
        out_specs=pl.BlockSpec((None, CHUNK, BAND), lambda h: (h, 0, 0)),
        out_shape=jax.ShapeDtypeStruct((nh, CHUNK, BAND), F32),
        compiler_params=_params("arbitrary"),
        name="band_bias",
    )(table)


def _tile_bias_kernel(b_ref, o_ref):
    v = pl.program_id(1)
    o_ref[...] = jnp.full(o_ref.shape, -jnp.inf, F32)
    for c in range(BAND_TQ // CHUNK):
        o_ref[c * CHUNK:(c + 1) * CHUNK, c * CHUNK:c * CHUNK + BAND] = b_ref[...]
    col = lax.broadcasted_iota(I32, o_ref.shape, 1)
    hidden = (BAND_VARIANTS - 1 - v) * BAND_TQ
    o_ref[...] = jnp.where(col >= hidden, o_ref[...], -jnp.inf)


def _band_tile_bias(bias):
    nh = bias.shape[0]
    return pl.pallas_call(
        _tile_bias_kernel,
        grid=(nh, BAND_VARIANTS),
        in_specs=[pl.BlockSpec((None, CHUNK, BAND), lambda h, v: (h, 0, 0))],
        out_specs=pl.BlockSpec((None, None, BAND_TQ, BAND_KEYS), lambda h, v: (h, v, 0, 0)),
        out_shape=jax.ShapeDtypeStruct((nh, BAND_VARIANTS, BAND_TQ, BAND_KEYS), F32),
        compiler_params=_params("parallel", "arbitrary"),
        name="band_tile_bias",
    )(bias)


def _band_tile_kernel(q_ref, *refs):
    n_kb = BAND_KEYS // BAND_TQ
    k_refs, v_refs = refs[:n_kb], refs[n_kb:2 * n_kb]
    bias_ref, o_ref = refs[2 * n_kb], refs[2 * n_kb + 1]
    q = q_ref[...]
    kcat = jnp.concatenate([r[...] for r in k_refs], axis=0)
    vcat = jnp.concatenate([r[...] for r in v_refs], axis=0)
    outs = []
    for h in range(HEADS_PER_SLAB):
        qm = jnp.where(_head_mask(h), q, jnp.zeros_like(q))
        s = lax.dot_general(qm, kcat, (((1,), (1,)), ((), ())),
                            preferred_element_type=F32) + bias_ref[h]
        m = jnp.max(s, axis=-1, keepdims=True)
        p = jnp.exp(s - m)
        l = jnp.sum(p, axis=-1, keepdims=True)
        pv = jnp.dot(p.astype(BF16), vcat, preferred_element_type=F32)
        outs.append(pv / l)
    o_ref[...] = jnp.where(_head_mask(0), outs[0], outs[1])


def _band_attention_prompt(qkv3, q_col0, k_col0, v_col0, tile_bias):
    b, t, _ = qkv3.shape
    nslab = tile_bias.shape[0] // HEADS_PER_SLAB
    n_kb = BAND_KEYS // BAND_TQ
    blk = (None, BAND_TQ, LANES)

    def kv_spec(col0, back):
        return pl.BlockSpec(blk, lambda hp, bi, j: (bi, jnp.maximum(j - back, 0), col0 + hp))

    in_specs = [pl.BlockSpec(blk, lambda hp, bi, j: (bi, j, q_col0 + hp))]
    in_specs += [kv_spec(k_col0, n_kb - 1 - i) for i in range(n_kb)]
    in_specs += [kv_spec(v_col0, n_kb - 1 - i) for i in range(n_kb)]
    in_specs += [pl.BlockSpec((HEADS_PER_SLAB, None, BAND_TQ, BAND_KEYS),
                              lambda hp, bi, j: (hp, jnp.minimum(j, BAND_VARIANTS - 1), 0, 0))]
    return pl.pallas_call(
        _band_tile_kernel,
        grid=(nslab, b, t // BAND_TQ),
        in_specs=in_specs,
        out_specs=pl.BlockSpec(blk, lambda hp, bi, j: (bi, j, hp)),
        out_shape=jax.ShapeDtypeStruct((b, t, nslab * LANES), F32),
        compiler_params=_params("parallel", "parallel", "arbitrary"),
        name="band_attention_prompt",
    )(qkv3, *([qkv3] * (2 * n_kb)), tile_bias)


def _band_chunk_kernel(q_ref, kp_ref, kc_ref, vp_ref, vc_ref, bias_ref, o_ref):
    qc = q_ref[...]
    kwin = jnp.concatenate([kp_ref[...].astype(BF16), kc_ref[...]], axis=0)
    vwin = jnp.concatenate([vp_ref[...].astype(BF16), vc_ref[...]], axis=0)
    outs = []
    for h in range(HEADS_PER_SLAB):
        qm = jnp.where(_head_mask(h), qc, jnp.zeros_like(qc))
        s = lax.dot_general(qm, kwin, (((1,), (1,)), ((), ())),
                            preferred_element_type=F32) + bias_ref[h]
        m = jnp.max(s, axis=-1, keepdims=True)
        p = jnp.exp(s - m)
        l = jnp.sum(p, axis=-1, keepdims=True)
        pv = jnp.dot(p.astype(BF16), vwin, preferred_element_type=F32)
        outs.append(pv / l)
    o_ref[...] = jnp.where(_head_mask(0), outs[0], outs[1])


def _band_attention_sample(qkv3, q_col0, k_col0, v_col0, cache_k, cache_v, bias):
    b, t, _ = qkv3.shape
    nslab = bias.shape[0] // HEADS_PER_SLAB
    new = lambda col0: pl.BlockSpec((None, CHUNK, LANES), lambda bi, hp: (bi, 0, col0 + hp))
    old = pl.BlockSpec((None, BAND_WINDOW, LANES), lambda bi, hp: (bi, 0, hp))
    return pl.pallas_call(
        _band_chunk_kernel,
        grid=(b, nslab),
        in_specs=[new(q_col0), old, new(k_col0), old, new(v_col0),
                  pl.BlockSpec((HEADS_PER_SLAB, CHUNK, BAND), lambda bi, hp: (hp, 0, 0))],
        out_specs=pl.BlockSpec((None, CHUNK, LANES), lambda bi, hp: (bi, 0, hp)),
        out_shape=jax.ShapeDtypeStruct((b, t, nslab * LANES), F32),
        compiler_params=_params("parallel", "parallel"),
        name="band_attention_sample",
    )(qkv3, cache_k, qkv3, cache_v, qkv3, bias)


def _sb_kernel(q_ref, kt_ref, vt_ref, kf_ref, vf_ref, o_ref,
               qm_ref, nu_ref, d_ref, tot_ref, acc_ref, run_ref, *, q_base):
    i = pl.program_id(2)
    tq = q_ref.shape[0]
    tile_keys = kt_ref.shape[0]
    tk = SB_TK
    qpos0 = q_base + i * tq
    n_full = qpos0 // tk
    n_masked = -(-tile_keys // tk)

    q = q_ref[...]
    for h in range(HEADS_PER_SLAB):
        qm_ref[h] = jnp.where(_head_mask(h), q, jnp.zeros_like(q))
    jj = lax.broadcasted_iota(I32, (tk, tk), 0)
    ss = lax.broadcasted_iota(I32, (tk, tk), 1)
    nu_ref[...] = jnp.where(jj >= ss, -1.0, 0.0).astype(BF16)
    acc_ref[...] = jnp.zeros_like(acc_ref)
    run_ref[...] = jnp.zeros_like(run_ref)

    def tile_block(ref, m):
        rows = min(tk, tile_keys - m * tk)
        blk = ref[m * tk:m * tk + rows, :].astype(BF16)
        if rows < tk:
            blk = jnp.concatenate([blk, jnp.zeros((tk - rows, LANES), BF16)], axis=0)
        return blk

    def full_block(ref, kb):
        k0 = pl.multiple_of(kb * tk, tk)
        return ref[pl.ds(k0, tk), :].astype(BF16)

    def stage_a(kblk, tile_m):
        if tile_m is not None:
            row = lax.broadcasted_iota(I32, (tq, tk), 0)
            col = lax.broadcasted_iota(I32, (tq, tk), 1)
            vis = tile_m * tk + col < row
        for h in range(HEADS_PER_SLAB):
            z = lax.dot_general(qm_ref[h], kblk, (((1,), (1,)), ((), ())),
                                preferred_element_type=F32)
            sp = jnp.maximum(z, 0.0) + jnp.log(1.0 + jnp.exp(-jnp.abs(z)))
            if tile_m is not None:
                sp = jnp.where(vis, sp, 0.0)
            after = jnp.dot(sp.astype(BF16), nu_ref[...], preferred_element_type=F32)
            d = z + after
            if tile_m is not None:
                d = jnp.where(vis, d, -jnp.inf)
            d_ref[h] = d
            tot_ref[h] = after[:, 0:1]

    def stage_b(vblk):
        for h in range(HEADS_PER_SLAB):
            run = run_ref[h]
            a = jnp.exp(d_ref[h] + run)
            acc_ref[h] += jnp.dot(a.astype(BF16), vblk, preferred_element_type=F32)
            run_ref[h] = run + tot_ref[h]

    def step(kb):
        stage_b(full_block(vf_ref, kb + 1))
        stage_a(full_block(kf_ref, kb), None)

    def alive():
        r = jnp.maximum(run_ref[0], run_ref[1])
        return (jnp.max(r) >= SB_DEAD).astype(I32)

    stage_a(tile_block(kt_ref, n_masked - 1), n_masked - 1)
    for m in range(n_masked - 2, -1, -1):
        stage_b(tile_block(vt_ref, m + 1))
        stage_a(tile_block(kt_ref, m), m)

    @pl.when(n_full == 0)
    def _():
        stage_b(tile_block(vt_ref, 0))

    @pl.when(n_full > 0)
    def _():
        stage_b(tile_block(vt_ref, 0))
        stage_a(full_block(kf_ref, n_full - 1), None)

        def cond(c):
            kb, live = c
            return (kb >= SB_UNROLL - 1) & (live > 0)

        def body(c):
            kb, _ = c
            for u in range(SB_UNROLL):
                step(kb - u)
            return kb - SB_UNROLL, alive()

        kb, live = lax.while_loop(cond, body, (n_full - 2, alive()))
        for _ in range(SB_UNROLL - 1):
            go = (kb >= 0) & (live > 0)

            @pl.when(go)
            def _():
                step(kb)

            kb = jnp.where(go, kb - 1, kb)
            live = alive()

        @pl.when(live > 0)
        def _():
            stage_b(full_block(vf_ref, kb + 1))

    o_ref[...] = jnp.where(_head_mask(0), acc_ref[0], acc_ref[1])


def _sb_attention(q_arr, q_col0, kt_arr, kt_col0, vt_arr, vt_col0,
                  kf_arr, kf_col0, vf_arr, vf_col0, nslab, tq, q_base):
    b, t, _ = q_arr.shape
    tf = kf_arr.shape[1]
    assert q_base % SB_TK == 0 and t % tq == 0 and tf % SB_TK == 0
    assert tq % SB_TK == 0 or t == tq
    assert q_base + t - tq <= tf
    tile = lambda col0: pl.BlockSpec((None, tq, LANES), lambda bi, hp, i: (bi, i, col0 + hp))
    full = lambda col0: pl.BlockSpec((None, tf, LANES), lambda bi, hp, i: (bi, 0, col0 + hp))
    return pl.pallas_call(
        functools.partial(_sb_kernel, q_base=q_base),
        grid=(b, nslab, t // tq),
        in_specs=[tile(q_col0), tile(kt_col0), tile(vt_col0), full(kf_col0), full(vf_col0)],
        out_specs=pl.BlockSpec((None, tq, LANES), lambda bi, hp, i: (bi, i, hp)),
        out_shape=jax.ShapeDtypeStruct((b, t, nslab * LANES), F32),
        scratch_shapes=[pltpu.VMEM((HEADS_PER_SLAB, tq, LANES), BF16),
                        pltpu.VMEM((SB_TK, SB_TK), BF16),
                        pltpu.VMEM((HEADS_PER_SLAB, tq, SB_TK), F32),
                        pltpu.VMEM((HEADS_PER_SLAB, tq, 1), F32),
                        pltpu.VMEM((HEADS_PER_SLAB, tq, LANES), F32),
                        pltpu.VMEM((HEADS_PER_SLAB, tq, 1), F32)],
        compiler_params=_params("parallel", "parallel", "arbitrary"),
        name="stick_breaking",
    )(q_arr, kt_arr, vt_arr, kf_arr, vf_arr)


def _split_bf16(x):
    hi = x.astype(BF16)
    return hi, (x - hi.astype(F32)).astype(BF16)


def _merge_kernel(x_ref, oa_ref, ob_ref, ga_ref, gb_ref, wo_ref, gf_ref, wr_ref, br_ref,
                  x1_ref, xn_ref, idx_ref, w_ref, rank_ref, cnt_ref):
    o = jnp.concatenate([_rms(oa_ref[...], ga_ref[...]), _rms(ob_ref[...], gb_ref[...])],
                        axis=-1).astype(BF16)
    x1 = x_ref[...] + jnp.dot(o, wo_ref[...], preferred_element_type=F32)
    x1_ref[...] = x1
    xn = _rms(x1, gf_ref[...])
    xn_ref[...] = xn.astype(BF16)
    xh, xl = _split_bf16(xn)
    wh, wl = _split_bf16(wr_ref[...])
    logits = (jnp.dot(xh, wh, preferred_element_type=F32)
              + jnp.dot(xl, wh, preferred_element_type=F32)
              + jnp.dot(xh, wl, preferred_element_type=F32)) + br_ref[...]
    tm, ne = logits.shape
    lane = lax.broadcasted_iota(I32, (tm, ne), 1)
    slot = lax.broadcasted_iota(I32, (tm, TOP_K), 1)
    work = logits
    top = None
    denom = None
    sels = []
    idx = jnp.zeros((tm, TOP_K), I32)
    wts = jnp.zeros((tm, TOP_K), F32)
    for r in range(TOP_K):
        m = jnp.max(work, axis=-1, keepdims=True)
        first = jnp.min(jnp.where(work == m, lane, ne), axis=-1, keepdims=True)
        sel = lane == first
        if top is None:
            top = m
        e = jnp.exp(m - top)
        denom = e if denom is None else denom + e
        idx = jnp.where(slot == r, first, idx)
        wts = jnp.where(slot == r, e, wts)
        sels.append(sel)
        work = jnp.where(sel, -jnp.inf, work)
    idx_ref[...] = idx
    w_ref[...] = wts / denom
    chosen = sels[0] | sels[1] | sels[2] | sels[3]
    onehot = jnp.where(chosen, 1.0, 0.0)
    rr = lax.broadcasted_iota(I32, (tm, tm), 0)
    cc = lax.broadcasted_iota(I32, (tm, tm), 1)
    lower = jnp.where(cc < rr, 1.0, 0.0).astype(BF16)
    before = jnp.dot(lower, onehot.astype(BF16), preferred_element_type=F32)
    rank = jnp.zeros((tm, TOP_K), I32)
    for r in range(TOP_K):
        rk = jnp.sum(jnp.where(sels[r], before, 0.0), axis=-1, keepdims=True)
        rank = jnp.where(slot == r, rk.astype(I32), rank)
    rank_ref[...] = rank
    cnt_ref[...] = jnp.sum(onehot, axis=0, keepdims=True).astype(I32)


def _merge(x2d, oa, ob, ga, gb, wo_bf, gf, w_router, b_router):
    n, dm = x2d.shape
    da = oa.shape[1]
    ne = w_router.shape[1]
    tm = ROW_TILE
    assert n % tm == 0
    row = lambda i: (i, 0)
    fixed = lambda i: (0, 0)
    return pl.pallas_call(
        _merge_kernel,
        grid=(n // tm,),
        in_specs=[pl.BlockSpec((tm, dm), row),
                  pl.BlockSpec((tm, da), row),
                  pl.BlockSpec((tm, da), row),
                  pl.BlockSpec((1, da), fixed),
                  pl.BlockSpec((1, da), fixed),
                  pl.BlockSpec((2 * da, dm), fixed),
                  pl.BlockSpec((1, dm), fixed),
                  pl.BlockSpec((dm, ne), fixed),
                  pl.BlockSpec((1, ne), fixed)],
        out_specs=[pl.BlockSpec((tm, dm), row),
                   pl.BlockSpec((tm, dm), row),
                   pl.BlockSpec((tm, TOP_K), row),
                   pl.BlockSpec((tm, TOP_K), row),
                   pl.BlockSpec((tm, TOP_K), row),
                   pl.BlockSpec((None, 1, ne), lambda i: (i, 0, 0))],
        out_shape=[jax.ShapeDtypeStruct((n, dm), F32),
                   jax.ShapeDtypeStruct((n, dm), BF16),
                   jax.ShapeDtypeStruct((n, TOP_K), I32),
                   jax.ShapeDtypeStruct((n, TOP_K), F32),
                   jax.ShapeDtypeStruct((n, TOP_K), I32),
                   jax.ShapeDtypeStruct((n // tm, 1, ne), I32)],
        compiler_params=_params("parallel"),
        name="merge_router",
    )(x2d, oa, ob, ga.reshape(1, da), gb.reshape(1, da), wo_bf, gf.reshape(1, dm),
      w_router, b_router.reshape(1, ne))


def _tile_slots(ne):
    return ROW_TILE * TOP_K + ne * SEG


def _slots_of(idx, rank, seg_row):
    tm, ne = idx.shape[0], seg_row.shape[-1]
    lane = lax.broadcasted_iota(I32, (tm, ne), 1)
    return [jnp.sum(jnp.where(lane == idx[:, r:r + 1], seg_row, 0), axis=-1, keepdims=True)
            + rank[:, r:r + 1] for r in range(TOP_K)]


def _chunk_copy(src_ref, src_row, dst_ref, dst_row, sem):
    aligned = lambda row: row if isinstance(row, int) else pl.multiple_of(row, SEG)
    return pltpu.make_async_copy(src_ref.at[pl.ds(aligned(src_row), SEG), :],
                                 dst_ref.at[pl.ds(aligned(dst_row), SEG), :], sem)


def _moe_plan(cnt, n_rows):
    nt, ne = cnt.shape
    tm = MOE_TILE
    seg_len = (cnt + SEG - 1) // SEG * SEG
    seg_row = jnp.cumsum(seg_len, axis=1) - seg_len
    rows = jnp.sum(seg_len, axis=0)
    region = (rows + tm - 1) // tm * tm
    base = jnp.cumsum(region) - region
    dst_row = base[None, :] + jnp.cumsum(seg_len, axis=0) - seg_len
    n_tiles = (n_rows * TOP_K + ne * ((SEG - 1) * nt + tm - 1)) // tm
    tail = jnp.concatenate([base[1:], jnp.full((1,), n_tiles * tm, base.dtype)]) - (base + rows)
    tile_end = jnp.cumsum(region // tm)
    tiles = jnp.arange(n_tiles, dtype=I32)
    tile_expert = jnp.minimum(jnp.sum((tiles[:, None] >= tile_end[None, :]).astype(I32), axis=1),
                              ne - 1)
    flat = lambda a: a.reshape(-1).astype(I32)
    return dict(seg_row=seg_row.astype(I32).reshape(nt, 1, ne), seg_row_s=flat(seg_row),
                chunks_s=flat(seg_len // SEG), dst_row_s=flat(dst_row),
                tail_row_s=flat(base + rows), tail_chunks_s=flat(tail // SEG),
                tile_expert=tile_expert.astype(I32),
                tile_valid=(tiles < tile_end[ne - 1]).astype(I32), n_slots=n_tiles * tm)


def _dispatch_kernel(seg_s, chunks_s, dst_s, tail_row_s, tail_chunks_s,
                     xa_ref, xb_ref, idx_ref, rank_ref, seg_ref, xs_ref,
                     buf_ref, zero_ref, sem, *, tiles_a, ne):
    t = pl.program_id(0)
    tm, ts = xa_ref.shape[0], buf_ref.shape[0]
    slots = _slots_of(idx_ref[...], rank_ref[...], seg_ref[...])
    lane = lax.broadcasted_iota(I32, (tm, ts), 1)
    hit = (lane == slots[0]) | (lane == slots[1]) | (lane == slots[2]) | (lane == slots[3])
    place = jnp.where(hit, 1.0, 0.0).astype(BF16)

    def order_rows(x_ref):
        buf_ref[...] = lax.dot_general(place, x_ref[...], (((0,), (0,)), ((), ())),
                                       preferred_element_type=F32)

    @pl.when(t < tiles_a)
    def _():
        order_rows(xa_ref)

    @pl.when(t >= tiles_a)
    def _():
        order_rows(xb_ref)

    def send(e, started):
        k = t * ne + e
        n_chunks = chunks_s[k]

        def start(c, carry):
            _chunk_copy(buf_ref, seg_s[k] + c * SEG, xs_ref, dst_s[k] + c * SEG, sem).start()
            return carry

        lax.fori_loop(0, n_chunks, start, 0)
        return started + n_chunks

    def wait(c, carry):
        _chunk_copy(buf_ref, 0, xs_ref, 0, sem).wait()
        return carry

    lax.fori_loop(0, lax.fori_loop(0, ne, send, 0), wait, 0)

    @pl.when(t == pl.num_programs(0) - 1)
    def _():
        zero_ref[...] = jnp.zeros_like(zero_ref)

        def fill(e, started):
            def start(c, carry):
                _chunk_copy(zero_ref, 0, xs_ref, tail_row_s[e] + c * SEG, sem).start()
                return carry

            lax.fori_loop(0, tail_chunks_s[e], start, 0)
            return started + tail_chunks_s[e]

        def wait_fill(c, carry):
            _chunk_copy(zero_ref, 0, xs_ref, 0, sem).wait()
            return carry

        lax.fori_loop(0, lax.fori_loop(0, ne, fill, 0), wait_fill, 0)


def _dispatch(plan, xa, xb, idx, rank):
    (na, dm), nb = xa.shape, xb.shape[0]
    tm = ROW_TILE
    tiles_a, tiles_b = na // tm, nb // tm
    ne = plan["seg_row"].shape[-1]
    tile = lambda t, *_: (t, 0)
    return pl.pallas_call(
        functools.partial(_dispatch_kernel, tiles_a=tiles_a, ne=ne),
        grid_spec=pltpu.PrefetchScalarGridSpec(
            num_scalar_prefetch=5,
            grid=(tiles_a + tiles_b,),
            in_specs=[pl.BlockSpec((tm, dm), lambda t, *_: (jnp.minimum(t, tiles_a - 1), 0)),
                      pl.BlockSpec((tm, dm), lambda t, *_: (jnp.maximum(t - tiles_a, 0), 0)),
                      pl.BlockSpec((tm, TOP_K), tile),
                      pl.BlockSpec((tm, TOP_K), tile),
                      pl.BlockSpec((None, 1, ne), lambda t, *_: (t, 0, 0))],
            out_specs=pl.BlockSpec(memory_space=pl.ANY),
            scratch_shapes=[pltpu.VMEM((_tile_slots(ne), dm), F32),
                            pltpu.VMEM((SEG, dm), F32),
                            pltpu.SemaphoreType.DMA]),
        out_shape=jax.ShapeDtypeStruct((plan["n_slots"], dm), F32),
        compiler_params=_params("arbitrary"),
        name="moe_dispatch",
    )(plan["seg_row_s"], plan["chunks_s"], plan["dst_row_s"], plan["tail_row_s"],
      plan["tail_chunks_s"], xa, xb, idx, rank, plan["seg_row"])


def _ffn_kernel(te_ref, tv_ref, xs_ref, wg_ref, bg_ref, wu_ref, bu_ref, wd_ref, bd_ref,
                y_ref, wgb_ref, wub_ref, wdb_ref):
    t = pl.program_id(0)
    new_expert = (t == 0) | (te_ref[t] != te_ref[jnp.maximum(t - 1, 0)])

    @pl.when(new_expert)
    def _():
        wgb_ref[...] = wg_ref[...].astype(BF16)
        wub_ref[...] = wu_ref[...].astype(BF16)
        wdb_ref[...] = wd_ref[...].astype(BF16)

    @pl.when(tv_ref[t] != 0)
    def _():
        x = xs_ref[...].astype(BF16)
        g = jnp.minimum(jnp.dot(x, wgb_ref[...], preferred_element_type=F32) + bg_ref[...],
                        SWIGLU_LIMIT)
        u = jnp.clip(jnp.dot(x, wub_ref[...], preferred_element_type=F32) + bu_ref[...],
                     -SWIGLU_LIMIT, SWIGLU_LIMIT)
        h = g * jax.nn.sigmoid(SWIGLU_ALPHA * g) * (u + 1.0)
        y_ref[...] = jnp.dot(h.astype(BF16), wdb_ref[...], preferred_element_type=F32) + bd_ref[...]

    @pl.when(tv_ref[t] == 0)
    def _():
        y_ref[...] = jnp.zeros_like(y_ref)


def _expert_ffn(tile_expert, tile_valid, xs, wg, bg, wu, bu, wd, bd):
    n_slots, dm = xs.shape
    ne, _, dff = wg.shape
    tm = MOE_TILE
    row = lambda t, te, tv: (t, 0)
    exp3 = lambda t, te, tv: (te[t], 0, 0)
    return pl.pallas_call(
        _ffn_kernel,
        grid_spec=pltpu.PrefetchScalarGridSpec(
            num_scalar_prefetch=2,
            grid=(n_slots // tm,),
            in_specs=[pl.BlockSpec((tm, dm), row),
                      pl.BlockSpec((None, dm, dff), exp3),
                      pl.BlockSpec((None, 1, dff), exp3),
                      pl.BlockSpec((None, dm, dff), exp3),
                      pl.BlockSpec((None, 1, dff), exp3),
                      pl.BlockSpec((None, dff, dm), exp3),
                      pl.BlockSpec((None, 1, dm), exp3)],
            out_specs=pl.BlockSpec((tm, dm), row),
            scratch_shapes=[pltpu.VMEM((dm, dff), BF16),
                            pltpu.VMEM((dm, dff), BF16),
                            pltpu.VMEM((dff, dm), BF16)]),
        out_shape=jax.ShapeDtypeStruct((n_slots, dm), F32),
        compiler_params=_params("arbitrary"),
        name="moe_ffn",
    )(tile_expert, tile_valid, xs, wg, bg.reshape(ne, 1, dff), wu, bu.reshape(ne, 1, dff),
      wd, bd.reshape(ne, 1, dm))


def _combine_kernel(seg_s, chunks_s, dst_s, ys_ref, idx_ref, rank_ref, w_ref, seg_ref, x1_ref,
                    g_ref, o_ref, buf_ref, sem, *, tile0, ne):
    i = pl.program_id(0)
    t = tile0 + i
    tm, ts = x1_ref.shape[0], buf_ref.shape[0]

    @pl.when(i == 0)
    def _():
        buf_ref[...] = jnp.zeros_like(buf_ref)

    def fetch(e, started):
        k = t * ne + e
        n_chunks = chunks_s[k]

        def start(c, carry):
            _chunk_copy(ys_ref, dst_s[k] + c * SEG, buf_ref, seg_s[k] + c * SEG, sem).start()
            return carry

        lax.fori_loop(0, n_chunks, start, 0)
        return started + n_chunks

    def wait(c, carry):
        _chunk_copy(ys_ref, 0, buf_ref, 0, sem).wait()
        return carry

    lax.fori_loop(0, lax.fori_loop(0, ne, fetch, 0), wait, 0)

    slots = _slots_of(idx_ref[...], rank_ref[...], seg_ref[...])
    lane = lax.broadcasted_iota(I32, (tm, ts), 1)
    w = w_ref[...]
    gate = jnp.zeros((tm, ts), F32)
    for r in range(TOP_K):
        gate = jnp.where(lane == slots[r], w[:, r:r + 1], gate)
    gate_hi, gate_lo = _split_bf16(gate)
    yb = buf_ref[...].astype(BF16)
    y = (jnp.dot(gate_hi, yb, preferred_element_type=F32)
         + jnp.dot(gate_lo, yb, preferred_element_type=F32))
    o_ref[...] = _rms(x1_ref[...] + y, g_ref[...])


def _combine(plan, tile0, ys, idx, rank, wts, x1, gfin):
    n, dm = x1.shape
    tm = ROW_TILE
    ne = plan["seg_row"].shape[-1]
    tile = lambda i, *_: (i, 0)
    return pl.pallas_call(
        functools.partial(_combine_kernel, tile0=tile0, ne=ne),
        grid_spec=pltpu.PrefetchScalarGridSpec(
            num_scalar_prefetch=3,
            grid=(n // tm,),
            in_specs=[pl.BlockSpec(memory_space=pl.ANY),
                      pl.BlockSpec((tm, TOP_K), tile),
                      pl.BlockSpec((tm, TOP_K), tile),
                      pl.BlockSpec((tm, TOP_K), tile),
                      pl.BlockSpec((None, 1, ne), lambda i, *_: (tile0 + i, 0, 0)),
                      pl.BlockSpec((tm, dm), tile),
                      pl.BlockSpec((1, dm), lambda i, *_: (0, 0))],
            out_specs=pl.BlockSpec((tm, dm), tile),
            scratch_shapes=[pltpu.VMEM((_tile_slots(ne), dm), F32),
                            pltpu.SemaphoreType.DMA]),
        out_shape=jax.ShapeDtypeStruct((n, dm), F32),
        compiler_params=_params("arbitrary"),
        name="moe_combine",
    )(plan["seg_row_s"], plan["chunks_s"], plan["dst_row_s"], ys, idx, rank, wts,
      plan["seg_row"], x1, gfin.reshape(1, dm))


def kernel(x_prompt, x_sample, cache_a_k, cache_a_v, cache_b_k, cache_b_v, norm_attn, w_in,
           rel_bias, g_out_a, g_out_b, w_out, norm_ffn, w_router, b_router, w_gate, b_gate,
           w_up, b_up, w_down, b_down, norm_final):
    depth = w_in.shape[0]
    assert depth == 1, "the MoE combine applies the final norm: single layer only"
    bp, sp, dm = x_prompt.shape
    bs, ts, _ = x_sample.shape
    past = cache_b_k.shape[2]
    wa = cache_a_k.shape[2]
    nh_a, hd = cache_a_k.shape[3], cache_a_k.shape[4]
    nh_b = cache_b_k.shape[3]
    da, db = nh_a * hd, nh_b * hd
    ne = w_router.shape[-1]
    assert da == db and hd == HEAD_DIM and wa == BAND_WINDOW and ts == CHUNK
    slabs = da // LANES
    QA, KA, VA, QB, KB, VB = (c * slabs for c in range(6))
    l = 0
    n_p, n_s = bp * sp, bs * ts

    xp2 = x_prompt.reshape(n_p, dm)
    xs2 = x_sample.reshape(n_s, dm)
    w_in_bf = w_in[l].astype(BF16)
    qkv_p, ka_p, va_p, kb_p, vb_p = _qkv_proj(xp2, norm_attn[l], w_in_bf)
    qkv_s, ka_s, va_s, kb_s, vb_s = _qkv_proj(xs2, norm_attn[l], w_in_bf)
    qkv_p3 = qkv_p.reshape(bp, sp, 6 * da)
    qkv_s3 = qkv_s.reshape(bs, ts, 6 * da)

    bias = _band_bias(rel_bias[l])
    tile_bias = _band_tile_bias(bias)

    oa_p = _band_attention_prompt(qkv_p3, QA, KA, VA, tile_bias)
    ob_p = _sb_attention(qkv_p3, QB, qkv_p3, KB, qkv_p3, VB, qkv_p3, KB, qkv_p3, VB,
                         slabs, tq=min(SB_TQ, sp), q_base=0)

    oa_s = _band_attention_sample(qkv_s3, QA, KA, VA, cache_a_k[l].reshape(bs, wa, da),
                                  cache_a_v[l].reshape(bs, wa, da), bias)
    ob_s = _sb_attention(qkv_s3, QB, qkv_s3, KB, qkv_s3, VB,
                         cache_b_k[l].reshape(bs, past, db), 0,
                         cache_b_v[l].reshape(bs, past, db), 0, slabs, tq=ts, q_base=past)

    w_out_bf = w_out[l].astype(BF16)
    merge = functools.partial(_merge, ga=g_out_a[l], gb=g_out_b[l], wo_bf=w_out_bf, gf=norm_ffn[l],
                              w_router=w_router[l], b_router=b_router[l])
    x1_p, xn_p, idx_p, wts_p, rank_p, cnt_p = merge(
        xp2, oa_p.reshape(n_p, da), ob_p.reshape(n_p, db))
    x1_s, xn_s, idx_s, wts_s, rank_s, cnt_s = merge(
        xs2, oa_s.reshape(n_s, da), ob_s.reshape(n_s, db))
    plan = _moe_plan(jnp.concatenate([cnt_p, cnt_s]).reshape(-1, ne), n_p + n_s)
    xs = _dispatch(plan, xn_p, xn_s, jnp.concatenate([idx_p, idx_s]),
                   jnp.concatenate([rank_p, rank_s]))
    ys = _expert_ffn(plan["tile_expert"], plan["tile_valid"], xs, w_gate[l], b_gate[l],
                     w_up[l], b_up[l], w_down[l], b_down[l])
    y_prompt = _combine(plan, 0, ys, idx_p, rank_p, wts_p, x1_p, norm_final).reshape(bp, sp, dm)
    y_sample = _combine(plan, n_p // ROW_TILE, ys, idx_s, rank_s, wts_s, x1_s,
                        norm_final).reshape(bs, ts, dm)

    wp = min(BAND_WINDOW, sp)
    heads_a = lambda t, b_, n_: t.reshape(b_, n_, nh_a, hd)
    heads_b = lambda t, b_, n_: t.reshape(b_, n_, nh_b, hd)
    new_a_k_prompt = heads_a(ka_p, bp, sp)[:, sp - wp:][None]
    new_a_v_prompt = heads_a(va_p, bp, sp)[:, sp - wp:][None]
    new_b_k_prompt = heads_b(kb_p, bp, sp)[None]
    new_b_v_prompt = heads_b(vb_p, bp, sp)[None]
    new_a_k_sample = jnp.concatenate([cache_a_k[l], heads_a(ka_s, bs, ts)], axis=1)[:, ts:][None]
    new_a_v_sample = jnp.concatenate([cache_a_v[l], heads_a(va_s, bs, ts)], axis=1)[:, ts:][None]
    new_b_k_sample = heads_b(kb_s, bs, ts)[None]
    new_b_v_sample = heads_b(vb_s, bs, ts)[None]
    return (y_prompt, y_sample, new_a_k_prompt, new_a_v_prompt, new_b_k_prompt, new_b_v_prompt,
            new_a_k_sample, new_a_v_sample, new_b_k_sample, new_b_v_sample)
```

```python
import functools

import jax
import jax.numpy as jnp
from jax import lax
from jax.experimental import pallas as pl
from jax.experimental.pallas import tpu as pltpu

F32 = jnp.float32
BF16 = jnp.bfloat16
I32 = jnp.int32

CHUNK = 64
LEFT_CHUNKS = 8
BAND_WINDOW = LEFT_CHUNKS * CHUNK
BAND = BAND_WINDOW + CHUNK
HEAD_DIM = 64
HEADS_PER_SLAB = 2
LANES = 128
REL_CLIP = 128
TOP_K = 4
SWIGLU_LIMIT = 7.0
SWIGLU_ALPHA = 1.702
EPS = 1e-6
SCALE = HEAD_DIM ** -0.5

ROW_TILE = 512
BAND_TQ = 256
BAND_KEYS = BAND_TQ + BAND_WINDOW
BAND_VARIANTS = BAND_WINDOW // BAND_TQ + 1
BAND_SLABS = 2
SB_TQ = 512
SB_TK = 256
SB_UNROLL = 1
SB_DEAD = -104.0
MOE_TILE = 512
SEG = 8
VMEM_LIMIT = 56 * 1024 * 1024


def _params(*sem):
    return pltpu.CompilerParams(dimension_semantics=sem, vmem_limit_bytes=VMEM_LIMIT)


def _rms(x, g):
    ms = jnp.mean(x * x, axis=-1, keepdims=True)
    return (x * lax.rsqrt(ms + EPS)) * g


def _head_mask(h):
    lane = lax.broadcasted_iota(I32, (1, LANES), 1)
    return (lane >= h * HEAD_DIM) & (lane < (h + 1) * HEAD_DIM)


def _qkv_kernel(x_ref, g_ref, w_ref, qkv_ref, ka_ref, va_ref, kb_ref, vb_ref):
    xb = _rms(x_ref[...], g_ref[...]).astype(BF16)
    d = ka_ref.shape[-1]
    f32_out = {1: ka_ref, 2: va_ref, 4: kb_ref, 5: vb_ref}
    for c in range(6):
        r = jnp.dot(xb, w_ref[:, c * d:(c + 1) * d], preferred_element_type=F32)
        if c in f32_out:
            f32_out[c][...] = r
        else:
            r = r * SCALE
        qkv_ref[:, c * d:(c + 1) * d] = r.astype(BF16)


def _qkv_proj(x2d, g, w_bf):
    n, dm = x2d.shape
    d3 = w_bf.shape[1]
    d = d3 // 6
    tm = min(ROW_TILE, n)
    row = lambda i: (i, 0)
    fixed = lambda i: (0, 0)
    return pl.pallas_call(
        _qkv_kernel,
        grid=(n // tm,),
        in_specs=[pl.BlockSpec((tm, dm), row),
                  pl.BlockSpec((1, dm), fixed),
                  pl.BlockSpec((dm, d3), fixed)],
        out_specs=[pl.BlockSpec((tm, d3), row)] + [pl.BlockSpec((tm, d), row)] * 4,
        out_shape=[jax.ShapeDtypeStruct((n, d3), BF16)]
        + [jax.ShapeDtypeStruct((n, d), F32)] * 4,
        compiler_params=_params("parallel"),
        name="qkv_proj",
    )(x2d, g.reshape(1, dm), w_bf)


def _bias_kernel(tab_ref, o_ref):
    h = pl.program_id(0)
    i = lax.broadcasted_iota(I32, (CHUNK, BAND), 0)
    j = lax.broadcasted_iota(I32, (CHUNK, BAND), 1)
    idx = jnp.clip(BAND_WINDOW + i - j, -REL_CLIP, REL_CLIP) + REL_CLIP
    n_tab = 2 * REL_CLIP + 1

    def body(t, acc):
        return jnp.where(idx == t, tab_ref[h, t], acc)

    init = jnp.full((CHUNK, BAND), tab_ref[h, n_tab - 1], F32)
    o_ref[...] = lax.fori_loop(0, n_tab - 1, body, init)


def _band_bias(table):
    nh = table.shape[0]
    return pl.pallas_call(
        _bias_kernel,
        grid=(nh,),
        in_specs=[pl.BlockSpec(memory_space=pltpu.SMEM)],
        out_specs=pl.BlockSpec((None, CHUNK, BAND), lambda h: (h, 0, 0)),
        out_shape=jax.ShapeDtypeStruct((nh, CHUNK, BAND), F32),
        compiler_params=_params("arbitrary"),
        name="band_bias",
    )(table)


def _tile_bias_kernel(b_ref, o_ref):
    v = pl.program_id(1)
    o_ref[...] = jnp.full(o_ref.shape, -jnp.inf, F32)
    for c in range(BAND_TQ // CHUNK):
        o_ref[c * CHUNK:(c + 1) * CHUNK, c * CHUNK:c * CHUNK + BAND] = b_ref[...]
    col = lax.broadcasted_iota(I32, o_ref.shape, 1)
    hidden = (BAND_VARIANTS - 1 - v) * BAND_TQ
    o_ref[...] = jnp.where(col >= hidden, o_ref[...], -jnp.inf)


def _band_tile_bias(bias):
    nh = bias.shape[0]
    return pl.pallas_call(
        _tile_bias_kernel,
        grid=(nh, BAND_VARIANTS),
        in_specs=[pl.BlockSpec((None, CHUNK, BAND), lambda h, v: (h, 0, 0))],
        out_specs=pl.BlockSpec((None, None, BAND_TQ, BAND_KEYS), lambda h, v: (h, v, 0, 0)),
        out_shape=jax.ShapeDtypeStruct((nh, BAND_VARIANTS, BAND_TQ, BAND_KEYS), F32),
        compiler_params=_params("parallel", "arbitrary"),
        name="band_tile_bias",
    )(bias)


def _band_tile_kernel(q_ref, *refs):
    n_kb = BAND_KEYS // BAND_TQ
    k_refs, v_refs = refs[:n_kb], refs[n_kb:2 * n_kb]
    bias_ref, o_ref = refs[2 * n_kb], refs[2 * n_kb + 1]
    for sl in range(BAND_SLABS):
        cols = slice(sl * LANES, (sl + 1) * LANES)
        q = q_ref[:, cols]
        kcat = jnp.concatenate([r[:, cols] for r in k_refs], axis=0)
        vcat = jnp.concatenate([r[:, cols] for r in v_refs], axis=0)
        outs = []
        for h in range(HEADS_PER_SLAB):
            qm = jnp.where(_head_mask(h), q, jnp.zeros_like(q))
            s = lax.dot_general(qm, kcat, (((1,), (1,)), ((), ())),
                                preferred_element_type=F32) + bias_ref[sl * HEADS_PER_SLAB + h]
            m = jnp.max(s, axis=-1, keepdims=True)
            p = jnp.exp(s - m)
            l = jnp.sum(p, axis=-1, keepdims=True)
            pv = jnp.dot(p.astype(BF16), vcat, preferred_element_type=F32)
            outs.append(pv / l)
        o_ref[:, cols] = jnp.where(_head_mask(0), outs[0], outs[1])


def _band_attention_prompt(qkv3, q_col0, k_col0, v_col0, tile_bias):
    b, t, _ = qkv3.shape
    nslab = tile_bias.shape[0] // HEADS_PER_SLAB
    n_kb = BAND_KEYS // BAND_TQ
    assert nslab % BAND_SLABS == 0 and all(c % BAND_SLABS == 0 for c in (q_col0, k_col0, v_col0))
    blk = (None, BAND_TQ, BAND_SLABS * LANES)
    group = lambda col0, g: col0 // BAND_SLABS + g

    def kv_spec(col0, back):
        return pl.BlockSpec(blk, lambda g, bi, j: (bi, jnp.maximum(j - back, 0), group(col0, g)))

    in_specs = [pl.BlockSpec(blk, lambda g, bi, j: (bi, j, group(q_col0, g)))]
    in_specs += [kv_spec(k_col0, n_kb - 1 - i) for i in range(n_kb)]
    in_specs += [kv_spec(v_col0, n_kb - 1 - i) for i in range(n_kb)]
    in_specs += [pl.BlockSpec((BAND_SLABS * HEADS_PER_SLAB, None, BAND_TQ, BAND_KEYS),
                              lambda g, bi, j: (g, jnp.minimum(j, BAND_VARIANTS - 1), 0, 0))]
    return pl.pallas_call(
        _band_tile_kernel,
        grid=(nslab // BAND_SLABS, b, t // BAND_TQ),
        in_specs=in_specs,
        out_specs=pl.BlockSpec(blk, lambda g, bi, j: (bi, j, g)),
        out_shape=jax.ShapeDtypeStruct((b, t, nslab * LANES), F32),
        compiler_params=_params("parallel", "parallel", "arbitrary"),
        name="band_attention_prompt",
    )(qkv3, *([qkv3] * (2 * n_kb)), tile_bias)


def _band_chunk_kernel(q_ref, kp_ref, kc_ref, vp_ref, vc_ref, bias_ref, o_ref):
    for sl in range(o_ref.shape[-1] // LANES):
        cols = slice(sl * LANES, (sl + 1) * LANES)
        qc = q_ref[:, cols]
        kwin = jnp.concatenate([kp_ref[:, cols].astype(BF16), kc_ref[:, cols]], axis=0)
        vwin = jnp.concatenate([vp_ref[:, cols].astype(BF16), vc_ref[:, cols]], axis=0)
        outs = []
        for h in range(HEADS_PER_SLAB):
            qm = jnp.where(_head_mask(h), qc, jnp.zeros_like(qc))
            s = lax.dot_general(qm, kwin, (((1,), (1,)), ((), ())),
                                preferred_element_type=F32) + bias_ref[sl * HEADS_PER_SLAB + h]
            m = jnp.max(s, axis=-1, keepdims=True)
            p = jnp.exp(s - m)
            l = jnp.sum(p, axis=-1, keepdims=True)
            pv = jnp.dot(p.astype(BF16), vwin, preferred_element_type=F32)
            outs.append(pv / l)
        o_ref[:, cols] = jnp.where(_head_mask(0), outs[0], outs[1])


def _band_attention_sample(qkv3, q_col0, k_col0, v_col0, cache_k, cache_v, bias):
    b, t, _ = qkv3.shape
    nh = bias.shape[0]
    d = nh // HEADS_PER_SLAB * LANES
    assert all(c * LANES % d == 0 for c in (q_col0, k_col0, v_col0))
    new = lambda col0: pl.BlockSpec((None, CHUNK, d), lambda bi: (bi, 0, col0 * LANES // d))
    old = pl.BlockSpec((None, BAND_WINDOW, d), lambda bi: (bi, 0, 0))
    return pl.pallas_call(
        _band_chunk_kernel,
        grid=(b,),
        in_specs=[new(q_col0), old, new(k_col0), old, new(v_col0),
                  pl.BlockSpec((nh, CHUNK, BAND), lambda bi: (0, 0, 0))],
        out_specs=pl.BlockSpec((None, CHUNK, d), lambda bi: (bi, 0, 0)),
        out_shape=jax.ShapeDtypeStruct((b, t, d), F32),
        compiler_params=_params("parallel"),
        name="band_attention_sample",
    )(qkv3, cache_k, qkv3, cache_v, qkv3, bias)


def _sb_kernel(q_ref, kt_ref, vt_ref, kf_ref, vf_ref, o_ref,
               qm_ref, nu_ref, d_ref, tot_ref, acc_ref, run_ref, *, q_base):
    i = pl.program_id(2)
    tq = q_ref.shape[0]
    tile_keys = kt_ref.shape[0]
    tk = SB_TK
    qpos0 = q_base + i * tq
    n_full = qpos0 // tk
    n_masked = -(-tile_keys // tk)

    q = q_ref[...]
    for h in range(HEADS_PER_SLAB):
        qm_ref[h] = jnp.where(_head_mask(h), q, jnp.zeros_like(q))
    jj = lax.broadcasted_iota(I32, (tk, tk), 0)
    ss = lax.broadcasted_iota(I32, (tk, tk), 1)
    nu_ref[...] = jnp.where(jj >= ss, -1.0, 0.0).astype(BF16)
    acc_ref[...] = jnp.zeros_like(acc_ref)
    run_ref[...] = jnp.zeros_like(run_ref)

    def tile_block(ref, m):
        rows = min(tk, tile_keys - m * tk)
        blk = ref[m * tk:m * tk + rows, :].astype(BF16)
        if rows < tk:
            blk = jnp.concatenate([blk, jnp.zeros((tk - rows, LANES), BF16)], axis=0)
        return blk

    def full_block(ref, kb):
        k0 = pl.multiple_of(kb * tk, tk)
        return ref[pl.ds(k0, tk), :].astype(BF16)

    def stage_a(kblk, tile_m):
        if tile_m is not None:
            row = lax.broadcasted_iota(I32, (tq, tk), 0)
            col = lax.broadcasted_iota(I32, (tq, tk), 1)
            vis = tile_m * tk + col < row
        for h in range(HEADS_PER_SLAB):
            z = lax.dot_general(qm_ref[h], kblk, (((1,), (1,)), ((), ())),
                                preferred_element_type=F32)
            sp = jnp.maximum(z, 0.0) + jnp.log(1.0 + jnp.exp(-jnp.abs(z)))
            if tile_m is not None:
                sp = jnp.where(vis, sp, 0.0)
            after = jnp.dot(sp.astype(BF16), nu_ref[...], preferred_element_type=F32)
            d = z + after
            if tile_m is not None:
                d = jnp.where(vis, d, -jnp.inf)
            d_ref[h] = d
            tot_ref[h] = after[:, 0:1]

    def stage_b(vblk):
        for h in range(HEADS_PER_SLAB):
            run = run_ref[h]
            a = jnp.exp(d_ref[h] + run)
            acc_ref[h] += jnp.dot(a.astype(BF16), vblk, preferred_element_type=F32)
            run_ref[h] = run + tot_ref[h]

    def step(kb):
        stage_b(full_block(vf_ref, kb + 1))
        stage_a(full_block(kf_ref, kb), None)

    def alive():
        r = jnp.maximum(run_ref[0], run_ref[1])
        return (jnp.max(r) >= SB_DEAD).astype(I32)

    stage_a(tile_block(kt_ref, n_masked - 1), n_masked - 1)
    for m in range(n_masked - 2, -1, -1):
        stage_b(tile_block(vt_ref, m + 1))
        stage_a(tile_block(kt_ref, m), m)

    @pl.when(n_full == 0)
    def _():
        stage_b(tile_block(vt_ref, 0))

    @pl.when(n_full > 0)
    def _():
        stage_b(tile_block(vt_ref, 0))
        stage_a(full_block(kf_ref, n_full - 1), None)

        def cond(c):
            kb, live = c
            return (kb >= SB_UNROLL - 1) & (live > 0)

        def body(c):
            kb, _ = c
            for u in range(SB_UNROLL):
                step(kb - u)
            return kb - SB_UNROLL, alive()

        kb, live = lax.while_loop(cond, body, (n_full - 2, alive()))
        for _ in range(SB_UNROLL - 1):
            go = (kb >= 0) & (live > 0)

            @pl.when(go)
            def _():
                step(kb)

            kb = jnp.where(go, kb - 1, kb)
            live = alive()

        @pl.when(live > 0)
        def _():
            stage_b(full_block(vf_ref, kb + 1))

    o_ref[...] = jnp.where(_head_mask(0), acc_ref[0], acc_ref[1])


def _sb_attention(q_arr, q_col0, kt_arr, kt_col0, vt_arr, vt_col0,
                  kf_arr, kf_col0, vf_arr, vf_col0, nslab, tq, q_base):
    b, t, _ = q_arr.shape
    tf = kf_arr.shape[1]
    assert q_base % SB_TK == 0 and t % tq == 0 and tf % SB_TK == 0
    assert tq % SB_TK == 0 or t == tq
    assert q_base + t - tq <= tf
    tile = lambda col0: pl.BlockSpec((None, tq, LANES), lambda bi, hp, i: (bi, i, col0 + hp))
    full = lambda col0: pl.BlockSpec((None, tf, LANES), lambda bi, hp, i: (bi, 0, col0 + hp))
    return pl.pallas_call(
        functools.partial(_sb_kernel, q_base=q_base),
        grid=(b, nslab, t // tq),
        in_specs=[tile(q_col0), tile(kt_col0), tile(vt_col0), full(kf_col0), full(vf_col0)],
        out_specs=pl.BlockSpec((None, tq, LANES), lambda bi, hp, i: (bi, i, hp)),
        out_shape=jax.ShapeDtypeStruct((b, t, nslab * LANES), F32),
        scratch_shapes=[pltpu.VMEM((HEADS_PER_SLAB, tq, LANES), BF16),
                        pltpu.VMEM((SB_TK, SB_TK), BF16),
                        pltpu.VMEM((HEADS_PER_SLAB, tq, SB_TK), F32),
                        pltpu.VMEM((HEADS_PER_SLAB, tq, 1), F32),
                        pltpu.VMEM((HEADS_PER_SLAB, tq, LANES), F32),
                        pltpu.VMEM((HEADS_PER_SLAB, tq, 1), F32)],
        compiler_params=_params("parallel", "parallel", "arbitrary"),
        name="stick_breaking",
    )(q_arr, kt_arr, vt_arr, kf_arr, vf_arr)


def _split_bf16(x):
    hi = x.astype(BF16)
    return hi, (x - hi.astype(F32)).astype(BF16)


def _merge_kernel(x_ref, oa_ref, ob_ref, ga_ref, gb_ref, wo_ref, gf_ref, wr_ref, br_ref,
                  x1_ref, xn_ref, idx_ref, w_ref, rank_ref, cnt_ref):
    o = jnp.concatenate([_rms(oa_ref[...], ga_ref[...]), _rms(ob_ref[...], gb_ref[...])],
                        axis=-1).astype(BF16)
    x1 = x_ref[...] + jnp.dot(o, wo_ref[...], preferred_element_type=F32)
    x1_ref[...] = x1
    xn = _rms(x1, gf_ref[...])
    xn_ref[...] = xn.astype(BF16)
    xh, xl = _split_bf16(xn)
    wh, wl = _split_bf16(wr_ref[...])
    logits = (jnp.dot(xh, wh, preferred_element_type=F32)
              + jnp.dot(xl, wh, preferred_element_type=F32)
              + jnp.dot(xh, wl, preferred_element_type=F32)) + br_ref[...]
    tm, ne = logits.shape
    lane = lax.broadcasted_iota(I32, (tm, ne), 1)
    slot = lax.broadcasted_iota(I32, (tm, TOP_K), 1)
    work = logits
    top = None
    denom = None
    sels = []
    idx = jnp.zeros((tm, TOP_K), I32)
    wts = jnp.zeros((tm, TOP_K), F32)
    for r in range(TOP_K):
        m = jnp.max(work, axis=-1, keepdims=True)
        first = jnp.min(jnp.where(work == m, lane, ne), axis=-1, keepdims=True)
        sel = lane == first
        if top is None:
            top = m
        e = jnp.exp(m - top)
        denom = e if denom is None else denom + e
        idx = jnp.where(slot == r, first, idx)
        wts = jnp.where(slot == r, e, wts)
        sels.append(sel)
        work = jnp.where(sel, -jnp.inf, work)
    idx_ref[...] = idx
    w_ref[...] = wts / denom
    chosen = sels[0] | sels[1] | sels[2] | sels[3]
    onehot = jnp.where(chosen, 1.0, 0.0)
    rr = lax.broadcasted_iota(I32, (tm, tm), 0)
    cc = lax.broadcasted_iota(I32, (tm, tm), 1)
    lower = jnp.where(cc < rr, 1.0, 0.0).astype(BF16)
    before = jnp.dot(lower, onehot.astype(BF16), preferred_element_type=F32)
    rank = jnp.zeros((tm, TOP_K), I32)
    for r in range(TOP_K):
        rk = jnp.sum(jnp.where(sels[r], before, 0.0), axis=-1, keepdims=True)
        rank = jnp.where(slot == r, rk.astype(I32), rank)
    rank_ref[...] = rank
    cnt_ref[...] = jnp.sum(onehot, axis=0, keepdims=True).astype(I32)


def _merge(x2d, oa, ob, ga, gb, wo_bf, gf, w_router, b_router):
    n, dm = x2d.shape
    da = oa.shape[1]
    ne = w_router.shape[1]
    tm = ROW_TILE
    assert n % tm == 0
    row = lambda i: (i, 0)
    fixed = lambda i: (0, 0)
    return pl.pallas_call(
        _merge_kernel,
        grid=(n // tm,),
        in_specs=[pl.BlockSpec((tm, dm), row),
                  pl.BlockSpec((tm, da), row),
                  pl.BlockSpec((tm, da), row),
                  pl.BlockSpec((1, da), fixed),
                  pl.BlockSpec((1, da), fixed),
                  pl.BlockSpec((2 * da, dm), fixed),
                  pl.BlockSpec((1, dm), fixed),
                  pl.BlockSpec((dm, ne), fixed),
                  pl.BlockSpec((1, ne), fixed)],
        out_specs=[pl.BlockSpec((tm, dm), row),
                   pl.BlockSpec((tm, dm), row),
                   pl.BlockSpec((tm, TOP_K), row),
                   pl.BlockSpec((tm, TOP_K), row),
                   pl.BlockSpec((tm, TOP_K), row),
                   pl.BlockSpec((None, 1, ne), lambda i: (i, 0, 0))],
        out_shape=[jax.ShapeDtypeStruct((n, dm), F32),
                   jax.ShapeDtypeStruct((n, dm), BF16),
                   jax.ShapeDtypeStruct((n, TOP_K), I32),
                   jax.ShapeDtypeStruct((n, TOP_K), F32),
                   jax.ShapeDtypeStruct((n, TOP_K), I32),
                   jax.ShapeDtypeStruct((n // tm, 1, ne), I32)],
        compiler_params=_params("parallel"),
        name="merge_router",
    )(x2d, oa, ob, ga.reshape(1, da), gb.reshape(1, da), wo_bf, gf.reshape(1, dm),
      w_router, b_router.reshape(1, ne))


def _tile_slots(ne):
    return ROW_TILE * TOP_K + ne * SEG


def _slots_of(idx, rank, seg_row):
    tm, ne = idx.shape[0], seg_row.shape[-1]
    lane = lax.broadcasted_iota(I32, (tm, ne), 1)
    return [jnp.sum(jnp.where(lane == idx[:, r:r + 1], seg_row, 0), axis=-1, keepdims=True)
            + rank[:, r:r + 1] for r in range(TOP_K)]


def _chunk_copy(src_ref, src_row, dst_ref, dst_row, sem):
    aligned = lambda row: row if isinstance(row, int) else pl.multiple_of(row, SEG)
    return pltpu.make_async_copy(src_ref.at[pl.ds(aligned(src_row), SEG), :],
                                 dst_ref.at[pl.ds(aligned(dst_row), SEG), :], sem)


def _moe_plan(cnt, n_rows):
    nt, ne = cnt.shape
    tm = MOE_TILE
    seg_len = (cnt + SEG - 1) // SEG * SEG
    seg_row = jnp.cumsum(seg_len, axis=1) - seg_len
    rows = jnp.sum(seg_len, axis=0)
    region = (rows + tm - 1) // tm * tm
    base = jnp.cumsum(region) - region
    dst_row = base[None, :] + jnp.cumsum(seg_len, axis=0) - seg_len
    n_tiles = (n_rows * TOP_K + ne * ((SEG - 1) * nt + tm - 1)) // tm
    tail = jnp.concatenate([base[1:], jnp.full((1,), n_tiles * tm, base.dtype)]) - (base + rows)
    tile_end = jnp.cumsum(region // tm)
    tiles = jnp.arange(n_tiles, dtype=I32)
    tile_expert = jnp.minimum(jnp.sum((tiles[:, None] >= tile_end[None, :]).astype(I32), axis=1),
                              ne - 1)
    flat = lambda a: a.reshape(-1).astype(I32)
    return dict(seg_row=seg_row.astype(I32).reshape(nt, 1, ne), seg_row_s=flat(seg_row),
                chunks_s=flat(seg_len // SEG), dst_row_s=flat(dst_row),
                tail_row_s=flat(base + rows), tail_chunks_s=flat(tail // SEG),
                tile_expert=tile_expert.astype(I32),
                tile_valid=(tiles < tile_end[ne - 1]).astype(I32), n_slots=n_tiles * tm)


def _dispatch_kernel(seg_s, chunks_s, dst_s, tail_row_s, tail_chunks_s,
                     xa_ref, xb_ref, idx_ref, rank_ref, seg_ref, xs_ref,
                     buf_ref, zero_ref, sem, *, tiles_a, ne):
    t = pl.program_id(0)
    last = pl.num_programs(0) - 1
    tm, ts = xa_ref.shape[0], buf_ref.shape[1]
    cur = t % 2
    slots = _slots_of(idx_ref[...], rank_ref[...], seg_ref[...])
    lane = lax.broadcasted_iota(I32, (tm, ts), 1)
    hit = (lane == slots[0]) | (lane == slots[1]) | (lane == slots[2]) | (lane == slots[3])
    place = jnp.where(hit, 1.0, 0.0).astype(BF16)

    def order_rows(x_ref):
        buf_ref[cur] = lax.dot_general(place, x_ref[...], (((0,), (0,)), ((), ())),
                                       preferred_element_type=F32)

    @pl.when(t < tiles_a)
    def _():
        order_rows(xa_ref)

    @pl.when(t >= tiles_a)
    def _():
        order_rows(xb_ref)

    def send(e, started):
        k = t * ne + e
        n_chunks = chunks_s[k]

        def start(c, carry):
            _chunk_copy(buf_ref.at[cur], seg_s[k] + c * SEG, xs_ref, dst_s[k] + c * SEG,
                        sem.at[cur]).start()
            return carry

        lax.fori_loop(0, n_chunks, start, 0)
        return started + n_chunks

    started = lax.fori_loop(0, ne, send, 0)

    def drain(slot, n_chunks):
        def wait(c, carry):
            _chunk_copy(buf_ref.at[slot], 0, xs_ref, 0, sem.at[slot]).wait()
            return carry

        lax.fori_loop(0, n_chunks, wait, 0)

    @pl.when(t > 0)
    def _():
        drain(1 - cur, lax.fori_loop(0, ne, lambda e, n: n + chunks_s[(t - 1) * ne + e], 0))

    @pl.when(t == last)
    def _():
        drain(cur, started)
        zero_ref[...] = jnp.zeros_like(zero_ref)

        def fill(e, filled):
            def start(c, carry):
                _chunk_copy(zero_ref, 0, xs_ref, tail_row_s[e] + c * SEG, sem.at[cur]).start()
                return carry

            lax.fori_loop(0, tail_chunks_s[e], start, 0)
            return filled + tail_chunks_s[e]

        def wait_fill(c, carry):
            _chunk_copy(zero_ref, 0, xs_ref, 0, sem.at[cur]).wait()
            return carry

        lax.fori_loop(0, lax.fori_loop(0, ne, fill, 0), wait_fill, 0)


def _dispatch(plan, xa, xb, idx, rank):
    (na, dm), nb = xa.shape, xb.shape[0]
    tm = ROW_TILE
    tiles_a, tiles_b = na // tm, nb // tm
    ne = plan["seg_row"].shape[-1]
    tile = lambda t, *_: (t, 0)
    return pl.pallas_call(
        functools.partial(_dispatch_kernel, tiles_a=tiles_a, ne=ne),
        grid_spec=pltpu.PrefetchScalarGridSpec(
            num_scalar_prefetch=5,
            grid=(tiles_a + tiles_b,),
            in_specs=[pl.BlockSpec((tm, dm), lambda t, *_: (jnp.minimum(t, tiles_a - 1), 0)),
                      pl.BlockSpec((tm, dm), lambda t, *_: (jnp.maximum(t - tiles_a, 0), 0)),
                      pl.BlockSpec((tm, TOP_K), tile),
                      pl.BlockSpec((tm, TOP_K), tile),
                      pl.BlockSpec((None, 1, ne), lambda t, *_: (t, 0, 0))],
            out_specs=pl.BlockSpec(memory_space=pl.ANY),
            scratch_shapes=[pltpu.VMEM((2, _tile_slots(ne), dm), F32),
                            pltpu.VMEM((SEG, dm), F32),
                            pltpu.SemaphoreType.DMA((2,))]),
        out_shape=jax.ShapeDtypeStruct((plan["n_slots"], dm), F32),
        compiler_params=_params("arbitrary"),
        name="moe_dispatch",
    )(plan["seg_row_s"], plan["chunks_s"], plan["dst_row_s"], plan["tail_row_s"],
      plan["tail_chunks_s"], xa, xb, idx, rank, plan["seg_row"])


def _ffn_kernel(te_ref, tv_ref, xs_ref, wg_ref, bg_ref, wu_ref, bu_ref, wd_ref, bd_ref,
                y_ref, wgb_ref, wub_ref, wdb_ref):
    t = pl.program_id(0)
    new_expert = (t == 0) | (te_ref[t] != te_ref[jnp.maximum(t - 1, 0)])

    @pl.when(new_expert)
    def _():
        wgb_ref[...] = wg_ref[...].astype(BF16)
        wub_ref[...] = wu_ref[...].astype(BF16)
        wdb_ref[...] = wd_ref[...].astype(BF16)

    @pl.when(tv_ref[t] != 0)
    def _():
        x = xs_ref[...].astype(BF16)
        g = jnp.minimum(jnp.dot(x, wgb_ref[...], preferred_element_type=F32) + bg_ref[...],
                        SWIGLU_LIMIT)
        u = jnp.clip(jnp.dot(x, wub_ref[...], preferred_element_type=F32) + bu_ref[...],
                     -SWIGLU_LIMIT, SWIGLU_LIMIT)
        h = g * jax.nn.sigmoid(SWIGLU_ALPHA * g) * (u + 1.0)
        y_ref[...] = jnp.dot(h.astype(BF16), wdb_ref[...], preferred_element_type=F32) + bd_ref[...]

    @pl.when(tv_ref[t] == 0)
    def _():
        y_ref[...] = jnp.zeros_like(y_ref)


def _expert_ffn(tile_expert, tile_valid, xs, wg, bg, wu, bu, wd, bd):
    n_slots, dm = xs.shape
    ne, _, dff = wg.shape
    tm = MOE_TILE
    row = lambda t, te, tv: (t, 0)
    exp3 = lambda t, te, tv: (te[t], 0, 0)
    return pl.pallas_call(
        _ffn_kernel,
        grid_spec=pltpu.PrefetchScalarGridSpec(
            num_scalar_prefetch=2,
            grid=(n_slots // tm,),
            in_specs=[pl.BlockSpec((tm, dm), row),
                      pl.BlockSpec((None, dm, dff), exp3),
                      pl.BlockSpec((None, 1, dff), exp3),
                      pl.BlockSpec((None, dm, dff), exp3),
                      pl.BlockSpec((None, 1, dff), exp3),
                      pl.BlockSpec((None, dff, dm), exp3),
                      pl.BlockSpec((None, 1, dm), exp3)],
            out_specs=pl.BlockSpec((tm, dm), row),
            scratch_shapes=[pltpu.VMEM((dm, dff), BF16),
                            pltpu.VMEM((dm, dff), BF16),
                            pltpu.VMEM((dff, dm), BF16)]),
        out_shape=jax.ShapeDtypeStruct((n_slots, dm), F32),
        compiler_params=_params("arbitrary"),
        name="moe_ffn",
    )(tile_expert, tile_valid, xs, wg, bg.reshape(ne, 1, dff), wu, bu.reshape(ne, 1, dff),
      wd, bd.reshape(ne, 1, dm))


def _combine_kernel(seg_s, chunks_s, dst_s, ys_ref, idx_ref, rank_ref, w_ref, seg_ref, x1_ref,
                    g_ref, o_ref, buf_ref, sem, *, tile0, ne):
    i = pl.program_id(0)
    t = tile0 + i
    tm, ts = x1_ref.shape[0], buf_ref.shape[0]

    @pl.when(i == 0)
    def _():
        buf_ref[...] = jnp.zeros_like(buf_ref)

    def fetch(e, started):
        k = t * ne + e
        n_chunks = chunks_s[k]

        def start(c, carry):
            _chunk_copy(ys_ref, dst_s[k] + c * SEG, buf_ref, seg_s[k] + c * SEG, sem).start()
            return carry

        lax.fori_loop(0, n_chunks, start, 0)
        return started + n_chunks

    def wait(c, carry):
        _chunk_copy(ys_ref, 0, buf_ref, 0, sem).wait()
        return carry

    started = lax.fori_loop(0, ne, fetch, 0)

    slots = _slots_of(idx_ref[...], rank_ref[...], seg_ref[...])
    lane = lax.broadcasted_iota(I32, (tm, ts), 1)
    w = w_ref[...]
    gate = jnp.zeros((tm, ts), F32)
    for r in range(TOP_K):
        gate = jnp.where(lane == slots[r], w[:, r:r + 1], gate)
    gate_hi, gate_lo = _split_bf16(gate)
    lax.fori_loop(0, started, wait, 0)
    yb = buf_ref[...].astype(BF16)
    y = (jnp.dot(gate_hi, yb, preferred_element_type=F32)
         + jnp.dot(gate_lo, yb, preferred_element_type=F32))
    o_ref[...] = _rms(x1_ref[...] + y, g_ref[...])


def _combine(plan, tile0, ys, idx, rank, wts, x1, gfin):
    n, dm = x1.shape
    tm = ROW_TILE
    ne = plan["seg_row"].shape[-1]
    tile = lambda i, *_: (i, 0)
    return pl.pallas_call(
        functools.partial(_combine_kernel, tile0=tile0, ne=ne),
        grid_spec=pltpu.PrefetchScalarGridSpec(
            num_scalar_prefetch=3,
            grid=(n // tm,),
            in_specs=[pl.BlockSpec(memory_space=pl.ANY),
                      pl.BlockSpec((tm, TOP_K), tile),
                      pl.BlockSpec((tm, TOP_K), tile),
                      pl.BlockSpec((tm, TOP_K), tile),
                      pl.BlockSpec((None, 1, ne), lambda i, *_: (tile0 + i, 0, 0)),
                      pl.BlockSpec((tm, dm), tile),
                      pl.BlockSpec((1, dm), lambda i, *_: (0, 0))],
            out_specs=pl.BlockSpec((tm, dm), tile),
            scratch_shapes=[pltpu.VMEM((_tile_slots(ne), dm), F32),
                            pltpu.SemaphoreType.DMA]),
        out_shape=jax.ShapeDtypeStruct((n, dm), F32),
        compiler_params=_params("arbitrary"),
        name="moe_combine",
    )(plan["seg_row_s"], plan["chunks_s"], plan["dst_row_s"], ys, idx, rank, wts,
      plan["seg_row"], x1, gfin.reshape(1, dm))


def kernel(x_prompt, x_sample, cache_a_k, cache_a_v, cache_b_k, cache_b_v, norm_attn, w_in,
           rel_bias, g_out_a, g_out_b, w_out, norm_ffn, w_router, b_router, w_gate, b_gate,
           w_up, b_up, w_down, b_down, norm_final):
    depth = w_in.shape[0]
    assert depth == 1, "the MoE combine applies the final norm: single layer only"
    bp, sp, dm = x_prompt.shape
    bs, ts, _ = x_sample.shape
    past = cache_b_k.shape[2]
    wa = cache_a_k.shape[2]
    nh_a, hd = cache_a_k.shape[3], cache_a_k.shape[4]
    nh_b = cache_b_k.shape[3]
    da, db = nh_a * hd, nh_b * hd
    ne = w_router.shape[-1]
    assert da == db and hd == HEAD_DIM and wa == BAND_WINDOW and ts == CHUNK
    slabs = da // LANES
    QA, KA, VA, QB, KB, VB = (c * slabs for c in range(6))
    l = 0
    n_p, n_s = bp * sp, bs * ts

    xp2 = x_prompt.reshape(n_p, dm)
    xs2 = x_sample.reshape(n_s, dm)
    w_in_bf = w_in[l].astype(BF16)
    qkv_p, ka_p, va_p, kb_p, vb_p = _qkv_proj(xp2, norm_attn[l], w_in_bf)
    qkv_s, ka_s, va_s, kb_s, vb_s = _qkv_proj(xs2, norm_attn[l], w_in_bf)
    qkv_p3 = qkv_p.reshape(bp, sp, 6 * da)
    qkv_s3 = qkv_s.reshape(bs, ts, 6 * da)

    bias = _band_bias(rel_bias[l])
    tile_bias = _band_tile_bias(bias)

    oa_p = _band_attention_prompt(qkv_p3, QA, KA, VA, tile_bias)
    ob_p = _sb_attention(qkv_p3, QB, qkv_p3, KB, qkv_p3, VB, qkv_p3, KB, qkv_p3, VB,
                         slabs, tq=min(SB_TQ, sp), q_base=0)

    oa_s = _band_attention_sample(qkv_s3, QA, KA, VA, cache_a_k[l].reshape(bs, wa, da),
                                  cache_a_v[l].reshape(bs, wa, da), bias)
    ob_s = _sb_attention(qkv_s3, QB, qkv_s3, KB, qkv_s3, VB,
                         cache_b_k[l].reshape(bs, past, db), 0,
                         cache_b_v[l].reshape(bs, past, db), 0, slabs, tq=ts, q_base=past)

    w_out_bf = w_out[l].astype(BF16)
    merge = functools.partial(_merge, ga=g_out_a[l], gb=g_out_b[l], wo_bf=w_out_bf, gf=norm_ffn[l],
                              w_router=w_router[l], b_router=b_router[l])
    x1_p, xn_p, idx_p, wts_p, rank_p, cnt_p = merge(
        xp2, oa_p.reshape(n_p, da), ob_p.reshape(n_p, db))
    x1_s, xn_s, idx_s, wts_s, rank_s, cnt_s = merge(
        xs2, oa_s.reshape(n_s, da), ob_s.reshape(n_s, db))
    plan = _moe_plan(jnp.concatenate([cnt_p, cnt_s]).reshape(-1, ne), n_p + n_s)
    xs = _dispatch(plan, xn_p, xn_s, jnp.concatenate([idx_p, idx_s]),
                   jnp.concatenate([rank_p, rank_s]))
    ys = _expert_ffn(plan["tile_expert"], plan["tile_valid"], xs, w_gate[l], b_gate[l],
                     w_up[l], b_up[l], w_down[l], b_down[l])
    y_prompt = _combine(plan, 0, ys, idx_p, rank_p, wts_p, x1_p, norm_final).reshape(bp, sp, dm)
    y_sample = _combine(plan, n_p // ROW_TILE, ys, idx_s, rank_s, wts_s, x1_s,
                        norm_final).reshape(bs, ts, dm)

    wp = min(BAND_WINDOW, sp)
    heads_a = lambda t, b_, n_: t.reshape(b_, n_, nh_a, hd)
    heads_b = lambda t, b_, n_: t.reshape(b_, n_, nh_b, hd)
    new_a_k_prompt = heads_a(ka_p, bp, sp)[:, sp - wp:][None]
    new_a_v_prompt = heads_a(va_p, bp, sp)[:, sp - wp:][None]
    new_b_k_prompt = heads_b(kb_p, bp, sp)[None]
    new_b_v_prompt = heads_b(vb_p, bp, sp)[None]
    new_a_k_sample = jnp.concatenate([cache_a_k[l], heads_a(ka_s, bs, ts)], axis=1)[:, ts:][None]
    new_a_v_sample = jnp.concatenate([cache_a_v[l], heads_a(va_s, bs, ts)], axis=1)[:, ts:][None]
    new_b_k_sample = heads_b(kb_s, bs, ts)[None]
    new_b_v_sample = heads_b(vb_s, bs, ts)[None]
    return (y_prompt, y_sample, new_a_k_prompt, new_a_v_prompt, new_b_k_prompt, new_b_v_prompt,
            new_a_k_sample, new_a_v_sample, new_b_k_sample, new_b_v_sample)
```

```python
import functools

import jax
import jax.numpy as jnp
from jax import lax
from jax.experimental import pallas as pl
from jax.experimental.pallas import tpu as pltpu

F32 = jnp.float32
BF16 = jnp.bfloat16
I32 = jnp.int32

CHUNK = 64
LEFT_CHUNKS = 8
BAND_WINDOW = LEFT_CHUNKS * CHUNK
BAND = BAND_WINDOW + CHUNK
HEAD_DIM = 64
HEADS_PER_SLAB = 2
LANES = 128
REL_CLIP = 128
TOP_K = 4
SWIGLU_LIMIT = 7.0
SWIGLU_ALPHA = 1.702
EPS = 1e-6
SCALE = HEAD_DIM ** -0.5

ROW_TILE = 512
BAND_TQ = 256
BAND_KEYS = BAND_TQ + BAND_WINDOW
BAND_VARIANTS = BAND_WINDOW // BAND_TQ + 1
BAND_SLABS = 2
SB_TQ = 512
SB_TK = 256
SB_UNROLL = 1
SB_DEAD = -104.0
MOE_TILE = 512
SEG = 8
VMEM_LIMIT = 56 * 1024 * 1024


def _params(*sem):
    return pltpu.CompilerParams(dimension_semantics=sem, vmem_limit_bytes=VMEM_LIMIT)


def _rms(x, g):
    ms = jnp.mean(x * x, axis=-1, keepdims=True)
    return (x * lax.rsqrt(ms + EPS)) * g


def _head_mask(h):
    lane = lax.broadcasted_iota(I32, (1, LANES), 1)
    return (lane >= h * HEAD_DIM) & (lane < (h + 1) * HEAD_DIM)


def _qkv_kernel(x_ref, g_ref, w_ref, qkv_ref, ka_ref, va_ref, kb_ref, vb_ref):
    xb = _rms(x_ref[...], g_ref[...]).astype(BF16)
    d = ka_ref.shape[-1]
    f32_out = {1: ka_ref, 2: va_ref, 4: kb_ref, 5: vb_ref}
    for c in range(6):
        r = jnp.dot(xb, w_ref[:, c * d:(c + 1) * d], preferred_element_type=F32)
        if c in f32_out:
            f32_out[c][...] = r
        else:
            r = r * SCALE
        qkv_ref[:, c * d:(c + 1) * d] = r.astype(BF16)


def _qkv_proj(x2d, g, w_bf, seq_tiles=1, band_tiles=1):
    n, dm = x2d.shape
    d3 = w_bf.shape[1]
    d = d3 // 6
    tm = min(ROW_TILE, n)
    assert (n // tm) % seq_tiles == 0 and band_tiles <= seq_tiles
    row = lambda i: (i, 0)
    fixed = lambda i: (0, 0)
    skip = seq_tiles - band_tiles
    band_row = lambda i: ((i // seq_tiles) * band_tiles + jnp.maximum(i % seq_tiles - skip, 0), 0)
    n_band = n // seq_tiles * band_tiles
    return pl.pallas_call(
        _qkv_kernel,
        grid=(n // tm,),
        in_specs=[pl.BlockSpec((tm, dm), row),
                  pl.BlockSpec((1, dm), fixed),
                  pl.BlockSpec((dm, d3), fixed)],
        out_specs=[pl.BlockSpec((tm, d3), row)] + [pl.BlockSpec((tm, d), band_row)] * 2
        + [pl.BlockSpec((tm, d), row)] * 2,
        out_shape=[jax.ShapeDtypeStruct((n, d3), BF16)]
        + [jax.ShapeDtypeStruct((n_band, d), F32)] * 2
        + [jax.ShapeDtypeStruct((n, d), F32)] * 2,
        compiler_params=_params("arbitrary"),
        name="qkv_proj",
    )(x2d, g.reshape(1, dm), w_bf)


def _bias_kernel(tab_ref, o_ref):
    h = pl.program_id(0)
    i = lax.broadcasted_iota(I32, (CHUNK, BAND), 0)
    j = lax.broadcasted_iota(I32, (CHUNK, BAND), 1)
    idx = jnp.clip(BAND_WINDOW + i - j, -REL_CLIP, REL_CLIP) + REL_CLIP
    n_tab = 2 * REL_CLIP + 1

    def body(t, acc):
        return jnp.where(idx == t, tab_ref[h, t], acc)

    init = jnp.full((CHUNK, BAND), tab_ref[h, n_tab - 1], F32)
    o_ref[...] = lax.fori_loop(0, n_tab - 1, body, init)


def _band_bias(table):
    nh = table.shape[0]
    return pl.pallas_call(
        _bias_kernel,
        grid=(nh,),
        in_specs=[pl.BlockSpec(memory_space=pltpu.SMEM)],
        out_specs=pl.BlockSpec((None, CHUNK, BAND), lambda h: (h, 0, 0)),
        out_shape=jax.ShapeDtypeStruct((nh, CHUNK, BAND), F32),
        compiler_params=_params("arbitrary"),
        name="band_bias",
    )(table)


def _tile_bias_kernel(b_ref, o_ref):
    v = pl.program_id(1)
    o_ref[...] = jnp.full(o_ref.shape, -jnp.inf, F32)
    for c in range(BAND_TQ // CHUNK):
        o_ref[c * CHUNK:(c + 1) * CHUNK, c * CHUNK:c * CHUNK + BAND] = b_ref[...]
    col = lax.broadcasted_iota(I32, o_ref.shape, 1)
    hidden = (BAND_VARIANTS - 1 - v) * BAND_TQ
    o_ref[...] = jnp.where(col >= hidden, o_ref[...], -jnp.inf)


def _band_tile_bias(bias):
    nh = bias.shape[0]
    return pl.pallas_call(
        _tile_bias_kernel,
        grid=(nh, BAND_VARIANTS),
        in_specs=[pl.BlockSpec((None, CHUNK, BAND), lambda h, v: (h, 0, 0))],
        out_specs=pl.BlockSpec((None, None, BAND_TQ, BAND_KEYS), lambda h, v: (h, v, 0, 0)),
        out_shape=jax.ShapeDtypeStruct((nh, BAND_VARIANTS, BAND_TQ, BAND_KEYS), F32),
        compiler_params=_params("parallel", "arbitrary"),
        name="band_tile_bias",
    )(bias)


def _band_tile_kernel(q_ref, *refs):
    n_kb = BAND_KEYS // BAND_TQ
    k_refs, v_refs = refs[:n_kb], refs[n_kb:2 * n_kb]
    bias_ref, o_ref = refs[2 * n_kb], refs[2 * n_kb + 1]
    for sl in range(BAND_SLABS):
        cols = slice(sl * LANES, (sl + 1) * LANES)
        q = q_ref[:, cols]
        kcat = jnp.concatenate([r[:, cols] for r in k_refs], axis=0)
        vcat = jnp.concatenate([r[:, cols] for r in v_refs], axis=0)
        outs = []
        for h in range(HEADS_PER_SLAB):
            qm = jnp.where(_head_mask(h), q, jnp.zeros_like(q))
            s = lax.dot_general(qm, kcat, (((1,), (1,)), ((), ())),
                                preferred_element_type=F32) + bias_ref[sl * HEADS_PER_SLAB + h]
            m = jnp.max(s, axis=-1, keepdims=True)
            p = jnp.exp(s - m)
            l = jnp.sum(p, axis=-1, keepdims=True)
            pv = jnp.dot(p.astype(BF16), vcat, preferred_element_type=F32)
            outs.append(pv / l)
        o_ref[:, cols] = jnp.where(_head_mask(0), outs[0], outs[1])


def _band_attention_prompt(qkv3, q_col0, k_col0, v_col0, tile_bias):
    b, t, _ = qkv3.shape
    nslab = tile_bias.shape[0] // HEADS_PER_SLAB
    n_kb = BAND_KEYS // BAND_TQ
    assert nslab % BAND_SLABS == 0 and all(c % BAND_SLABS == 0 for c in (q_col0, k_col0, v_col0))
    blk = (None, BAND_TQ, BAND_SLABS * LANES)
    group = lambda col0, g: col0 // BAND_SLABS + g

    def kv_spec(col0, back):
        return pl.BlockSpec(blk, lambda g, bi, j: (bi, jnp.maximum(j - back, 0), group(col0, g)))

    in_specs = [pl.BlockSpec(blk, lambda g, bi, j: (bi, j, group(q_col0, g)))]
    in_specs += [kv_spec(k_col0, n_kb - 1 - i) for i in range(n_kb)]
    in_specs += [kv_spec(v_col0, n_kb - 1 - i) for i in range(n_kb)]
    in_specs += [pl.BlockSpec((BAND_SLABS * HEADS_PER_SLAB, None, BAND_TQ, BAND_KEYS),
                              lambda g, bi, j: (g, jnp.minimum(j, BAND_VARIANTS - 1), 0, 0))]
    return pl.pallas_call(
        _band_tile_kernel,
        grid=(nslab // BAND_SLABS, b, t // BAND_TQ),
        in_specs=in_specs,
        out_specs=pl.BlockSpec(blk, lambda g, bi, j: (bi, j, g)),
        out_shape=jax.ShapeDtypeStruct((b, t, nslab * LANES), F32),
        compiler_params=_params("parallel", "parallel", "arbitrary"),
        name="band_attention_prompt",
    )(qkv3, *([qkv3] * (2 * n_kb)), tile_bias)


def _band_chunk_kernel(q_ref, kp_ref, kc_ref, vp_ref, vc_ref, bias_ref, o_ref):
    def cached(ref, sl):
        heads = [ref[:, sl * HEADS_PER_SLAB + h, :] for h in range(HEADS_PER_SLAB)]
        return jnp.concatenate(heads, axis=-1).astype(BF16)

    for sl in range(o_ref.shape[-1] // LANES):
        cols = slice(sl * LANES, (sl + 1) * LANES)
        qc = q_ref[:, cols]
        kwin = jnp.concatenate([cached(kp_ref, sl), kc_ref[:, cols]], axis=0)
        vwin = jnp.concatenate([cached(vp_ref, sl), vc_ref[:, cols]], axis=0)
        outs = []
        for h in range(HEADS_PER_SLAB):
            qm = jnp.where(_head_mask(h), qc, jnp.zeros_like(qc))
            s = lax.dot_general(qm, kwin, (((1,), (1,)), ((), ())),
                                preferred_element_type=F32) + bias_ref[sl * HEADS_PER_SLAB + h]
            m = jnp.max(s, axis=-1, keepdims=True)
            p = jnp.exp(s - m)
            l = jnp.sum(p, axis=-1, keepdims=True)
            pv = jnp.dot(p.astype(BF16), vwin, preferred_element_type=F32)
            outs.append(pv / l)
        o_ref[:, cols] = jnp.where(_head_mask(0), outs[0], outs[1])


def _band_attention_sample(qkv3, q_col0, k_col0, v_col0, cache_k, cache_v, bias):
    b, t, _ = qkv3.shape
    nh = bias.shape[0]
    d = nh // HEADS_PER_SLAB * LANES
    assert all(c * LANES % d == 0 for c in (q_col0, k_col0, v_col0))
    new = lambda col0: pl.BlockSpec((None, CHUNK, d), lambda bi: (bi, 0, col0 * LANES // d))
    old = pl.BlockSpec((None, BAND_WINDOW, nh, HEAD_DIM), lambda bi: (bi, 0, 0, 0))
    return pl.pallas_call(
        _band_chunk_kernel,
        grid=(b,),
        in_specs=[new(q_col0), old, new(k_col0), old, new(v_col0),
                  pl.BlockSpec((nh, CHUNK, BAND), lambda bi: (0, 0, 0))],
        out_specs=pl.BlockSpec((None, CHUNK, d), lambda bi: (bi, 0, 0)),
        out_shape=jax.ShapeDtypeStruct((b, t, d), F32),
        compiler_params=_params("parallel"),
        name="band_attention_sample",
    )(qkv3, cache_k, qkv3, cache_v, qkv3, bias)


def _sb_kernel(q_ref, kt_ref, vt_ref, kf_ref, vf_ref, o_ref,
               qm_ref, nu_ref, d_ref, tot_ref, acc_ref, run_ref, *, q_base):
    i = pl.program_id(2)
    tq = q_ref.shape[0]
    tile_keys = kt_ref.shape[0]
    tk = SB_TK
    qpos0 = q_base + i * tq
    n_full = qpos0 // tk
    n_masked = -(-tile_keys // tk)

    q = q_ref[...]
    for h in range(HEADS_PER_SLAB):
        qm_ref[h] = jnp.where(_head_mask(h), q, jnp.zeros_like(q))
    jj = lax.broadcasted_iota(I32, (tk, tk), 0)
    ss = lax.broadcasted_iota(I32, (tk, tk), 1)
    nu_ref[...] = jnp.where(jj >= ss, -1.0, 0.0).astype(BF16)
    acc_ref[...] = jnp.zeros_like(acc_ref)
    run_ref[...] = jnp.zeros_like(run_ref)

    def tile_block(ref, m):
        rows = min(tk, tile_keys - m * tk)
        blk = ref[m * tk:m * tk + rows, :].astype(BF16)
        if rows < tk:
            blk = jnp.concatenate([blk, jnp.zeros((tk - rows, LANES), BF16)], axis=0)
        return blk

    def full_block(ref, kb):
        k0 = pl.multiple_of(kb * tk, tk)
        return ref[pl.ds(k0, tk), :].astype(BF16)

    def stage_a(kblk, tile_m):
        if tile_m is not None:
            row = lax.broadcasted_iota(I32, (tq, tk), 0)
            col = lax.broadcasted_iota(I32, (tq, tk), 1)
            vis = tile_m * tk + col < row
        for h in range(HEADS_PER_SLAB):
            z = lax.dot_general(qm_ref[h], kblk, (((1,), (1,)), ((), ())),
                                preferred_element_type=F32)
            sp = jnp.maximum(z, 0.0) + jnp.log(1.0 + jnp.exp(-jnp.abs(z)))
            if tile_m is not None:
                sp = jnp.where(vis, sp, 0.0)
            after = jnp.dot(sp.astype(BF16), nu_ref[...], preferred_element_type=F32)
            d = z + after
            if tile_m is not None:
                d = jnp.where(vis, d, -jnp.inf)
            d_ref[h] = d
            tot_ref[h] = after[:, 0:1]

    def stage_b(vblk):
        for h in range(HEADS_PER_SLAB):
            run = run_ref[h]
            a = jnp.exp(d_ref[h] + run)
            acc_ref[h] += jnp.dot(a.astype(BF16), vblk, preferred_element_type=F32)
            run_ref[h] = run + tot_ref[h]

    def step(kb):
        stage_b(full_block(vf_ref, kb + 1))
        stage_a(full_block(kf_ref, kb), None)

    def alive():
        r = jnp.maximum(run_ref[0], run_ref[1])
        return (jnp.max(r) >= SB_DEAD).astype(I32)

    stage_a(tile_block(kt_ref, n_masked - 1), n_masked - 1)
    for m in range(n_masked - 2, -1, -1):
        stage_b(tile_block(vt_ref, m + 1))
        stage_a(tile_block(kt_ref, m), m)

    @pl.when(n_full == 0)
    def _():
        stage_b(tile_block(vt_ref, 0))

    @pl.when(n_full > 0)
    def _():
        stage_b(tile_block(vt_ref, 0))
        stage_a(full_block(kf_ref, n_full - 1), None)

        def cond(c):
            kb, live = c
            return (kb >= SB_UNROLL - 1) & (live > 0)

        def body(c):
            kb, _ = c
            for u in range(SB_UNROLL):
                step(kb - u)
            return kb - SB_UNROLL, alive()

        kb, live = lax.while_loop(cond, body, (n_full - 2, alive()))
        for _ in range(SB_UNROLL - 1):
            go = (kb >= 0) & (live > 0)

            @pl.when(go)
            def _():
                step(kb)

            kb = jnp.where(go, kb - 1, kb)
            live = alive()

        @pl.when(live > 0)
        def _():
            stage_b(full_block(vf_ref, kb + 1))

    o_ref[...] = jnp.where(_head_mask(0), acc_ref[0], acc_ref[1])


def _sb_attention(q_arr, q_col0, kt_arr, kt_col0, vt_arr, vt_col0,
                  kf_arr, kf_col0, vf_arr, vf_col0, nslab, tq, q_base):
    b, t, _ = q_arr.shape
    tf = kf_arr.shape[1]
    assert q_base % SB_TK == 0 and t % tq == 0 and tf % SB_TK == 0
    assert tq % SB_TK == 0 or t == tq
    assert q_base + t - tq <= tf
    tile = lambda col0: pl.BlockSpec((None, tq, LANES), lambda bi, hp, i: (bi, i, col0 + hp))
    full = lambda col0: pl.BlockSpec((None, tf, LANES), lambda bi, hp, i: (bi, 0, col0 + hp))
    return pl.pallas_call(
        functools.partial(_sb_kernel, q_base=q_base),
        grid=(b, nslab, t // tq),
        in_specs=[tile(q_col0), tile(kt_col0), tile(vt_col0), full(kf_col0), full(vf_col0)],
        out_specs=pl.BlockSpec((None, tq, LANES), lambda bi, hp, i: (bi, i, hp)),
        out_shape=jax.ShapeDtypeStruct((b, t, nslab * LANES), F32),
        scratch_shapes=[pltpu.VMEM((HEADS_PER_SLAB, tq, LANES), BF16),
                        pltpu.VMEM((SB_TK, SB_TK), BF16),
                        pltpu.VMEM((HEADS_PER_SLAB, tq, SB_TK), F32),
                        pltpu.VMEM((HEADS_PER_SLAB, tq, 1), F32),
                        pltpu.VMEM((HEADS_PER_SLAB, tq, LANES), F32),
                        pltpu.VMEM((HEADS_PER_SLAB, tq, 1), F32)],
        compiler_params=_params("parallel", "parallel", "arbitrary"),
        name="stick_breaking",
    )(q_arr, kt_arr, vt_arr, kf_arr, vf_arr)


def _split_bf16(x):
    hi = x.astype(BF16)
    return hi, (x - hi.astype(F32)).astype(BF16)


def _merge_kernel(x_ref, oa_ref, ob_ref, ga_ref, gb_ref, wo_ref, gf_ref, wr_ref, br_ref,
                  x1_ref, xn_ref, idx_ref, w_ref, rank_ref, cnt_ref):
    o = jnp.concatenate([_rms(oa_ref[...], ga_ref[...]), _rms(ob_ref[...], gb_ref[...])],
                        axis=-1).astype(BF16)
    x1 = x_ref[...] + jnp.dot(o, wo_ref[...], preferred_element_type=F32)
    x1_ref[...] = x1
    xn = _rms(x1, gf_ref[...])
    xn_ref[...] = xn.astype(BF16)
    xh, xl = _split_bf16(xn)
    wh, wl = _split_bf16(wr_ref[...])
    logits = (jnp.dot(xh, wh, preferred_element_type=F32)
              + jnp.dot(xl, wh, preferred_element_type=F32)
              + jnp.dot(xh, wl, preferred_element_type=F32)) + br_ref[...]
    tm, ne = logits.shape
    lane = lax.broadcasted_iota(I32, (tm, ne), 1)
    slot = lax.broadcasted_iota(I32, (tm, TOP_K), 1)
    work = logits
    top = None
    denom = None
    sels = []
    idx = jnp.zeros((tm, TOP_K), I32)
    wts = jnp.zeros((tm, TOP_K), F32)
    for r in range(TOP_K):
        m = jnp.max(work, axis=-1, keepdims=True)
        first = jnp.min(jnp.where(work == m, lane, ne), axis=-1, keepdims=True)
        sel = lane == first
        if top is None:
            top = m
        e = jnp.exp(m - top)
        denom = e if denom is None else denom + e
        idx = jnp.where(slot == r, first, idx)
        wts = jnp.where(slot == r, e, wts)
        sels.append(sel)
        work = jnp.where(sel, -jnp.inf, work)
    idx_ref[...] = idx
    w_ref[...] = wts / denom
    chosen = sels[0] | sels[1] | sels[2] | sels[3]
    onehot = jnp.where(chosen, 1.0, 0.0)
    rr = lax.broadcasted_iota(I32, (tm, tm), 0)
    cc = lax.broadcasted_iota(I32, (tm, tm), 1)
    lower = jnp.where(cc < rr, 1.0, 0.0).astype(BF16)
    before = jnp.dot(lower, onehot.astype(BF16), preferred_element_type=F32)
    rank = jnp.zeros((tm, TOP_K), I32)
    for r in range(TOP_K):
        rk = jnp.sum(jnp.where(sels[r], before, 0.0), axis=-1, keepdims=True)
        rank = jnp.where(slot == r, rk.astype(I32), rank)
    rank_ref[...] = rank
    cnt_ref[...] = jnp.sum(onehot, axis=0, keepdims=True).astype(I32)


def _merge(x2d, oa, ob, ga, gb, wo_bf, gf, w_router, b_router):
    n, dm = x2d.shape
    da = oa.shape[1]
    ne = w_router.shape[1]
    tm = ROW_TILE
    assert n % tm == 0
    row = lambda i: (i, 0)
    fixed = lambda i: (0, 0)
    return pl.pallas_call(
        _merge_kernel,
        grid=(n // tm,),
        in_specs=[pl.BlockSpec((tm, dm), row),
                  pl.BlockSpec((tm, da), row),
                  pl.BlockSpec((tm, da), row),
                  pl.BlockSpec((1, da), fixed),
                  pl.BlockSpec((1, da), fixed),
                  pl.BlockSpec((2 * da, dm), fixed),
                  pl.BlockSpec((1, dm), fixed),
                  pl.BlockSpec((dm, ne), fixed),
                  pl.BlockSpec((1, ne), fixed)],
        out_specs=[pl.BlockSpec((tm, dm), row),
                   pl.BlockSpec((tm, dm), row),
                   pl.BlockSpec((tm, TOP_K), row),
                   pl.BlockSpec((tm, TOP_K), row),
                   pl.BlockSpec((tm, TOP_K), row),
                   pl.BlockSpec((None, 1, ne), lambda i: (i, 0, 0))],
        out_shape=[jax.ShapeDtypeStruct((n, dm), F32),
                   jax.ShapeDtypeStruct((n, dm), BF16),
                   jax.ShapeDtypeStruct((n, TOP_K), I32),
                   jax.ShapeDtypeStruct((n, TOP_K), F32),
                   jax.ShapeDtypeStruct((n, TOP_K), I32),
                   jax.ShapeDtypeStruct((n // tm, 1, ne), I32)],
        compiler_params=_params("parallel"),
        name="merge_router",
    )(x2d, oa, ob, ga.reshape(1, da), gb.reshape(1, da), wo_bf, gf.reshape(1, dm),
      w_router, b_router.reshape(1, ne))


def _tile_slots(ne):
    return ROW_TILE * TOP_K + ne * SEG


def _slots_of(idx, rank, seg_row):
    tm, ne = idx.shape[0], seg_row.shape[-1]
    lane = lax.broadcasted_iota(I32, (tm, ne), 1)
    return [jnp.sum(jnp.where(lane == idx[:, r:r + 1], seg_row, 0), axis=-1, keepdims=True)
            + rank[:, r:r + 1] for r in range(TOP_K)]


def _chunk_copy(src_ref, src_row, dst_ref, dst_row, sem):
    aligned = lambda row: row if isinstance(row, int) else pl.multiple_of(row, SEG)
    return pltpu.make_async_copy(src_ref.at[pl.ds(aligned(src_row), SEG), :],
                                 dst_ref.at[pl.ds(aligned(dst_row), SEG), :], sem)


def _moe_plan(cnt, n_rows):
    nt, ne = cnt.shape
    tm = MOE_TILE
    seg_len = (cnt + SEG - 1) // SEG * SEG
    seg_row = jnp.cumsum(seg_len, axis=1) - seg_len
    rows = jnp.sum(seg_len, axis=0)
    region = (rows + tm - 1) // tm * tm
    base = jnp.cumsum(region) - region
    dst_row = base[None, :] + jnp.cumsum(seg_len, axis=0) - seg_len
    n_tiles = (n_rows * TOP_K + ne * ((SEG - 1) * nt + tm - 1)) // tm
    tail = jnp.concatenate([base[1:], jnp.full((1,), n_tiles * tm, base.dtype)]) - (base + rows)
    tile_end = jnp.cumsum(region // tm)
    tiles = jnp.arange(n_tiles, dtype=I32)
    tile_expert = jnp.minimum(jnp.sum((tiles[:, None] >= tile_end[None, :]).astype(I32), axis=1),
                              ne - 1)
    flat = lambda a: a.reshape(-1).astype(I32)
    return dict(seg_row=seg_row.astype(I32).reshape(nt, 1, ne), seg_row_s=flat(seg_row),
                chunks_s=flat(seg_len // SEG), dst_row_s=flat(dst_row),
                tail_row_s=flat(base + rows), tail_chunks_s=flat(tail // SEG),
                tile_expert=tile_expert.astype(I32),
                tile_valid=(tiles < tile_end[ne - 1]).astype(I32), n_slots=n_tiles * tm)


def _dispatch_kernel(seg_s, chunks_s, dst_s, tail_row_s, tail_chunks_s,
                     xa_ref, xb_ref, idx_ref, rank_ref, seg_ref, xs_ref,
                     buf_ref, zero_ref, sem, *, tiles_a, ne):
    t = pl.program_id(0)
    last = pl.num_programs(0) - 1
    tm, ts = xa_ref.shape[0], buf_ref.shape[1]
    cur = t % 2
    slots = _slots_of(idx_ref[...], rank_ref[...], seg_ref[...])
    lane = lax.broadcasted_iota(I32, (tm, ts), 1)
    hit = (lane == slots[0]) | (lane == slots[1]) | (lane == slots[2]) | (lane == slots[3])
    place = jnp.where(hit, 1.0, 0.0).astype(BF16)

    def order_rows(x_ref):
        buf_ref[cur] = lax.dot_general(place, x_ref[...], (((0,), (0,)), ((), ())),
                                       preferred_element_type=F32)

    @pl.when(t < tiles_a)
    def _():
        order_rows(xa_ref)

    @pl.when(t >= tiles_a)
    def _():
        order_rows(xb_ref)

    def send(e, started):
        k = t * ne + e
        n_chunks = chunks_s[k]

        def start(c, carry):
            _chunk_copy(buf_ref.at[cur], seg_s[k] + c * SEG, xs_ref, dst_s[k] + c * SEG,
                        sem.at[cur]).start()
            return carry

        lax.fori_loop(0, n_chunks, start, 0)
        return started + n_chunks

    started = lax.fori_loop(0, ne, send, 0)

    def drain(slot, n_chunks):
        def wait(c, carry):
            _chunk_copy(buf_ref.at[slot], 0, xs_ref, 0, sem.at[slot]).wait()
            return carry

        lax.fori_loop(0, n_chunks, wait, 0)

    @pl.when(t > 0)
    def _():
        drain(1 - cur, lax.fori_loop(0, ne, lambda e, n: n + chunks_s[(t - 1) * ne + e], 0))

    @pl.when(t == last)
    def _():
        drain(cur, started)
        zero_ref[...] = jnp.zeros_like(zero_ref)

        def fill(e, filled):
            def start(c, carry):
                _chunk_copy(zero_ref, 0, xs_ref, tail_row_s[e] + c * SEG, sem.at[cur]).start()
                return carry

            lax.fori_loop(0, tail_chunks_s[e], start, 0)
            return filled + tail_chunks_s[e]

        def wait_fill(c, carry):
            _chunk_copy(zero_ref, 0, xs_ref, 0, sem.at[cur]).wait()
            return carry

        lax.fori_loop(0, lax.fori_loop(0, ne, fill, 0), wait_fill, 0)


def _dispatch(plan, xa, xb, idx, rank):
    (na, dm), nb = xa.shape, xb.shape[0]
    tm = ROW_TILE
    tiles_a, tiles_b = na // tm, nb // tm
    ne = plan["seg_row"].shape[-1]
    tile = lambda t, *_: (t, 0)
    return pl.pallas_call(
        functools.partial(_dispatch_kernel, tiles_a=tiles_a, ne=ne),
        grid_spec=pltpu.PrefetchScalarGridSpec(
            num_scalar_prefetch=5,
            grid=(tiles_a + tiles_b,),
            in_specs=[pl.BlockSpec((tm, dm), lambda t, *_: (jnp.minimum(t, tiles_a - 1), 0)),
                      pl.BlockSpec((tm, dm), lambda t, *_: (jnp.maximum(t - tiles_a, 0), 0)),
                      pl.BlockSpec((tm, TOP_K), tile),
                      pl.BlockSpec((tm, TOP_K), tile),
                      pl.BlockSpec((None, 1, ne), lambda t, *_: (t, 0, 0))],
            out_specs=pl.BlockSpec(memory_space=pl.ANY),
            scratch_shapes=[pltpu.VMEM((2, _tile_slots(ne), dm), F32),
                            pltpu.VMEM((SEG, dm), F32),
                            pltpu.SemaphoreType.DMA((2,))]),
        out_shape=jax.ShapeDtypeStruct((plan["n_slots"], dm), F32),
        compiler_params=_params("arbitrary"),
        name="moe_dispatch",
    )(plan["seg_row_s"], plan["chunks_s"], plan["dst_row_s"], plan["tail_row_s"],
      plan["tail_chunks_s"], xa, xb, idx, rank, plan["seg_row"])


def _ffn_kernel(te_ref, tv_ref, xs_ref, wg_ref, bg_ref, wu_ref, bu_ref, wd_ref, bd_ref,
                y_ref, wgb_ref, wub_ref, wdb_ref):
    t = pl.program_id(0)
    new_expert = (t == 0) | (te_ref[t] != te_ref[jnp.maximum(t - 1, 0)])

    @pl.when(new_expert)
    def _():
        wgb_ref[...] = wg_ref[...].astype(BF16)
        wub_ref[...] = wu_ref[...].astype(BF16)
        wdb_ref[...] = wd_ref[...].astype(BF16)

    @pl.when(tv_ref[t] != 0)
    def _():
        x = xs_ref[...].astype(BF16)
        g = jnp.minimum(jnp.dot(x, wgb_ref[...], preferred_element_type=F32) + bg_ref[...],
                        SWIGLU_LIMIT)
        u = jnp.clip(jnp.dot(x, wub_ref[...], preferred_element_type=F32) + bu_ref[...],
                     -SWIGLU_LIMIT, SWIGLU_LIMIT)
        h = g * jax.nn.sigmoid(SWIGLU_ALPHA * g) * (u + 1.0)
        y_ref[...] = jnp.dot(h.astype(BF16), wdb_ref[...], preferred_element_type=F32) + bd_ref[...]

    @pl.when(tv_ref[t] == 0)
    def _():
        y_ref[...] = jnp.zeros_like(y_ref)


def _expert_ffn(tile_expert, tile_valid, xs, wg, bg, wu, bu, wd, bd):
    n_slots, dm = xs.shape
    ne, _, dff = wg.shape
    tm = MOE_TILE
    row = lambda t, te, tv: (t, 0)
    exp3 = lambda t, te, tv: (te[t], 0, 0)
    return pl.pallas_call(
        _ffn_kernel,
        grid_spec=pltpu.PrefetchScalarGridSpec(
            num_scalar_prefetch=2,
            grid=(n_slots // tm,),
            in_specs=[pl.BlockSpec((tm, dm), row),
                      pl.BlockSpec((None, dm, dff), exp3),
                      pl.BlockSpec((None, 1, dff), exp3),
                      pl.BlockSpec((None, dm, dff), exp3),
                      pl.BlockSpec((None, 1, dff), exp3),
                      pl.BlockSpec((None, dff, dm), exp3),
                      pl.BlockSpec((None, 1, dm), exp3)],
            out_specs=pl.BlockSpec((tm, dm), row),
            scratch_shapes=[pltpu.VMEM((dm, dff), BF16),
                            pltpu.VMEM((dm, dff), BF16),
                            pltpu.VMEM((dff, dm), BF16)]),
        out_shape=jax.ShapeDtypeStruct((n_slots, dm), F32),
        compiler_params=_params("arbitrary"),
        name="moe_ffn",
    )(tile_expert, tile_valid, xs, wg, bg.reshape(ne, 1, dff), wu, bu.reshape(ne, 1, dff),
      wd, bd.reshape(ne, 1, dm))


def _combine_kernel(seg_s, chunks_s, dst_s, ys_ref, idx_ref, rank_ref, w_ref, seg_ref, x1_ref,
                    g_ref, o_ref, buf_ref, sem, *, tile0, ne):
    i = pl.program_id(0)
    t = tile0 + i
    tm, ts = x1_ref.shape[0], buf_ref.shape[0]

    @pl.when(i == 0)
    def _():
        buf_ref[...] = jnp.zeros_like(buf_ref)

    def fetch(e, started):
        k = t * ne + e
        n_chunks = chunks_s[k]

        def start(c, carry):
            _chunk_copy(ys_ref, dst_s[k] + c * SEG, buf_ref, seg_s[k] + c * SEG, sem).start()
            return carry

        lax.fori_loop(0, n_chunks, start, 0)
        return started + n_chunks

    def wait(c, carry):
        _chunk_copy(ys_ref, 0, buf_ref, 0, sem).wait()
        return carry

    started = lax.fori_loop(0, ne, fetch, 0)

    slots = _slots_of(idx_ref[...], rank_ref[...], seg_ref[...])
    lane = lax.broadcasted_iota(I32, (tm, ts), 1)
    w = w_ref[...]
    gate = jnp.zeros((tm, ts), F32)
    for r in range(TOP_K):
        gate = jnp.where(lane == slots[r], w[:, r:r + 1], gate)
    gate = gate.astype(BF16)
    lax.fori_loop(0, started, wait, 0)
    y = jnp.dot(gate, buf_ref[...].astype(BF16), preferred_element_type=F32)
    o_ref[...] = _rms(x1_ref[...] + y, g_ref[...])


def _combine(plan, tile0, ys, idx, rank, wts, x1, gfin):
    n, dm = x1.shape
    tm = ROW_TILE
    ne = plan["seg_row"].shape[-1]
    tile = lambda i, *_: (i, 0)
    return pl.pallas_call(
        functools.partial(_combine_kernel, tile0=tile0, ne=ne),
        grid_spec=pltpu.PrefetchScalarGridSpec(
            num_scalar_prefetch=3,
            grid=(n // tm,),
            in_specs=[pl.BlockSpec(memory_space=pl.ANY),
                      pl.BlockSpec((tm, TOP_K), tile),
                      pl.BlockSpec((tm, TOP_K), tile),
                      pl.BlockSpec((tm, TOP_K), tile),
                      pl.BlockSpec((None, 1, ne), lambda i, *_: (tile0 + i, 0, 0)),
                      pl.BlockSpec((tm, dm), tile),
                      pl.BlockSpec((1, dm), lambda i, *_: (0, 0))],
            out_specs=pl.BlockSpec((tm, dm), tile),
            scratch_shapes=[pltpu.VMEM((_tile_slots(ne), dm), F32),
                            pltpu.SemaphoreType.DMA]),
        out_shape=jax.ShapeDtypeStruct((n, dm), F32),
        compiler_params=_params("arbitrary"),
        name="moe_combine",
    )(plan["seg_row_s"], plan["chunks_s"], plan["dst_row_s"], ys, idx, rank, wts,
      plan["seg_row"], x1, gfin.reshape(1, dm))


def kernel(x_prompt, x_sample, cache_a_k, cache_a_v, cache_b_k, cache_b_v, norm_attn, w_in,
           rel_bias, g_out_a, g_out_b, w_out, norm_ffn, w_router, b_router, w_gate, b_gate,
           w_up, b_up, w_down, b_down, norm_final):
    depth = w_in.shape[0]
    assert depth == 1, "the MoE combine applies the final norm: single layer only"
    bp, sp, dm = x_prompt.shape
    bs, ts, _ = x_sample.shape
    past = cache_b_k.shape[2]
    wa = cache_a_k.shape[2]
    nh_a, hd = cache_a_k.shape[3], cache_a_k.shape[4]
    nh_b = cache_b_k.shape[3]
    da, db = nh_a * hd, nh_b * hd
    ne = w_router.shape[-1]
    assert da == db and hd == HEAD_DIM and wa == BAND_WINDOW and ts == CHUNK
    slabs = da // LANES
    QA, KA, VA, QB, KB, VB = (c * slabs for c in range(6))
    l = 0
    n_p, n_s = bp * sp, bs * ts

    xp2 = x_prompt.reshape(n_p, dm)
    xs2 = x_sample.reshape(n_s, dm)
    w_in_bf = w_in[l].astype(BF16)
    wp = min(BAND_WINDOW, sp)
    assert sp % ROW_TILE == 0 and wp % ROW_TILE == 0
    qkv_p, ka_p, va_p, kb_p, vb_p = _qkv_proj(xp2, norm_attn[l], w_in_bf,
                                              seq_tiles=sp // ROW_TILE, band_tiles=wp // ROW_TILE)
    qkv_s, ka_s, va_s, kb_s, vb_s = _qkv_proj(xs2, norm_attn[l], w_in_bf)
    qkv_p3 = qkv_p.reshape(bp, sp, 6 * da)
    qkv_s3 = qkv_s.reshape(bs, ts, 6 * da)

    bias = _band_bias(rel_bias[l])
    tile_bias = _band_tile_bias(bias)

    oa_p = _band_attention_prompt(qkv_p3, QA, KA, VA, tile_bias)
    ob_p = _sb_attention(qkv_p3, QB, qkv_p3, KB, qkv_p3, VB, qkv_p3, KB, qkv_p3, VB,
                         slabs, tq=min(SB_TQ, sp), q_base=0)

    oa_s = _band_attention_sample(qkv_s3, QA, KA, VA, cache_a_k[l], cache_a_v[l], bias)
    ob_s = _sb_attention(qkv_s3, QB, qkv_s3, KB, qkv_s3, VB,
                         cache_b_k[l].reshape(bs, past, db), 0,
                         cache_b_v[l].reshape(bs, past, db), 0, slabs, tq=ts, q_base=past)

    w_out_bf = w_out[l].astype(BF16)
    merge = functools.partial(_merge, ga=g_out_a[l], gb=g_out_b[l], wo_bf=w_out_bf, gf=norm_ffn[l],
                              w_router=w_router[l], b_router=b_router[l])
    x1_p, xn_p, idx_p, wts_p, rank_p, cnt_p = merge(
        xp2, oa_p.reshape(n_p, da), ob_p.reshape(n_p, db))
    x1_s, xn_s, idx_s, wts_s, rank_s, cnt_s = merge(
        xs2, oa_s.reshape(n_s, da), ob_s.reshape(n_s, db))
    plan = _moe_plan(jnp.concatenate([cnt_p, cnt_s]).reshape(-1, ne), n_p + n_s)
    xs = _dispatch(plan, xn_p, xn_s, jnp.concatenate([idx_p, idx_s]),
                   jnp.concatenate([rank_p, rank_s]))
    ys = _expert_ffn(plan["tile_expert"], plan["tile_valid"], xs, w_gate[l], b_gate[l],
                     w_up[l], b_up[l], w_down[l], b_down[l])
    y_prompt = _combine(plan, 0, ys, idx_p, rank_p, wts_p, x1_p, norm_final).reshape(bp, sp, dm)
    y_sample = _combine(plan, n_p // ROW_TILE, ys, idx_s, rank_s, wts_s, x1_s,
                        norm_final).reshape(bs, ts, dm)

    heads_a = lambda t, b_, n_: t.reshape(b_, n_, nh_a, hd)
    heads_b = lambda t, b_, n_: t.reshape(b_, n_, nh_b, hd)
    new_a_k_prompt = heads_a(ka_p, bp, wp)[None]
    new_a_v_prompt = heads_a(va_p, bp, wp)[None]
    new_b_k_prompt = heads_b(kb_p, bp, sp)[None]
    new_b_v_prompt = heads_b(vb_p, bp, sp)[None]
    new_a_k_sample = jnp.concatenate([cache_a_k[l][:, ts:], heads_a(ka_s, bs, ts)], axis=1)[None]
    new_a_v_sample = jnp.concatenate([cache_a_v[l][:, ts:], heads_a(va_s, bs, ts)], axis=1)[None]
    new_b_k_sample = heads_b(kb_s, bs, ts)[None]
    new_b_v_sample = heads_b(vb_s, bs, ts)[None]
    return (y_prompt, y_sample, new_a_k_prompt, new_a_v_prompt, new_b_k_prompt, new_b_v_prompt,
            new_a_k_sample, new_a_v_sample, new_b_k_sample, new_b_v_sample)
```

```python
import functools

import jax
import jax.numpy as jnp
from jax import lax
from jax.experimental import pallas as pl
from jax.experimental.pallas import tpu as pltpu

F32 = jnp.float32
BF16 = jnp.bfloat16
I32 = jnp.int32

CHUNK = 64
LEFT_CHUNKS = 8
BAND_WINDOW = LEFT_CHUNKS * CHUNK
BAND = BAND_WINDOW + CHUNK
HEAD_DIM = 64
HEADS_PER_SLAB = 2
LANES = 128
REL_CLIP = 128
TOP_K = 4
SWIGLU_LIMIT = 7.0
SWIGLU_ALPHA = 1.702
EPS = 1e-6
SCALE = HEAD_DIM ** -0.5

ROW_TILE = 512
BAND_TQ = 256
BAND_KEYS = BAND_TQ + BAND_WINDOW
BAND_VARIANTS = BAND_WINDOW // BAND_TQ + 1
BAND_SLABS = 2
SB_TQ = 512
SB_TK = 256
SB_UNROLL = 1
SB_DEAD = -104.0
MOE_TILE = 512
SEG = 8
VMEM_LIMIT = 56 * 1024 * 1024


def _params(*sem):
    return pltpu.CompilerParams(dimension_semantics=sem, vmem_limit_bytes=VMEM_LIMIT)


def _rms(x, g):
    ms = jnp.mean(x * x, axis=-1, keepdims=True)
    return (x * lax.rsqrt(ms + EPS)) * g


def _head_mask(h):
    lane = lax.broadcasted_iota(I32, (1, LANES), 1)
    return (lane >= h * HEAD_DIM) & (lane < (h + 1) * HEAD_DIM)


def _qkv_kernel(x_ref, g_ref, w_ref, qkv_ref, ka_ref, va_ref, kb_ref, vb_ref):
    xb = _rms(x_ref[...], g_ref[...]).astype(BF16)
    d = ka_ref.shape[-1]
    f32_out = {1: ka_ref, 2: va_ref, 4: kb_ref, 5: vb_ref}
    for c in range(6):
        r = jnp.dot(xb, w_ref[:, c * d:(c + 1) * d], preferred_element_type=F32)
        if c in f32_out:
            f32_out[c][...] = r
        else:
            r = r * SCALE
        qkv_ref[:, c * d:(c + 1) * d] = r.astype(BF16)


def _qkv_proj(x2d, g, w_bf, seq_tiles=1, band_tiles=1):
    n, dm = x2d.shape
    d3 = w_bf.shape[1]
    d = d3 // 6
    tm = min(ROW_TILE, n)
    assert (n // tm) % seq_tiles == 0 and band_tiles <= seq_tiles
    row = lambda i: (i, 0)
    fixed = lambda i: (0, 0)
    skip = seq_tiles - band_tiles
    band_row = lambda i: ((i // seq_tiles) * band_tiles + jnp.maximum(i % seq_tiles - skip, 0), 0)
    n_band = n // seq_tiles * band_tiles
    return pl.pallas_call(
        _qkv_kernel,
        grid=(n // tm,),
        in_specs=[pl.BlockSpec((tm, dm), row),
                  pl.BlockSpec((1, dm), fixed),
                  pl.BlockSpec((dm, d3), fixed)],
        out_specs=[pl.BlockSpec((tm, d3), row)] + [pl.BlockSpec((tm, d), band_row)] * 2
        + [pl.BlockSpec((tm, d), row)] * 2,
        out_shape=[jax.ShapeDtypeStruct((n, d3), BF16)]
        + [jax.ShapeDtypeStruct((n_band, d), F32)] * 2
        + [jax.ShapeDtypeStruct((n, d), F32)] * 2,
        compiler_params=_params("arbitrary"),
        name="qkv_proj",
    )(x2d, g.reshape(1, dm), w_bf)


def _bias_kernel(tab_ref, o_ref):
    h = pl.program_id(0)
    i = lax.broadcasted_iota(I32, (CHUNK, BAND), 0)
    j = lax.broadcasted_iota(I32, (CHUNK, BAND), 1)
    idx = jnp.clip(BAND_WINDOW + i - j, -REL_CLIP, REL_CLIP) + REL_CLIP
    n_tab = 2 * REL_CLIP + 1

    def body(t, acc):
        return jnp.where(idx == t, tab_ref[h, t], acc)

    init = jnp.full((CHUNK, BAND), tab_ref[h, n_tab - 1], F32)
    o_ref[...] = lax.fori_loop(0, n_tab - 1, body, init)


def _band_bias(table):
    nh = table.shape[0]
    return pl.pallas_call(
        _bias_kernel,
        grid=(nh,),
        in_specs=[pl.BlockSpec(memory_space=pltpu.SMEM)],
        out_specs=pl.BlockSpec((None, CHUNK, BAND), lambda h: (h, 0, 0)),
        out_shape=jax.ShapeDtypeStruct((nh, CHUNK, BAND), F32),
        compiler_params=_params("arbitrary"),
        name="band_bias",
    )(table)


def _tile_bias_kernel(b_ref, o_ref):
    v = pl.program_id(1)
    o_ref[...] = jnp.full(o_ref.shape, -jnp.inf, F32)
    for c in range(BAND_TQ // CHUNK):
        o_ref[c * CHUNK:(c + 1) * CHUNK, c * CHUNK:c * CHUNK + BAND] = b_ref[...]
    col = lax.broadcasted_iota(I32, o_ref.shape, 1)
    hidden = (BAND_VARIANTS - 1 - v) * BAND_TQ
    o_ref[...] = jnp.where(col >= hidden, o_ref[...], -jnp.inf)


def _band_tile_bias(bias):
    nh = bias.shape[0]
    return pl.pallas_call(
        _tile_bias_kernel,
        grid=(nh, BAND_VARIANTS),
        in_specs=[pl.BlockSpec((None, CHUNK, BAND), lambda h, v: (h, 0, 0))],
        out_specs=pl.BlockSpec((None, None, BAND_TQ, BAND_KEYS), lambda h, v: (h, v, 0, 0)),
        out_shape=jax.ShapeDtypeStruct((nh, BAND_VARIANTS, BAND_TQ, BAND_KEYS), F32),
        compiler_params=_params("parallel", "arbitrary"),
        name="band_tile_bias",
    )(bias)


def _band_tile_kernel(q_ref, *refs):
    n_kb = BAND_KEYS // BAND_TQ
    k_refs, v_refs = refs[:n_kb], refs[n_kb:2 * n_kb]
    bias_ref, o_ref = refs[2 * n_kb], refs[2 * n_kb + 1]
    for sl in range(BAND_SLABS):
        cols = slice(sl * LANES, (sl + 1) * LANES)
        q = q_ref[:, cols]
        kcat = jnp.concatenate([r[:, cols] for r in k_refs], axis=0)
        vcat = jnp.concatenate([r[:, cols] for r in v_refs], axis=0)
        outs = []
        for h in range(HEADS_PER_SLAB):
            qm = jnp.where(_head_mask(h), q, jnp.zeros_like(q))
            s = lax.dot_general(qm, kcat, (((1,), (1,)), ((), ())),
                                preferred_element_type=F32) + bias_ref[sl * HEADS_PER_SLAB + h]
            m = jnp.max(s, axis=-1, keepdims=True)
            p = jnp.exp(s - m)
            l = jnp.sum(p, axis=-1, keepdims=True)
            pv = jnp.dot(p.astype(BF16), vcat, preferred_element_type=F32)
            outs.append(pv / l)
        o_ref[:, cols] = jnp.where(_head_mask(0), outs[0], outs[1])


def _band_attention_prompt(qkv3, q_col0, k_col0, v_col0, tile_bias):
    b, t, _ = qkv3.shape
    nslab = tile_bias.shape[0] // HEADS_PER_SLAB
    n_kb = BAND_KEYS // BAND_TQ
    assert nslab % BAND_SLABS == 0 and all(c % BAND_SLABS == 0 for c in (q_col0, k_col0, v_col0))
    blk = (None, BAND_TQ, BAND_SLABS * LANES)
    group = lambda col0, g: col0 // BAND_SLABS + g

    def kv_spec(col0, back):
        return pl.BlockSpec(blk, lambda g, bi, j: (bi, jnp.maximum(j - back, 0), group(col0, g)))

    in_specs = [pl.BlockSpec(blk, lambda g, bi, j: (bi, j, group(q_col0, g)))]
    in_specs += [kv_spec(k_col0, n_kb - 1 - i) for i in range(n_kb)]
    in_specs += [kv_spec(v_col0, n_kb - 1 - i) for i in range(n_kb)]
    in_specs += [pl.BlockSpec((BAND_SLABS * HEADS_PER_SLAB, None, BAND_TQ, BAND_KEYS),
                              lambda g, bi, j: (g, jnp.minimum(j, BAND_VARIANTS - 1), 0, 0))]
    return pl.pallas_call(
        _band_tile_kernel,
        grid=(nslab // BAND_SLABS, b, t // BAND_TQ),
        in_specs=in_specs,
        out_specs=pl.BlockSpec(blk, lambda g, bi, j: (bi, j, g)),
        out_shape=jax.ShapeDtypeStruct((b, t, nslab * LANES), F32),
        compiler_params=_params("parallel", "parallel", "arbitrary"),
        name="band_attention_prompt",
    )(qkv3, *([qkv3] * (2 * n_kb)), tile_bias)


def _band_chunk_kernel(q_ref, kp_ref, kc_ref, vp_ref, vc_ref, bias_ref, o_ref):
    def cached(ref, sl):
        heads = [ref[:, sl * HEADS_PER_SLAB + h, :] for h in range(HEADS_PER_SLAB)]
        return jnp.concatenate(heads, axis=-1).astype(BF16)

    for sl in range(o_ref.shape[-1] // LANES):
        cols = slice(sl * LANES, (sl + 1) * LANES)
        qc = q_ref[:, cols]
        kwin = jnp.concatenate([cached(kp_ref, sl), kc_ref[:, cols]], axis=0)
        vwin = jnp.concatenate([cached(vp_ref, sl), vc_ref[:, cols]], axis=0)
        outs = []
        for h in range(HEADS_PER_SLAB):
            qm = jnp.where(_head_mask(h), qc, jnp.zeros_like(qc))
            s = lax.dot_general(qm, kwin, (((1,), (1,)), ((), ())),
                                preferred_element_type=F32) + bias_ref[sl * HEADS_PER_SLAB + h]
            m = jnp.max(s, axis=-1, keepdims=True)
            p = jnp.exp(s - m)
            l = jnp.sum(p, axis=-1, keepdims=True)
            pv = jnp.dot(p.astype(BF16), vwin, preferred_element_type=F32)
            outs.append(pv / l)
        o_ref[:, cols] = jnp.where(_head_mask(0), outs[0], outs[1])


def _band_attention_sample(qkv3, q_col0, k_col0, v_col0, cache_k, cache_v, bias):
    b, t, _ = qkv3.shape
    nh = bias.shape[0]
    d = nh // HEADS_PER_SLAB * LANES
    assert all(c * LANES % d == 0 for c in (q_col0, k_col0, v_col0))
    new = lambda col0: pl.BlockSpec((None, CHUNK, d), lambda bi: (bi, 0, col0 * LANES // d))
    old = pl.BlockSpec((None, BAND_WINDOW, nh, HEAD_DIM), lambda bi: (bi, 0, 0, 0))
    return pl.pallas_call(
        _band_chunk_kernel,
        grid=(b,),
        in_specs=[new(q_col0), old, new(k_col0), old, new(v_col0),
                  pl.BlockSpec((nh, CHUNK, BAND), lambda bi: (0, 0, 0))],
        out_specs=pl.BlockSpec((None, CHUNK, d), lambda bi: (bi, 0, 0)),
        out_shape=jax.ShapeDtypeStruct((b, t, d), F32),
        compiler_params=_params("parallel"),
        name="band_attention_sample",
    )(qkv3, cache_k, qkv3, cache_v, qkv3, bias)


def _sb_kernel(q_ref, kt_ref, vt_ref, kf_ref, vf_ref, o_ref,
               qm_ref, nu_ref, d_ref, tot_ref, acc_ref, run_ref, *, q_base):
    i = pl.program_id(2)
    tq = q_ref.shape[0]
    tile_keys = kt_ref.shape[0]
    tk = SB_TK
    qpos0 = q_base + i * tq
    n_full = qpos0 // tk
    n_masked = -(-tile_keys // tk)

    q = q_ref[...]
    for h in range(HEADS_PER_SLAB):
        qm_ref[h] = jnp.where(_head_mask(h), q, jnp.zeros_like(q))
    jj = lax.broadcasted_iota(I32, (tk, tk), 0)
    ss = lax.broadcasted_iota(I32, (tk, tk), 1)
    nu_ref[...] = jnp.where(jj >= ss, -1.0, 0.0).astype(BF16)
    acc_ref[...] = jnp.zeros_like(acc_ref)
    run_ref[...] = jnp.zeros_like(run_ref)

    def tile_block(ref, m):
        rows = min(tk, tile_keys - m * tk)
        blk = ref[m * tk:m * tk + rows, :].astype(BF16)
        if rows < tk:
            blk = jnp.concatenate([blk, jnp.zeros((tk - rows, LANES), BF16)], axis=0)
        return blk

    def full_block(ref, kb):
        k0 = pl.multiple_of(kb * tk, tk)
        return ref[pl.ds(k0, tk), :].astype(BF16)

    def stage_a(kblk, tile_m):
        if tile_m is not None:
            row = lax.broadcasted_iota(I32, (tq, tk), 0)
            col = lax.broadcasted_iota(I32, (tq, tk), 1)
            vis = tile_m * tk + col < row
        for h in range(HEADS_PER_SLAB):
            z = lax.dot_general(qm_ref[h], kblk, (((1,), (1,)), ((), ())),
                                preferred_element_type=F32)
            sp = jnp.maximum(z, 0.0) + jnp.log(1.0 + jnp.exp(-jnp.abs(z)))
            if tile_m is not None:
                sp = jnp.where(vis, sp, 0.0)
            after = jnp.dot(sp.astype(BF16), nu_ref[...], preferred_element_type=F32)
            d = z + after
            if tile_m is not None:
                d = jnp.where(vis, d, -jnp.inf)
            d_ref[h] = d
            tot_ref[h] = after[:, 0:1]

    def stage_b(vblk):
        for h in range(HEADS_PER_SLAB):
            run = run_ref[h]
            a = jnp.exp(d_ref[h] + run)
            acc_ref[h] += jnp.dot(a.astype(BF16), vblk, preferred_element_type=F32)
            run_ref[h] = run + tot_ref[h]

    def step(kb):
        stage_b(full_block(vf_ref, kb + 1))
        stage_a(full_block(kf_ref, kb), None)

    def alive():
        r = jnp.maximum(run_ref[0], run_ref[1])
        return (jnp.max(r) >= SB_DEAD).astype(I32)

    stage_a(tile_block(kt_ref, n_masked - 1), n_masked - 1)
    for m in range(n_masked - 2, -1, -1):
        stage_b(tile_block(vt_ref, m + 1))
        stage_a(tile_block(kt_ref, m), m)

    @pl.when(n_full == 0)
    def _():
        stage_b(tile_block(vt_ref, 0))

    @pl.when(n_full > 0)
    def _():
        stage_b(tile_block(vt_ref, 0))
        stage_a(full_block(kf_ref, n_full - 1), None)

        def cond(c):
            kb, live = c
            return (kb >= SB_UNROLL - 1) & (live > 0)

        def body(c):
            kb, _ = c
            for u in range(SB_UNROLL):
                step(kb - u)
            return kb - SB_UNROLL, alive()

        kb, live = lax.while_loop(cond, body, (n_full - 2, alive()))
        for _ in range(SB_UNROLL - 1):
            go = (kb >= 0) & (live > 0)

            @pl.when(go)
            def _():
                step(kb)

            kb = jnp.where(go, kb - 1, kb)
            live = alive()

        @pl.when(live > 0)
        def _():
            stage_b(full_block(vf_ref, kb + 1))

    o_ref[...] = jnp.where(_head_mask(0), acc_ref[0], acc_ref[1])


def _sb_attention(q_arr, q_col0, kt_arr, kt_col0, vt_arr, vt_col0,
                  kf_arr, kf_col0, vf_arr, vf_col0, nslab, tq, q_base):
    b, t, _ = q_arr.shape
    tf = kf_arr.shape[1]
    assert q_base % SB_TK == 0 and t % tq == 0 and tf % SB_TK == 0
    assert tq % SB_TK == 0 or t == tq
    assert q_base + t - tq <= tf
    tile = lambda col0: pl.BlockSpec((None, tq, LANES), lambda bi, hp, i: (bi, i, col0 + hp))
    full = lambda col0: pl.BlockSpec((None, tf, LANES), lambda bi, hp, i: (bi, 0, col0 + hp))
    return pl.pallas_call(
        functools.partial(_sb_kernel, q_base=q_base),
        grid=(b, nslab, t // tq),
        in_specs=[tile(q_col0), tile(kt_col0), tile(vt_col0), full(kf_col0), full(vf_col0)],
        out_specs=pl.BlockSpec((None, tq, LANES), lambda bi, hp, i: (bi, i, hp)),
        out_shape=jax.ShapeDtypeStruct((b, t, nslab * LANES), F32),
        scratch_shapes=[pltpu.VMEM((HEADS_PER_SLAB, tq, LANES), BF16),
                        pltpu.VMEM((SB_TK, SB_TK), BF16),
                        pltpu.VMEM((HEADS_PER_SLAB, tq, SB_TK), F32),
                        pltpu.VMEM((HEADS_PER_SLAB, tq, 1), F32),
                        pltpu.VMEM((HEADS_PER_SLAB, tq, LANES), F32),
                        pltpu.VMEM((HEADS_PER_SLAB, tq, 1), F32)],
        compiler_params=_params("parallel", "parallel", "arbitrary"),
        name="stick_breaking",
    )(q_arr, kt_arr, vt_arr, kf_arr, vf_arr)


def _split_bf16(x):
    hi = x.astype(BF16)
    return hi, (x - hi.astype(F32)).astype(BF16)


def _merge_kernel(xp_ref, xs_ref, oap_ref, oas_ref, obp_ref, obs_ref, ga_ref, gb_ref, wo_ref,
                  gf_ref, wr_ref, br_ref, x1_ref, xt_ref, idx_ref, w_ref, rank_ref, cnt_ref,
                  *, tiles_p):
    prompt = pl.program_id(0) < tiles_p
    pick = lambda p_ref, s_ref: jnp.where(prompt, p_ref[...], s_ref[...])
    o = jnp.concatenate([_rms(pick(oap_ref, oas_ref), ga_ref[...]),
                         _rms(pick(obp_ref, obs_ref), gb_ref[...])], axis=-1).astype(BF16)
    x1 = pick(xp_ref, xs_ref) + jnp.dot(o, wo_ref[...], preferred_element_type=F32)
    x1_ref[...] = x1
    xn = _rms(x1, gf_ref[...])
    xh, xl = _split_bf16(xn)
    wh, wl = _split_bf16(wr_ref[...])
    logits = (jnp.dot(xh, wh, preferred_element_type=F32)
              + jnp.dot(xl, wh, preferred_element_type=F32)
              + jnp.dot(xh, wl, preferred_element_type=F32)) + br_ref[...]
    tm, ne = logits.shape
    lane = lax.broadcasted_iota(I32, (tm, ne), 1)
    slot = lax.broadcasted_iota(I32, (tm, TOP_K), 1)
    work = logits
    top = None
    denom = None
    sels = []
    idx = jnp.zeros((tm, TOP_K), I32)
    wts = jnp.zeros((tm, TOP_K), F32)
    for r in range(TOP_K):
        m = jnp.max(work, axis=-1, keepdims=True)
        first = jnp.min(jnp.where(work == m, lane, ne), axis=-1, keepdims=True)
        sel = lane == first
        if top is None:
            top = m
        e = jnp.exp(m - top)
        denom = e if denom is None else denom + e
        idx = jnp.where(slot == r, first, idx)
        wts = jnp.where(slot == r, e, wts)
        sels.append(sel)
        work = jnp.where(sel, -jnp.inf, work)
    idx_ref[...] = idx
    w_ref[...] = wts / denom
    chosen = sels[0] | sels[1] | sels[2] | sels[3]
    onehot = jnp.where(chosen, 1.0, 0.0)
    rr = lax.broadcasted_iota(I32, (tm, tm), 0)
    cc = lax.broadcasted_iota(I32, (tm, tm), 1)
    lower = jnp.where(cc < rr, 1.0, 0.0).astype(BF16)
    before = jnp.dot(lower, onehot.astype(BF16), preferred_element_type=F32)
    rank = jnp.zeros((tm, TOP_K), I32)
    for r in range(TOP_K):
        rk = jnp.sum(jnp.where(sels[r], before, 0.0), axis=-1, keepdims=True)
        rank = jnp.where(slot == r, rk.astype(I32), rank)
    rank_ref[...] = rank
    cnt = jnp.sum(onehot, axis=0, keepdims=True).astype(I32)
    cnt_ref[...] = cnt
    chunks = lax.shift_right_logical(cnt + (SEG - 1), SEG.bit_length() - 1)
    ee = lax.broadcasted_iota(I32, (ne, ne), 0)
    ff = lax.broadcasted_iota(I32, (ne, ne), 1)
    earlier = jnp.where(ee < ff, 1.0, 0.0).astype(BF16)
    chunks8 = jnp.broadcast_to(chunks.astype(F32), (SEG, ne)).astype(BF16)
    seg_row = (jnp.dot(chunks8, earlier, preferred_element_type=F32)[0:1] * SEG).astype(I32)
    slots = _slots_of(idx, rank, seg_row)
    ts = xt_ref.shape[0]
    lane_s = lax.broadcasted_iota(I32, (tm, ts), 1)
    hit = ((lane_s == slots[0]) | (lane_s == slots[1]) | (lane_s == slots[2])
           | (lane_s == slots[3]))
    place = jnp.where(hit, 1.0, 0.0).astype(BF16)
    xt_ref[...] = lax.dot_general(place, xn.astype(BF16), (((0,), (0,)), ((), ())),
                                  preferred_element_type=F32)


def _merge(xp, xs, oa_p, oa_s, ob_p, ob_s, ga, gb, wo_bf, gf, w_router, b_router):
    (n_p, dm), n_s = xp.shape, xs.shape[0]
    n = n_p + n_s
    da = oa_p.shape[1]
    ne = w_router.shape[1]
    tm = ROW_TILE
    assert n_p % tm == 0 and n_s % tm == 0
    tiles_p = n_p // tm
    row = lambda i: (i, 0)
    fixed = lambda i: (0, 0)
    from_p = lambda i: (jnp.minimum(i, tiles_p - 1), 0)
    from_s = lambda i: (jnp.maximum(i - tiles_p, 0), 0)
    return pl.pallas_call(
        functools.partial(_merge_kernel, tiles_p=tiles_p),
        grid=(n // tm,),
        in_specs=[pl.BlockSpec((tm, dm), from_p),
                  pl.BlockSpec((tm, dm), from_s),
                  pl.BlockSpec((tm, da), from_p),
                  pl.BlockSpec((tm, da), from_s),
                  pl.BlockSpec((tm, da), from_p),
                  pl.BlockSpec((tm, da), from_s),
                  pl.BlockSpec((1, da), fixed),
                  pl.BlockSpec((1, da), fixed),
                  pl.BlockSpec((2 * da, dm), fixed),
                  pl.BlockSpec((1, dm), fixed),
                  pl.BlockSpec((dm, ne), fixed),
                  pl.BlockSpec((1, ne), fixed)],
        out_specs=[pl.BlockSpec((tm, dm), row),
                   pl.BlockSpec((_tile_slots(ne), dm), row),
                   pl.BlockSpec((tm, TOP_K), row),
                   pl.BlockSpec((tm, TOP_K), row),
                   pl.BlockSpec((tm, TOP_K), row),
                   pl.BlockSpec((None, 1, ne), lambda i: (i, 0, 0))],
        out_shape=[jax.ShapeDtypeStruct((n, dm), F32),
                   jax.ShapeDtypeStruct((n // tm * _tile_slots(ne), dm), F32),
                   jax.ShapeDtypeStruct((n, TOP_K), I32),
                   jax.ShapeDtypeStruct((n, TOP_K), F32),
                   jax.ShapeDtypeStruct((n, TOP_K), I32),
                   jax.ShapeDtypeStruct((n // tm, 1, ne), I32)],
        compiler_params=_params("parallel"),
        name="merge_router",
    )(xp, xs, oa_p, oa_s, ob_p, ob_s, ga.reshape(1, da), gb.reshape(1, da), wo_bf,
      gf.reshape(1, dm), w_router, b_router.reshape(1, ne))


def _tile_slots(ne):
    return ROW_TILE * TOP_K + ne * SEG


def _slots_of(idx, rank, seg_row):
    tm, ne = idx.shape[0], seg_row.shape[-1]
    lane = lax.broadcasted_iota(I32, (tm, ne), 1)
    return [jnp.sum(jnp.where(lane == idx[:, r:r + 1], seg_row, 0), axis=-1, keepdims=True)
            + rank[:, r:r + 1] for r in range(TOP_K)]


def _chunk_copy(src_ref, src_row, dst_ref, dst_row, sem):
    aligned = lambda row: row if isinstance(row, int) else pl.multiple_of(row, SEG)
    return pltpu.make_async_copy(src_ref.at[pl.ds(aligned(src_row), SEG), :],
                                 dst_ref.at[pl.ds(aligned(dst_row), SEG), :], sem)


def _moe_plan(cnt, n_rows):
    nt, ne = cnt.shape
    tm = MOE_TILE
    seg_len = (cnt + SEG - 1) // SEG * SEG
    seg_row = jnp.cumsum(seg_len, axis=1) - seg_len
    rows = jnp.sum(seg_len, axis=0)
    region = (rows + tm - 1) // tm * tm
    base = jnp.cumsum(region) - region
    dst_row = base[None, :] + jnp.cumsum(seg_len, axis=0) - seg_len
    n_tiles = (n_rows * TOP_K + ne * ((SEG - 1) * nt + tm - 1)) // tm
    tile_end = jnp.cumsum(region // tm)
    tiles = jnp.arange(n_tiles, dtype=I32)
    tile_expert = jnp.minimum(jnp.sum((tiles[:, None] >= tile_end[None, :]).astype(I32), axis=1),
                              ne - 1)
    ts = _tile_slots(ne)
    chunks = seg_len // SEG
    chunk_end = jnp.cumsum(chunks, axis=0).T.reshape(-1)
    g = jnp.arange(n_tiles * tm // SEG, dtype=I32)
    e_g = tile_expert[g // (tm // SEG)]
    local = g - (base // SEG)[e_g]
    t_g = jnp.sum((chunk_end.reshape(ne, nt)[e_g] <= local[:, None]).astype(I32), axis=1)
    t_c = jnp.minimum(t_g, nt - 1)
    at = e_g * nt + t_c
    first = (chunk_end - chunks.T.reshape(-1))[at]
    src = t_c * ts + seg_row.T.reshape(-1)[at] + (local - first) * SEG
    src = jnp.where((t_g < nt) & (local >= 0), src, ts - SEG)
    flat = lambda a: a.reshape(-1).astype(I32)
    return dict(seg_row=seg_row.astype(I32).reshape(nt, 1, ne), seg_row_s=flat(seg_row),
                chunks_s=flat(chunks), dst_row_s=flat(dst_row), src_row_s=flat(src),
                tile_expert=tile_expert.astype(I32),
                tile_valid=(tiles < tile_end[ne - 1]).astype(I32), n_slots=n_tiles * tm)


def _ffn_kernel(te_ref, tv_ref, src_ref, xt_ref, wg_ref, bg_ref, wu_ref, bu_ref,
                wd_ref, bd_ref, y_ref, xbuf_ref, wgb_ref, wub_ref, wdb_ref, sem):
    t = pl.program_id(0)
    tm = y_ref.shape[0]
    n_chunks = tm // SEG
    cur = t % 2
    new_expert = (t == 0) | (te_ref[t] != te_ref[jnp.maximum(t - 1, 0)])

    def gather(tile, slot):
        def start(c, carry):
            _chunk_copy(xt_ref, src_ref[tile * n_chunks + c], xbuf_ref.at[slot], c * SEG,
                        sem.at[slot]).start()
            return carry

        lax.fori_loop(0, n_chunks, start, 0)

    @pl.when(t == 0)
    def _():
        gather(0, 0)

    nxt = jnp.minimum(t + 1, pl.num_programs(0) - 1)
    more = (t + 1 < pl.num_programs(0)) & (tv_ref[nxt] != 0)
    for slot in range(2):
        @pl.when(more & (cur != slot))
        def _():
            gather(t + 1, slot)

    @pl.when(new_expert)
    def _():
        wgb_ref[...] = wg_ref[...].astype(BF16)
        wub_ref[...] = wu_ref[...].astype(BF16)
        wdb_ref[...] = wd_ref[...].astype(BF16)

    @pl.when(tv_ref[t] != 0)
    def _():
        pltpu.make_async_copy(xt_ref.at[pl.ds(0, tm), :], xbuf_ref.at[cur], sem.at[cur]).wait()
        x = xbuf_ref[cur].astype(BF16)
        g = jnp.minimum(jnp.dot(x, wgb_ref[...], preferred_element_type=F32) + bg_ref[...],
                        SWIGLU_LIMIT)
        u = jnp.clip(jnp.dot(x, wub_ref[...], preferred_element_type=F32) + bu_ref[...],
                     -SWIGLU_LIMIT, SWIGLU_LIMIT)
        h = g * jax.nn.sigmoid(SWIGLU_ALPHA * g) * (u + 1.0)
        y_ref[...] = jnp.dot(h.astype(BF16), wdb_ref[...], preferred_element_type=F32) + bd_ref[...]

    @pl.when(tv_ref[t] == 0)
    def _():
        y_ref[...] = jnp.zeros_like(y_ref)


def _expert_ffn(plan, xt, wg, bg, wu, bu, wd, bd):
    n_slots, dm = plan["n_slots"], xt.shape[1]
    ne, _, dff = wg.shape
    tm = MOE_TILE
    row = lambda t, *_: (t, 0)
    exp3 = lambda t, te, *_: (te[t], 0, 0)
    return pl.pallas_call(
        _ffn_kernel,
        grid_spec=pltpu.PrefetchScalarGridSpec(
            num_scalar_prefetch=3,
            grid=(n_slots // tm,),
            in_specs=[pl.BlockSpec(memory_space=pl.ANY),
                      pl.BlockSpec((None, dm, dff), exp3),
                      pl.BlockSpec((None, 1, dff), exp3),
                      pl.BlockSpec((None, dm, dff), exp3),
                      pl.BlockSpec((None, 1, dff), exp3),
                      pl.BlockSpec((None, dff, dm), exp3),
                      pl.BlockSpec((None, 1, dm), exp3)],
            out_specs=pl.BlockSpec((tm, dm), row),
            scratch_shapes=[pltpu.VMEM((2, tm, dm), F32),
                            pltpu.VMEM((dm, dff), BF16),
                            pltpu.VMEM((dm, dff), BF16),
                            pltpu.VMEM((dff, dm), BF16),
                            pltpu.SemaphoreType.DMA((2,))]),
        out_shape=jax.ShapeDtypeStruct((n_slots, dm), F32),
        compiler_params=_params("arbitrary"),
        name="moe_ffn",
    )(plan["tile_expert"], plan["tile_valid"], plan["src_row_s"], xt,
      wg, bg.reshape(ne, 1, dff), wu, bu.reshape(ne, 1, dff), wd, bd.reshape(ne, 1, dm))


def _combine_kernel(seg_s, chunks_s, dst_s, ys_ref, idx_ref, rank_ref, w_ref, seg_ref, x1_ref,
                    g_ref, o_ref, buf_ref, sem, *, tile0, ne):
    i = pl.program_id(0)
    t = tile0 + i
    tm, ts = x1_ref.shape[0], buf_ref.shape[0]

    @pl.when(i == 0)
    def _():
        buf_ref[...] = jnp.zeros_like(buf_ref)

    def fetch(e, started):
        k = t * ne + e
        n_chunks = chunks_s[k]

        def start(c, carry):
            _chunk_copy(ys_ref, dst_s[k] + c * SEG, buf_ref, seg_s[k] + c * SEG, sem).start()
            return carry

        lax.fori_loop(0, n_chunks, start, 0)
        return started + n_chunks

    def wait(c, carry):
        _chunk_copy(ys_ref, 0, buf_ref, 0, sem).wait()
        return carry

    started = lax.fori_loop(0, ne, fetch, 0)

    slots = _slots_of(idx_ref[...], rank_ref[...], seg_ref[...])
    lane = lax.broadcasted_iota(I32, (tm, ts), 1)
    w = w_ref[...]
    gate = jnp.zeros((tm, ts), F32)
    for r in range(TOP_K):
        gate = jnp.where(lane == slots[r], w[:, r:r + 1], gate)
    gate = gate.astype(BF16)
    lax.fori_loop(0, started, wait, 0)
    y = jnp.dot(gate, buf_ref[...].astype(BF16), preferred_element_type=F32)
    o_ref[...] = _rms(x1_ref[...] + y, g_ref[...])


def _combine(plan, tile0, n, ys, idx, rank, wts, x1, gfin):
    dm = x1.shape[1]
    tm = ROW_TILE
    ne = plan["seg_row"].shape[-1]
    tile = lambda i, *_: (tile0 + i, 0)
    return pl.pallas_call(
        functools.partial(_combine_kernel, tile0=tile0, ne=ne),
        grid_spec=pltpu.PrefetchScalarGridSpec(
            num_scalar_prefetch=3,
            grid=(n // tm,),
            in_specs=[pl.BlockSpec(memory_space=pl.ANY),
                      pl.BlockSpec((tm, TOP_K), tile),
                      pl.BlockSpec((tm, TOP_K), tile),
                      pl.BlockSpec((tm, TOP_K), tile),
                      pl.BlockSpec((None, 1, ne), lambda i, *_: (tile0 + i, 0, 0)),
                      pl.BlockSpec((tm, dm), tile),
                      pl.BlockSpec((1, dm), lambda i, *_: (0, 0))],
            out_specs=pl.BlockSpec((tm, dm), lambda i, *_: (i, 0)),
            scratch_shapes=[pltpu.VMEM((_tile_slots(ne), dm), F32),
                            pltpu.SemaphoreType.DMA]),
        out_shape=jax.ShapeDtypeStruct((n, dm), F32),
        compiler_params=_params("arbitrary"),
        name="moe_combine",
    )(plan["seg_row_s"], plan["chunks_s"], plan["dst_row_s"], ys, idx, rank, wts,
      plan["seg_row"], x1, gfin.reshape(1, dm))


def kernel(x_prompt, x_sample, cache_a_k, cache_a_v, cache_b_k, cache_b_v, norm_attn, w_in,
           rel_bias, g_out_a, g_out_b, w_out, norm_ffn, w_router, b_router, w_gate, b_gate,
           w_up, b_up, w_down, b_down, norm_final):
    depth = w_in.shape[0]
    assert depth == 1, "the MoE combine applies the final norm: single layer only"
    bp, sp, dm = x_prompt.shape
    bs, ts, _ = x_sample.shape
    past = cache_b_k.shape[2]
    wa = cache_a_k.shape[2]
    nh_a, hd = cache_a_k.shape[3], cache_a_k.shape[4]
    nh_b = cache_b_k.shape[3]
    da, db = nh_a * hd, nh_b * hd
    ne = w_router.shape[-1]
    assert da == db and hd == HEAD_DIM and wa == BAND_WINDOW and ts == CHUNK
    slabs = da // LANES
    QA, KA, VA, QB, KB, VB = (c * slabs for c in range(6))
    l = 0
    n_p, n_s = bp * sp, bs * ts

    xp2 = x_prompt.reshape(n_p, dm)
    xs2 = x_sample.reshape(n_s, dm)
    w_in_bf = w_in[l].astype(BF16)
    wp = min(BAND_WINDOW, sp)
    assert sp % ROW_TILE == 0 and wp % ROW_TILE == 0
    qkv_p, ka_p, va_p, kb_p, vb_p = _qkv_proj(xp2, norm_attn[l], w_in_bf,
                                              seq_tiles=sp // ROW_TILE, band_tiles=wp // ROW_TILE)
    qkv_s, ka_s, va_s, kb_s, vb_s = _qkv_proj(xs2, norm_attn[l], w_in_bf)
    qkv_p3 = qkv_p.reshape(bp, sp, 6 * da)
    qkv_s3 = qkv_s.reshape(bs, ts, 6 * da)

    bias = _band_bias(rel_bias[l])
    tile_bias = _band_tile_bias(bias)

    oa_p = _band_attention_prompt(qkv_p3, QA, KA, VA, tile_bias)
    ob_p = _sb_attention(qkv_p3, QB, qkv_p3, KB, qkv_p3, VB, qkv_p3, KB, qkv_p3, VB,
                         slabs, tq=min(SB_TQ, sp), q_base=0)

    oa_s = _band_attention_sample(qkv_s3, QA, KA, VA, cache_a_k[l], cache_a_v[l], bias)
    ob_s = _sb_attention(qkv_s3, QB, qkv_s3, KB, qkv_s3, VB,
                         cache_b_k[l].reshape(bs, past, db), 0,
                         cache_b_v[l].reshape(bs, past, db), 0, slabs, tq=ts, q_base=past)

    w_out_bf = w_out[l].astype(BF16)
    x1, xt, idx, wts, rank, cnt = _merge(
        xp2, xs2, oa_p.reshape(n_p, da), oa_s.reshape(n_s, da), ob_p.reshape(n_p, db),
        ob_s.reshape(n_s, db), g_out_a[l], g_out_b[l], w_out_bf, norm_ffn[l], w_router[l],
        b_router[l])
    plan = _moe_plan(cnt.reshape(-1, ne), n_p + n_s)
    ys = _expert_ffn(plan, xt, w_gate[l], b_gate[l], w_up[l], b_up[l], w_down[l], b_down[l])
    combine = functools.partial(_combine, plan, ys=ys, idx=idx, rank=rank, wts=wts, x1=x1,
                                gfin=norm_final)
    y_prompt = combine(0, n_p).reshape(bp, sp, dm)
    y_sample = combine(n_p // ROW_TILE, n_s).reshape(bs, ts, dm)

    heads_a = lambda t, b_, n_: t.reshape(b_, n_, nh_a, hd)
    heads_b = lambda t, b_, n_: t.reshape(b_, n_, nh_b, hd)
    new_a_k_prompt = heads_a(ka_p, bp, wp)[None]
    new_a_v_prompt = heads_a(va_p, bp, wp)[None]
    new_b_k_prompt = heads_b(kb_p, bp, sp)[None]
    new_b_v_prompt = heads_b(vb_p, bp, sp)[None]
    new_a_k_sample = jnp.concatenate([cache_a_k[l][:, ts:], heads_a(ka_s, bs, ts)], axis=1)[None]
    new_a_v_sample = jnp.concatenate([cache_a_v[l][:, ts:], heads_a(va_s, bs, ts)], axis=1)[None]
    new_b_k_sample = heads_b(kb_s, bs, ts)[None]
    new_b_v_sample = heads_b(vb_s, bs, ts)[None]
    return (y_prompt, y_sample, new_a_k_prompt, new_a_v_prompt, new_b_k_prompt, new_b_v_prompt,
            new_a_k_sample, new_a_v_sample, new_b_k_sample, new_b_v_sample)
```

```python
import functools

import jax
import jax.numpy as jnp
from jax import lax
from jax.experimental import pallas as pl
from jax.experimental.pallas import tpu as pltpu

F32 = jnp.float32
BF16 = jnp.bfloat16
I32 = jnp.int32

CHUNK = 64
LEFT_CHUNKS = 8
BAND_WINDOW = LEFT_CHUNKS * CHUNK
BAND = BAND_WINDOW + CHUNK
HEAD_DIM = 64
HEADS_PER_SLAB = 2
LANES = 128
REL_CLIP = 128
TOP_K = 4
SWIGLU_LIMIT = 7.0
SWIGLU_ALPHA = 1.702
EPS = 1e-6
SCALE = HEAD_DIM ** -0.5

ROW_TILE = 512
BAND_TQ = 256
BAND_KEYS = BAND_TQ + BAND_WINDOW
BAND_VARIANTS = BAND_WINDOW // BAND_TQ + 1
BAND_SLABS = 2
SB_TQ = 512
SB_TK = 256
SB_UNROLL = 1
SB_DEAD = -104.0
MOE_TILE = 512
SEG = 8
VMEM_LIMIT = 56 * 1024 * 1024


def _params(*sem):
    return pltpu.CompilerParams(dimension_semantics=sem, vmem_limit_bytes=VMEM_LIMIT)


def _rms(x, g):
    ms = jnp.mean(x * x, axis=-1, keepdims=True)
    return (x * lax.rsqrt(ms + EPS)) * g


def _head_mask(h):
    lane = lax.broadcasted_iota(I32, (1, LANES), 1)
    return (lane >= h * HEAD_DIM) & (lane < (h + 1) * HEAD_DIM)


def _qkv_kernel(x_ref, g_ref, w_ref, qkv_ref, ka_ref, va_ref, kb_ref, vb_ref):
    xb = _rms(x_ref[...], g_ref[...]).astype(BF16)
    d = ka_ref.shape[-1]
    f32_out = {1: ka_ref, 2: va_ref, 4: kb_ref, 5: vb_ref}
    for c in range(6):
        r = jnp.dot(xb, w_ref[:, c * d:(c + 1) * d], preferred_element_type=F32)
        if c in f32_out:
            f32_out[c][...] = r
        else:
            r = r * SCALE
        qkv_ref[:, c * d:(c + 1) * d] = r.astype(BF16)


def _qkv_proj(x2d, g, w_bf, seq_tiles=1, band_tiles=1):
    n, dm = x2d.shape
    d3 = w_bf.shape[1]
    d = d3 // 6
    tm = min(ROW_TILE, n)
    assert (n // tm) % seq_tiles == 0 and band_tiles <= seq_tiles
    row = lambda i: (i, 0)
    fixed = lambda i: (0, 0)
    skip = seq_tiles - band_tiles
    band_row = lambda i: ((i // seq_tiles) * band_tiles + jnp.maximum(i % seq_tiles - skip, 0), 0)
    n_band = n // seq_tiles * band_tiles
    return pl.pallas_call(
        _qkv_kernel,
        grid=(n // tm,),
        in_specs=[pl.BlockSpec((tm, dm), row),
                  pl.BlockSpec((1, dm), fixed),
                  pl.BlockSpec((dm, d3), fixed)],
        out_specs=[pl.BlockSpec((tm, d3), row)] + [pl.BlockSpec((tm, d), band_row)] * 2
        + [pl.BlockSpec((tm, d), row)] * 2,
        out_shape=[jax.ShapeDtypeStruct((n, d3), BF16)]
        + [jax.ShapeDtypeStruct((n_band, d), F32)] * 2
        + [jax.ShapeDtypeStruct((n, d), F32)] * 2,
        compiler_params=_params("arbitrary"),
        name="qkv_proj",
    )(x2d, g.reshape(1, dm), w_bf)


def _bias_kernel(tab_ref, o_ref):
    h = pl.program_id(0)
    i = lax.broadcasted_iota(I32, (CHUNK, BAND), 0)
    j = lax.broadcasted_iota(I32, (CHUNK, BAND), 1)
    idx = jnp.clip(BAND_WINDOW + i - j, -REL_CLIP, REL_CLIP) + REL_CLIP
    n_tab = 2 * REL_CLIP + 1

    def body(t, acc):
        return jnp.where(idx == t, tab_ref[h, t], acc)

    init = jnp.full((CHUNK, BAND), tab_ref[h, n_tab - 1], F32)
    o_ref[...] = lax.fori_loop(0, n_tab - 1, body, init)


def _band_bias(table):
    nh = table.shape[0]
    return pl.pallas_call(
        _bias_kernel,
        grid=(nh,),
        in_specs=[pl.BlockSpec(memory_space=pltpu.SMEM)],
        out_specs=pl.BlockSpec((None, CHUNK, BAND), lambda h: (h, 0, 0)),
        out_shape=jax.ShapeDtypeStruct((nh, CHUNK, BAND), F32),
        compiler_params=_params("arbitrary"),
        name="band_bias",
    )(table)


def _tile_bias_kernel(b_ref, o_ref):
    v = pl.program_id(1)
    o_ref[...] = jnp.full(o_ref.shape, -jnp.inf, F32)
    for c in range(BAND_TQ // CHUNK):
        o_ref[c * CHUNK:(c + 1) * CHUNK, c * CHUNK:c * CHUNK + BAND] = b_ref[...]
    col = lax.broadcasted_iota(I32, o_ref.shape, 1)
    hidden = (BAND_VARIANTS - 1 - v) * BAND_TQ
    o_ref[...] = jnp.where(col >= hidden, o_ref[...], -jnp.inf)


def _band_tile_bias(bias):
    nh = bias.shape[0]
    return pl.pallas_call(
        _tile_bias_kernel,
        grid=(nh, BAND_VARIANTS),
        in_specs=[pl.BlockSpec((None, CHUNK, BAND), lambda h, v: (h, 0, 0))],
        out_specs=pl.BlockSpec((None, None, BAND_TQ, BAND_KEYS), lambda h, v: (h, v, 0, 0)),
        out_shape=jax.ShapeDtypeStruct((nh, BAND_VARIANTS, BAND_TQ, BAND_KEYS), F32),
        compiler_params=_params("parallel", "arbitrary"),
        name="band_tile_bias",
    )(bias)


def _band_tile_kernel(q_ref, *refs):
    n_kb = BAND_KEYS // BAND_TQ
    k_refs, v_refs = refs[:n_kb], refs[n_kb:2 * n_kb]
    bias_ref, o_ref = refs[2 * n_kb], refs[2 * n_kb + 1]
    for sl in range(BAND_SLABS):
        cols = slice(sl * LANES, (sl + 1) * LANES)
        q = q_ref[:, cols]
        kcat = jnp.concatenate([r[:, cols] for r in k_refs], axis=0)
        vcat = jnp.concatenate([r[:, cols] for r in v_refs], axis=0)
        outs = []
        for h in range(HEADS_PER_SLAB):
            qm = jnp.where(_head_mask(h), q, jnp.zeros_like(q))
            s = lax.dot_general(qm, kcat, (((1,), (1,)), ((), ())),
                                preferred_element_type=F32) + bias_ref[sl * HEADS_PER_SLAB + h]
            m = jnp.max(s, axis=-1, keepdims=True)
            p = jnp.exp(s - m)
            l = jnp.sum(p, axis=-1, keepdims=True)
            pv = jnp.dot(p.astype(BF16), vcat, preferred_element_type=F32)
            outs.append(pv / l)
        o_ref[:, cols] = jnp.where(_head_mask(0), outs[0], outs[1])


def _band_attention_prompt(qkv3, q_col0, k_col0, v_col0, tile_bias):
    b, t, _ = qkv3.shape
    nslab = tile_bias.shape[0] // HEADS_PER_SLAB
    n_kb = BAND_KEYS // BAND_TQ
    assert nslab % BAND_SLABS == 0 and all(c % BAND_SLABS == 0 for c in (q_col0, k_col0, v_col0))
    blk = (None, BAND_TQ, BAND_SLABS * LANES)
    group = lambda col0, g: col0 // BAND_SLABS + g

    def kv_spec(col0, back):
        return pl.BlockSpec(blk, lambda g, bi, j: (bi, jnp.maximum(j - back, 0), group(col0, g)))

    in_specs = [pl.BlockSpec(blk, lambda g, bi, j: (bi, j, group(q_col0, g)))]
    in_specs += [kv_spec(k_col0, n_kb - 1 - i) for i in range(n_kb)]
    in_specs += [kv_spec(v_col0, n_kb - 1 - i) for i in range(n_kb)]
    in_specs += [pl.BlockSpec((BAND_SLABS * HEADS_PER_SLAB, None, BAND_TQ, BAND_KEYS),
                              lambda g, bi, j: (g, jnp.minimum(j, BAND_VARIANTS - 1), 0, 0))]
    return pl.pallas_call(
        _band_tile_kernel,
        grid=(nslab // BAND_SLABS, b, t // BAND_TQ),
        in_specs=in_specs,
        out_specs=pl.BlockSpec(blk, lambda g, bi, j: (bi, j, g)),
        out_shape=jax.ShapeDtypeStruct((b, t, nslab * LANES), F32),
        compiler_params=_params("parallel", "parallel", "arbitrary"),
        name="band_attention_prompt",
    )(qkv3, *([qkv3] * (2 * n_kb)), tile_bias)


def _band_chunk_kernel(q_ref, kp_ref, kc_ref, vp_ref, vc_ref, bias_ref, o_ref):
    def cached(ref, sl):
        heads = [ref[:, sl * HEADS_PER_SLAB + h, :] for h in range(HEADS_PER_SLAB)]
        return jnp.concatenate(heads, axis=-1).astype(BF16)

    for sl in range(o_ref.shape[-1] // LANES):
        cols = slice(sl * LANES, (sl + 1) * LANES)
        qc = q_ref[:, cols]
        kwin = jnp.concatenate([cached(kp_ref, sl), kc_ref[:, cols]], axis=0)
        vwin = jnp.concatenate([cached(vp_ref, sl), vc_ref[:, cols]], axis=0)
        outs = []
        for h in range(HEADS_PER_SLAB):
            qm = jnp.where(_head_mask(h), qc, jnp.zeros_like(qc))
            s = lax.dot_general(qm, kwin, (((1,), (1,)), ((), ())),
                                preferred_element_type=F32) + bias_ref[sl * HEADS_PER_SLAB + h]
            m = jnp.max(s, axis=-1, keepdims=True)
            p = jnp.exp(s - m)
            l = jnp.sum(p, axis=-1, keepdims=True)
            pv = jnp.dot(p.astype(BF16), vwin, preferred_element_type=F32)
            outs.append(pv / l)
        o_ref[:, cols] = jnp.where(_head_mask(0), outs[0], outs[1])


def _band_attention_sample(qkv3, q_col0, k_col0, v_col0, cache_k, cache_v, bias):
    b, t, _ = qkv3.shape
    nh = bias.shape[0]
    d = nh // HEADS_PER_SLAB * LANES
    assert all(c * LANES % d == 0 for c in (q_col0, k_col0, v_col0))
    new = lambda col0: pl.BlockSpec((None, CHUNK, d), lambda bi: (bi, 0, col0 * LANES // d))
    old = pl.BlockSpec((None, BAND_WINDOW, nh, HEAD_DIM), lambda bi: (bi, 0, 0, 0))
    return pl.pallas_call(
        _band_chunk_kernel,
        grid=(b,),
        in_specs=[new(q_col0), old, new(k_col0), old, new(v_col0),
                  pl.BlockSpec((nh, CHUNK, BAND), lambda bi: (0, 0, 0))],
        out_specs=pl.BlockSpec((None, CHUNK, d), lambda bi: (bi, 0, 0)),
        out_shape=jax.ShapeDtypeStruct((b, t, d), F32),
        compiler_params=_params("parallel"),
        name="band_attention_sample",
    )(qkv3, cache_k, qkv3, cache_v, qkv3, bias)


def _sb_kernel(q_ref, kt_ref, vt_ref, kf_ref, vf_ref, o_ref,
               qm_ref, nu_ref, d_ref, tot_ref, acc_ref, run_ref, *, q_base):
    i = pl.program_id(2)
    tq = q_ref.shape[0]
    tile_keys = kt_ref.shape[0]
    tk = SB_TK
    qpos0 = q_base + i * tq
    n_full = qpos0 // tk
    n_masked = -(-tile_keys // tk)

    q = q_ref[...]
    for h in range(HEADS_PER_SLAB):
        qm_ref[h] = jnp.where(_head_mask(h), q, jnp.zeros_like(q))
    jj = lax.broadcasted_iota(I32, (tk, tk), 0)
    ss = lax.broadcasted_iota(I32, (tk, tk), 1)
    nu_ref[...] = jnp.where(jj >= ss, -1.0, 0.0).astype(BF16)
    acc_ref[...] = jnp.zeros_like(acc_ref)
    run_ref[...] = jnp.zeros_like(run_ref)

    def tile_block(ref, m):
        rows = min(tk, tile_keys - m * tk)
        blk = ref[m * tk:m * tk + rows, :].astype(BF16)
        if rows < tk:
            blk = jnp.concatenate([blk, jnp.zeros((tk - rows, LANES), BF16)], axis=0)
        return blk

    def full_block(ref, kb):
        k0 = pl.multiple_of(kb * tk, tk)
        return ref[pl.ds(k0, tk), :].astype(BF16)

    def stage_a(kblk, tile_m):
        if tile_m is not None:
            row = lax.broadcasted_iota(I32, (tq, tk), 0)
            col = lax.broadcasted_iota(I32, (tq, tk), 1)
            vis = tile_m * tk + col < row
        for h in range(HEADS_PER_SLAB):
            z = lax.dot_general(qm_ref[h], kblk, (((1,), (1,)), ((), ())),
                                preferred_element_type=F32)
            sp = jnp.maximum(z, 0.0) + jnp.log(1.0 + jnp.exp(-jnp.abs(z)))
            if tile_m is not None:
                sp = jnp.where(vis, sp, 0.0)
            after = jnp.dot(sp.astype(BF16), nu_ref[...], preferred_element_type=F32)
            d = z + after
            if tile_m is not None:
                d = jnp.where(vis, d, -jnp.inf)
            d_ref[h] = d
            tot_ref[h] = after[:, 0:1]

    def stage_b(vblk):
        for h in range(HEADS_PER_SLAB):
            run = run_ref[h]
            a = jnp.exp(d_ref[h] + run)
            acc_ref[h] += jnp.dot(a.astype(BF16), vblk, preferred_element_type=F32)
            run_ref[h] = run + tot_ref[h]

    def step(kb):
        stage_b(full_block(vf_ref, kb + 1))
        stage_a(full_block(kf_ref, kb), None)

    def alive():
        r = jnp.maximum(run_ref[0], run_ref[1])
        return (jnp.max(r) >= SB_DEAD).astype(I32)

    stage_a(tile_block(kt_ref, n_masked - 1), n_masked - 1)
    for m in range(n_masked - 2, -1, -1):
        stage_b(tile_block(vt_ref, m + 1))
        stage_a(tile_block(kt_ref, m), m)

    @pl.when(n_full == 0)
    def _():
        stage_b(tile_block(vt_ref, 0))

    @pl.when(n_full > 0)
    def _():
        stage_b(tile_block(vt_ref, 0))
        stage_a(full_block(kf_ref, n_full - 1), None)

        def cond(c):
            kb, live = c
            return (kb >= SB_UNROLL - 1) & (live > 0)

        def body(c):
            kb, _ = c
            for u in range(SB_UNROLL):
                step(kb - u)
            return kb - SB_UNROLL, alive()

        kb, live = lax.while_loop(cond, body, (n_full - 2, alive()))
        for _ in range(SB_UNROLL - 1):
            go = (kb >= 0) & (live > 0)

            @pl.when(go)
            def _():
                step(kb)

            kb = jnp.where(go, kb - 1, kb)
            live = alive()

        @pl.when(live > 0)
        def _():
            stage_b(full_block(vf_ref, kb + 1))

    o_ref[...] = jnp.where(_head_mask(0), acc_ref[0], acc_ref[1])


def _sb_attention(q_arr, q_col0, kt_arr, kt_col0, vt_arr, vt_col0,
                  kf_arr, kf_col0, vf_arr, vf_col0, nslab, tq, q_base):
    b, t, _ = q_arr.shape
    tf = kf_arr.shape[1]
    assert q_base % SB_TK == 0 and t % tq == 0 and tf % SB_TK == 0
    assert tq % SB_TK == 0 or t == tq
    assert q_base + t - tq <= tf
    tile = lambda col0: pl.BlockSpec((None, tq, LANES), lambda bi, hp, i: (bi, i, col0 + hp))
    full = lambda col0: pl.BlockSpec((None, tf, LANES), lambda bi, hp, i: (bi, 0, col0 + hp))
    return pl.pallas_call(
        functools.partial(_sb_kernel, q_base=q_base),
        grid=(b, nslab, t // tq),
        in_specs=[tile(q_col0), tile(kt_col0), tile(vt_col0), full(kf_col0), full(vf_col0)],
        out_specs=pl.BlockSpec((None, tq, LANES), lambda bi, hp, i: (bi, i, hp)),
        out_shape=jax.ShapeDtypeStruct((b, t, nslab * LANES), F32),
        scratch_shapes=[pltpu.VMEM((HEADS_PER_SLAB, tq, LANES), BF16),
                        pltpu.VMEM((SB_TK, SB_TK), BF16),
                        pltpu.VMEM((HEADS_PER_SLAB, tq, SB_TK), F32),
                        pltpu.VMEM((HEADS_PER_SLAB, tq, 1), F32),
                        pltpu.VMEM((HEADS_PER_SLAB, tq, LANES), F32),
                        pltpu.VMEM((HEADS_PER_SLAB, tq, 1), F32)],
        compiler_params=_params("parallel", "parallel", "arbitrary"),
        name="stick_breaking",
    )(q_arr, kt_arr, vt_arr, kf_arr, vf_arr)


def _split_bf16(x):
    hi = x.astype(BF16)
    return hi, (x - hi.astype(F32)).astype(BF16)


def _merge_kernel(xp_ref, xs_ref, oap_ref, oas_ref, obp_ref, obs_ref, ga_ref, gb_ref, wo_ref,
                  gf_ref, wr_ref, br_ref, x1_ref, xt_ref, idx_ref, w_ref, rank_ref, cnt_ref,
                  *, tiles_p):
    prompt = pl.program_id(0) < tiles_p
    pick = lambda p_ref, s_ref: jnp.where(prompt, p_ref[...], s_ref[...])
    o = jnp.concatenate([_rms(pick(oap_ref, oas_ref), ga_ref[...]),
                         _rms(pick(obp_ref, obs_ref), gb_ref[...])], axis=-1).astype(BF16)
    x1 = pick(xp_ref, xs_ref) + jnp.dot(o, wo_ref[...], preferred_element_type=F32)
    x1_ref[...] = x1
    xn = _rms(x1, gf_ref[...])
    xh, xl = _split_bf16(xn)
    wh, wl = _split_bf16(wr_ref[...])
    logits = (jnp.dot(xh, wh, preferred_element_type=F32)
              + jnp.dot(xl, wh, preferred_element_type=F32)
              + jnp.dot(xh, wl, preferred_element_type=F32)) + br_ref[...]
    tm, ne = logits.shape
    lane = lax.broadcasted_iota(I32, (tm, ne), 1)
    slot = lax.broadcasted_iota(I32, (tm, TOP_K), 1)
    work = logits
    top = None
    denom = None
    sels = []
    idx = jnp.zeros((tm, TOP_K), I32)
    wts = jnp.zeros((tm, TOP_K), F32)
    for r in range(TOP_K):
        m = jnp.max(work, axis=-1, keepdims=True)
        first = jnp.min(jnp.where(work == m, lane, ne), axis=-1, keepdims=True)
        sel = lane == first
        if top is None:
            top = m
        e = jnp.exp(m - top)
        denom = e if denom is None else denom + e
        idx = jnp.where(slot == r, first, idx)
        wts = jnp.where(slot == r, e, wts)
        sels.append(sel)
        work = jnp.where(sel, -jnp.inf, work)
    idx_ref[...] = idx
    w_ref[...] = wts / denom
    chosen = sels[0] | sels[1] | sels[2] | sels[3]
    onehot = jnp.where(chosen, 1.0, 0.0)
    rr = lax.broadcasted_iota(I32, (tm, tm), 0)
    cc = lax.broadcasted_iota(I32, (tm, tm), 1)
    lower = jnp.where(cc < rr, 1.0, 0.0).astype(BF16)
    before = jnp.dot(lower, onehot.astype(BF16), preferred_element_type=F32)
    rank = jnp.zeros((tm, TOP_K), I32)
    for r in range(TOP_K):
        rk = jnp.sum(jnp.where(sels[r], before, 0.0), axis=-1, keepdims=True)
        rank = jnp.where(slot == r, rk.astype(I32), rank)
    rank_ref[...] = rank
    cnt = jnp.sum(onehot, axis=0, keepdims=True).astype(I32)
    cnt_ref[...] = cnt
    chunks = lax.shift_right_logical(cnt + (SEG - 1), SEG.bit_length() - 1)
    ee = lax.broadcasted_iota(I32, (ne, ne), 0)
    ff = lax.broadcasted_iota(I32, (ne, ne), 1)
    earlier = jnp.where(ee < ff, 1.0, 0.0).astype(BF16)
    chunks8 = jnp.broadcast_to(chunks.astype(F32), (SEG, ne)).astype(BF16)
    seg_row = (jnp.dot(chunks8, earlier, preferred_element_type=F32)[0:1] * SEG).astype(I32)
    slots = _slots_of(idx, rank, seg_row)
    ts = xt_ref.shape[0]
    lane_s = lax.broadcasted_iota(I32, (tm, ts), 1)
    hit = ((lane_s == slots[0]) | (lane_s == slots[1]) | (lane_s == slots[2])
           | (lane_s == slots[3]))
    place = jnp.where(hit, 1.0, 0.0).astype(BF16)
    xt_ref[...] = lax.dot_general(place, xn.astype(BF16), (((0,), (0,)), ((), ())),
                                  preferred_element_type=F32)


def _merge(xp, xs, oa_p, oa_s, ob_p, ob_s, ga, gb, wo_bf, gf, w_router, b_router):
    (n_p, dm), n_s = xp.shape, xs.shape[0]
    n = n_p + n_s
    da = oa_p.shape[1]
    ne = w_router.shape[1]
    tm = ROW_TILE
    assert n_p % tm == 0 and n_s % tm == 0
    tiles_p = n_p // tm
    row = lambda i: (i, 0)
    fixed = lambda i: (0, 0)
    from_p = lambda i: (jnp.minimum(i, tiles_p - 1), 0)
    from_s = lambda i: (jnp.maximum(i - tiles_p, 0), 0)
    return pl.pallas_call(
        functools.partial(_merge_kernel, tiles_p=tiles_p),
        grid=(n // tm,),
        in_specs=[pl.BlockSpec((tm, dm), from_p),
                  pl.BlockSpec((tm, dm), from_s),
                  pl.BlockSpec((tm, da), from_p),
                  pl.BlockSpec((tm, da), from_s),
                  pl.BlockSpec((tm, da), from_p),
                  pl.BlockSpec((tm, da), from_s),
                  pl.BlockSpec((1, da), fixed),
                  pl.BlockSpec((1, da), fixed),
                  pl.BlockSpec((2 * da, dm), fixed),
                  pl.BlockSpec((1, dm), fixed),
                  pl.BlockSpec((dm, ne), fixed),
                  pl.BlockSpec((1, ne), fixed)],
        out_specs=[pl.BlockSpec((tm, dm), row),
                   pl.BlockSpec((_tile_slots(ne), dm), row),
                   pl.BlockSpec((tm, TOP_K), row),
                   pl.BlockSpec((tm, TOP_K), row),
                   pl.BlockSpec((tm, TOP_K), row),
                   pl.BlockSpec((None, 1, ne), lambda i: (i, 0, 0))],
        out_shape=[jax.ShapeDtypeStruct((n, dm), F32),
                   jax.ShapeDtypeStruct((n // tm * _tile_slots(ne), dm), F32),
                   jax.ShapeDtypeStruct((n, TOP_K), I32),
                   jax.ShapeDtypeStruct((n, TOP_K), F32),
                   jax.ShapeDtypeStruct((n, TOP_K), I32),
                   jax.ShapeDtypeStruct((n // tm, 1, ne), I32)],
        compiler_params=_params("parallel"),
        name="merge_router",
    )(xp, xs, oa_p, oa_s, ob_p, ob_s, ga.reshape(1, da), gb.reshape(1, da), wo_bf,
      gf.reshape(1, dm), w_router, b_router.reshape(1, ne))


def _tile_slots(ne):
    return ROW_TILE * TOP_K + ne * SEG


def _slots_of(idx, rank, seg_row):
    tm, ne = idx.shape[0], seg_row.shape[-1]
    lane = lax.broadcasted_iota(I32, (tm, ne), 1)
    return [jnp.sum(jnp.where(lane == idx[:, r:r + 1], seg_row, 0), axis=-1, keepdims=True)
            + rank[:, r:r + 1] for r in range(TOP_K)]


def _chunk_copy(src_ref, src_row, dst_ref, dst_row, sem):
    aligned = lambda row: row if isinstance(row, int) else pl.multiple_of(row, SEG)
    return pltpu.make_async_copy(src_ref.at[pl.ds(aligned(src_row), SEG), :],
                                 dst_ref.at[pl.ds(aligned(dst_row), SEG), :], sem)


def _moe_plan(cnt, n_rows):
    nt, ne = cnt.shape
    tm = MOE_TILE
    seg_len = (cnt + SEG - 1) // SEG * SEG
    seg_row = jnp.cumsum(seg_len, axis=1) - seg_len
    rows = jnp.sum(seg_len, axis=0)
    region = (rows + tm - 1) // tm * tm
    base = jnp.cumsum(region) - region
    dst_row = base[None, :] + jnp.cumsum(seg_len, axis=0) - seg_len
    n_tiles = (n_rows * TOP_K + ne * ((SEG - 1) * nt + tm - 1)) // tm
    tile_end = jnp.cumsum(region // tm)
    tiles = jnp.arange(n_tiles, dtype=I32)
    tile_expert = jnp.minimum(jnp.sum((tiles[:, None] >= tile_end[None, :]).astype(I32), axis=1),
                              ne - 1)
    flat = lambda a: a.reshape(-1).astype(I32)
    chunks_s, seg_row_s = flat(seg_len // SEG), flat(seg_row)
    src_row_s = _chunk_sources(chunks_s, seg_row_s, flat(base // SEG), nt, n_tiles * tm // SEG)
    return dict(seg_row=seg_row.astype(I32).reshape(nt, 1, ne), seg_row_s=seg_row_s,
                chunks_s=chunks_s, dst_row_s=flat(dst_row), src_row_s=src_row_s,
                tile_expert=tile_expert.astype(I32),
                tile_valid=(tiles < tile_end[ne - 1]).astype(I32), n_slots=n_tiles * tm)


def _chunk_source_kernel(chunks_ref, seg_ref, base_ref, src_ref, *, nt, ne):
    ts = _tile_slots(ne)

    def clear(g, carry):
        src_ref[g] = ts - SEG
        return carry

    lax.fori_loop(0, src_ref.shape[0], clear, 0)

    def expert(e, carry):
        def tile(t, pos):
            k = t * ne + e
            row0 = t * ts + seg_ref[k]

            def chunk(c, p):
                src_ref[p] = row0 + c * SEG
                return p + 1

            return lax.fori_loop(0, chunks_ref[k], chunk, pos)

        lax.fori_loop(0, nt, tile, base_ref[e])
        return carry

    lax.fori_loop(0, ne, expert, 0)


def _chunk_sources(chunks_s, seg_row_s, base_chunk_s, nt, n_out):
    ne = base_chunk_s.shape[0]
    smem = pl.BlockSpec(memory_space=pltpu.SMEM)
    return pl.pallas_call(
        functools.partial(_chunk_source_kernel, nt=nt, ne=ne),
        in_specs=[smem, smem, smem],
        out_specs=smem,
        out_shape=jax.ShapeDtypeStruct((n_out,), I32),
        name="moe_chunk_sources",
    )(chunks_s, seg_row_s, base_chunk_s)


def _ffn_kernel(te_ref, tv_ref, src_ref, xt_ref, wg_ref, bg_ref, wu_ref, bu_ref,
                wd_ref, bd_ref, y_ref, xbuf_ref, wgb_ref, wub_ref, wdb_ref, sem):
    t = pl.program_id(0)
    tm = y_ref.shape[0]
    n_chunks = tm // SEG
    cur = t % 2
    new_expert = (t == 0) | (te_ref[t] != te_ref[jnp.maximum(t - 1, 0)])

    def gather(tile, slot):
        def start(c, carry):
            _chunk_copy(xt_ref, src_ref[tile * n_chunks + c], xbuf_ref.at[slot], c * SEG,
                        sem.at[slot]).start()
            return carry

        lax.fori_loop(0, n_chunks, start, 0)

    @pl.when(t == 0)
    def _():
        gather(0, 0)

    nxt = jnp.minimum(t + 1, pl.num_programs(0) - 1)
    more = (t + 1 < pl.num_programs(0)) & (tv_ref[nxt] != 0)
    for slot in range(2):
        @pl.when(more & (cur != slot))
        def _():
            gather(t + 1, slot)

    @pl.when(new_expert)
    def _():
        wgb_ref[...] = wg_ref[...].astype(BF16)
        wub_ref[...] = wu_ref[...].astype(BF16)
        wdb_ref[...] = wd_ref[...].astype(BF16)

    @pl.when(tv_ref[t] != 0)
    def _():
        pltpu.make_async_copy(xt_ref.at[pl.ds(0, tm), :], xbuf_ref.at[cur], sem.at[cur]).wait()
        x = xbuf_ref[cur].astype(BF16)
        g = jnp.minimum(jnp.dot(x, wgb_ref[...], preferred_element_type=F32) + bg_ref[...],
                        SWIGLU_LIMIT)
        u = jnp.clip(jnp.dot(x, wub_ref[...], preferred_element_type=F32) + bu_ref[...],
                     -SWIGLU_LIMIT, SWIGLU_LIMIT)
        h = g * jax.nn.sigmoid(SWIGLU_ALPHA * g) * (u + 1.0)
        y_ref[...] = jnp.dot(h.astype(BF16), wdb_ref[...], preferred_element_type=F32) + bd_ref[...]

    @pl.when(tv_ref[t] == 0)
    def _():
        y_ref[...] = jnp.zeros_like(y_ref)


def _expert_ffn(plan, xt, wg, bg, wu, bu, wd, bd):
    n_slots, dm = plan["n_slots"], xt.shape[1]
    ne, _, dff = wg.shape
    tm = MOE_TILE
    row = lambda t, *_: (t, 0)
    exp3 = lambda t, te, *_: (te[t], 0, 0)
    return pl.pallas_call(
        _ffn_kernel,
        grid_spec=pltpu.PrefetchScalarGridSpec(
            num_scalar_prefetch=3,
            grid=(n_slots // tm,),
            in_specs=[pl.BlockSpec(memory_space=pl.ANY),
                      pl.BlockSpec((None, dm, dff), exp3),
                      pl.BlockSpec((None, 1, dff), exp3),
                      pl.BlockSpec((None, dm, dff), exp3),
                      pl.BlockSpec((None, 1, dff), exp3),
                      pl.BlockSpec((None, dff, dm), exp3),
                      pl.BlockSpec((None, 1, dm), exp3)],
            out_specs=pl.BlockSpec((tm, dm), row),
            scratch_shapes=[pltpu.VMEM((2, tm, dm), F32),
                            pltpu.VMEM((dm, dff), BF16),
                            pltpu.VMEM((dm, dff), BF16),
                            pltpu.VMEM((dff, dm), BF16),
                            pltpu.SemaphoreType.DMA((2,))]),
        out_shape=jax.ShapeDtypeStruct((n_slots, dm), F32),
        compiler_params=_params("arbitrary"),
        name="moe_ffn",
    )(plan["tile_expert"], plan["tile_valid"], plan["src_row_s"], xt,
      wg, bg.reshape(ne, 1, dff), wu, bu.reshape(ne, 1, dff), wd, bd.reshape(ne, 1, dm))


def _combine_kernel(seg_s, chunks_s, dst_s, ys_ref, idx_ref, rank_ref, w_ref, seg_ref, x1_ref,
                    g_ref, o_ref, buf_ref, sem, *, tile0, ne):
    i = pl.program_id(0)
    t = tile0 + i
    tm, ts = x1_ref.shape[0], buf_ref.shape[0]

    @pl.when(i == 0)
    def _():
        buf_ref[...] = jnp.zeros_like(buf_ref)

    def fetch(e, started):
        k = t * ne + e
        n_chunks = chunks_s[k]

        def start(c, carry):
            _chunk_copy(ys_ref, dst_s[k] + c * SEG, buf_ref, seg_s[k] + c * SEG, sem).start()
            return carry

        lax.fori_loop(0, n_chunks, start, 0)
        return started + n_chunks

    def wait(c, carry):
        _chunk_copy(ys_ref, 0, buf_ref, 0, sem).wait()
        return carry

    started = lax.fori_loop(0, ne, fetch, 0)

    slots = _slots_of(idx_ref[...], rank_ref[...], seg_ref[...])
    lane = lax.broadcasted_iota(I32, (tm, ts), 1)
    w = w_ref[...]
    gate = jnp.zeros((tm, ts), F32)
    for r in range(TOP_K):
        gate = jnp.where(lane == slots[r], w[:, r:r + 1], gate)
    gate = gate.astype(BF16)
    lax.fori_loop(0, started, wait, 0)
    y = jnp.dot(gate, buf_ref[...].astype(BF16), preferred_element_type=F32)
    o_ref[...] = _rms(x1_ref[...] + y, g_ref[...])


def _combine(plan, tile0, n, ys, idx, rank, wts, x1, gfin):
    dm = x1.shape[1]
    tm = ROW_TILE
    ne = plan["seg_row"].shape[-1]
    tile = lambda i, *_: (tile0 + i, 0)
    return pl.pallas_call(
        functools.partial(_combine_kernel, tile0=tile0, ne=ne),
        grid_spec=pltpu.PrefetchScalarGridSpec(
            num_scalar_prefetch=3,
            grid=(n // tm,),
            in_specs=[pl.BlockSpec(memory_space=pl.ANY),
                      pl.BlockSpec((tm, TOP_K), tile),
                      pl.BlockSpec((tm, TOP_K), tile),
                      pl.BlockSpec((tm, TOP_K), tile),
                      pl.BlockSpec((None, 1, ne), lambda i, *_: (tile0 + i, 0, 0)),
                      pl.BlockSpec((tm, dm), tile),
                      pl.BlockSpec((1, dm), lambda i, *_: (0, 0))],
            out_specs=pl.BlockSpec((tm, dm), lambda i, *_: (i, 0)),
            scratch_shapes=[pltpu.VMEM((_tile_slots(ne), dm), F32),
                            pltpu.SemaphoreType.DMA]),
        out_shape=jax.ShapeDtypeStruct((n, dm), F32),
        compiler_params=_params("arbitrary"),
        name="moe_combine",
    )(plan["seg_row_s"], plan["chunks_s"], plan["dst_row_s"], ys, idx, rank, wts,
      plan["seg_row"], x1, gfin.reshape(1, dm))


def kernel(x_prompt, x_sample, cache_a_k, cache_a_v, cache_b_k, cache_b_v, norm_attn, w_in,
           rel_bias, g_out_a, g_out_b, w_out, norm_ffn, w_router, b_router, w_gate, b_gate,
           w_up, b_up, w_down, b_down, norm_final):
    depth = w_in.shape[0]
    assert depth == 1, "the MoE combine applies the final norm: single layer only"
    bp, sp, dm = x_prompt.shape
    bs, ts, _ = x_sample.shape
    past = cache_b_k.shape[2]
    wa = cache_a_k.shape[2]
    nh_a, hd = cache_a_k.shape[3], cache_a_k.shape[4]
    nh_b = cache_b_k.shape[3]
    da, db = nh_a * hd, nh_b * hd
    ne = w_router.shape[-1]
    assert da == db and hd == HEAD_DIM and wa == BAND_WINDOW and ts == CHUNK
    slabs = da // LANES
    QA, KA, VA, QB, KB, VB = (c * slabs for c in range(6))
    l = 0
    n_p, n_s = bp * sp, bs * ts

    xp2 = x_prompt.reshape(n_p, dm)
    xs2 = x_sample.reshape(n_s, dm)
    w_in_bf = w_in[l].astype(BF16)
    wp = min(BAND_WINDOW, sp)
    assert sp % ROW_TILE == 0 and wp % ROW_TILE == 0
    qkv_p, ka_p, va_p, kb_p, vb_p = _qkv_proj(xp2, norm_attn[l], w_in_bf,
                                              seq_tiles=sp // ROW_TILE, band_tiles=wp // ROW_TILE)
    qkv_s, ka_s, va_s, kb_s, vb_s = _qkv_proj(xs2, norm_attn[l], w_in_bf)
    qkv_p3 = qkv_p.reshape(bp, sp, 6 * da)
    qkv_s3 = qkv_s.reshape(bs, ts, 6 * da)

    bias = _band_bias(rel_bias[l])
    tile_bias = _band_tile_bias(bias)

    oa_p = _band_attention_prompt(qkv_p3, QA, KA, VA, tile_bias)
    ob_p = _sb_attention(qkv_p3, QB, qkv_p3, KB, qkv_p3, VB, qkv_p3, KB, qkv_p3, VB,
                         slabs, tq=min(SB_TQ, sp), q_base=0)

    oa_s = _band_attention_sample(qkv_s3, QA, KA, VA, cache_a_k[l], cache_a_v[l], bias)
    ob_s = _sb_attention(qkv_s3, QB, qkv_s3, KB, qkv_s3, VB,
                         cache_b_k[l].reshape(bs, past, db), 0,
                         cache_b_v[l].reshape(bs, past, db), 0, slabs, tq=ts, q_base=past)

    w_out_bf = w_out[l].astype(BF16)
    x1, xt, idx, wts, rank, cnt = _merge(
        xp2, xs2, oa_p.reshape(n_p, da), oa_s.reshape(n_s, da), ob_p.reshape(n_p, db),
        ob_s.reshape(n_s, db), g_out_a[l], g_out_b[l], w_out_bf, norm_ffn[l], w_router[l],
        b_router[l])
    plan = _moe_plan(cnt.reshape(-1, ne), n_p + n_s)
    ys = _expert_ffn(plan, xt, w_gate[l], b_gate[l], w_up[l], b_up[l], w_down[l], b_down[l])
    combine = functools.partial(_combine, plan, ys=ys, idx=idx, rank=rank, wts=wts, x1=x1,
                                gfin=norm_final)
    y_prompt = combine(0, n_p).reshape(bp, sp, dm)
    y_sample = combine(n_p // ROW_TILE, n_s).reshape(bs, ts, dm)

    heads_a = lambda t, b_, n_: t.reshape(b_, n_, nh_a, hd)
    heads_b = lambda t, b_, n_: t.reshape(b_, n_, nh_b, hd)
    new_a_k_prompt = heads_a(ka_p, bp, wp)[None]
    new_a_v_prompt = heads_a(va_p, bp, wp)[None]
    new_b_k_prompt = heads_b(kb_p, bp, sp)[None]
    new_b_v_prompt = heads_b(vb_p, bp, sp)[None]
    new_a_k_sample = jnp.concatenate([cache_a_k[l][:, ts:], heads_a(ka_s, bs, ts)], axis=1)[None]
    new_a_v_sample = jnp.concatenate([cache_a_v[l][:, ts:], heads_a(va_s, bs, ts)], axis=1)[None]
    new_b_k_sample = heads_b(kb_s, bs, ts)[None]
    new_b_v_sample = heads_b(vb_s, bs, ts)[None]
    return (y_prompt, y_sample, new_a_k_prompt, new_a_v_prompt, new_b_k_prompt, new_b_v_prompt,
            new_a_k_sample, new_a_v_sample, new_b_k_sample, new_b_v_sample)
```

```python
import functools

import jax
import jax.numpy as jnp
from jax import lax
from jax.experimental import pallas as pl
from jax.experimental.pallas import tpu as pltpu

F32 = jnp.float32
BF16 = jnp.bfloat16
I32 = jnp.int32

CHUNK = 64
LEFT_CHUNKS = 8
BAND_WINDOW = LEFT_CHUNKS * CHUNK
BAND = BAND_WINDOW + CHUNK
HEAD_DIM = 64
HEADS_PER_SLAB = 2
LANES = 128
REL_CLIP = 128
TOP_K = 4
SWIGLU_LIMIT = 7.0
SWIGLU_ALPHA = 1.702
EPS = 1e-6
SCALE = HEAD_DIM ** -0.5

ROW_TILE = 512
BAND_TQ = 256
BAND_KEYS = BAND_TQ + BAND_WINDOW
BAND_VARIANTS = BAND_WINDOW // BAND_TQ + 1
BAND_SLABS = 2
SB_TQ = 512
SB_TK = 256
SB_UNROLL = 1
SB_DEAD = -104.0
MOE_TILE = 512
SEG = 8
VMEM_LIMIT = 56 * 1024 * 1024


def _params(*sem):
    return pltpu.CompilerParams(dimension_semantics=sem, vmem_limit_bytes=VMEM_LIMIT)


def _rms(x, g):
    ms = jnp.mean(x * x, axis=-1, keepdims=True)
    return (x * lax.rsqrt(ms + EPS)) * g


def _head_mask(h):
    lane = lax.broadcasted_iota(I32, (1, LANES), 1)
    return (lane >= h * HEAD_DIM) & (lane < (h + 1) * HEAD_DIM)


def _qkv_kernel(x_ref, g_ref, w_ref, qkv_ref, ka_ref, va_ref, kb_ref, vb_ref):
    xb = _rms(x_ref[...], g_ref[...]).astype(BF16)
    d = ka_ref.shape[-1]
    f32_out = {1: ka_ref, 2: va_ref, 4: kb_ref, 5: vb_ref}
    for c in range(6):
        r = jnp.dot(xb, w_ref[:, c * d:(c + 1) * d], preferred_element_type=F32)
        if c in f32_out:
            f32_out[c][...] = r
        else:
            r = r * SCALE
        qkv_ref[:, c * d:(c + 1) * d] = r.astype(BF16)


def _qkv_proj(x2d, g, w_bf, seq_tiles=1, band_tiles=1):
    n, dm = x2d.shape
    d3 = w_bf.shape[1]
    d = d3 // 6
    tm = min(ROW_TILE, n)
    assert (n // tm) % seq_tiles == 0 and band_tiles <= seq_tiles
    row = lambda i: (i, 0)
    fixed = lambda i: (0, 0)
    skip = seq_tiles - band_tiles
    band_row = lambda i: ((i // seq_tiles) * band_tiles + jnp.maximum(i % seq_tiles - skip, 0), 0)
    n_band = n // seq_tiles * band_tiles
    return pl.pallas_call(
        _qkv_kernel,
        grid=(n // tm,),
        in_specs=[pl.BlockSpec((tm, dm), row),
                  pl.BlockSpec((1, dm), fixed),
                  pl.BlockSpec((dm, d3), fixed)],
        out_specs=[pl.BlockSpec((tm, d3), row)] + [pl.BlockSpec((tm, d), band_row)] * 2
        + [pl.BlockSpec((tm, d), row)] * 2,
        out_shape=[jax.ShapeDtypeStruct((n, d3), BF16)]
        + [jax.ShapeDtypeStruct((n_band, d), F32)] * 2
        + [jax.ShapeDtypeStruct((n, d), F32)] * 2,
        compiler_params=_params("arbitrary"),
        name="qkv_proj",
    )(x2d, g.reshape(1, dm), w_bf)


def _bias_kernel(tab_ref, o_ref):
    h = pl.program_id(0)
    i = lax.broadcasted_iota(I32, (CHUNK, BAND), 0)
    j = lax.broadcasted_iota(I32, (CHUNK, BAND), 1)
    idx = jnp.clip(BAND_WINDOW + i - j, -REL_CLIP, REL_CLIP) + REL_CLIP
    n_tab = 2 * REL_CLIP + 1

    def body(t, acc):
        return jnp.where(idx == t, tab_ref[h, t], acc)

    init = jnp.full((CHUNK, BAND), tab_ref[h, n_tab - 1], F32)
    o_ref[...] = lax.fori_loop(0, n_tab - 1, body, init)


def _band_bias(table):
    nh = table.shape[0]
    return pl.pallas_call(
        _bias_kernel,
        grid=(nh,),
        in_specs=[pl.BlockSpec(memory_space=pltpu.SMEM)],
        out_specs=pl.BlockSpec((None, CHUNK, BAND), lambda h: (h, 0, 0)),
        out_shape=jax.ShapeDtypeStruct((nh, CHUNK, BAND), F32),
        compiler_params=_params("arbitrary"),
        name="band_bias",
    )(table)


def _tile_bias_kernel(b_ref, o_ref):
    v = pl.program_id(1)
    o_ref[...] = jnp.full(o_ref.shape, -jnp.inf, F32)
    for c in range(BAND_TQ // CHUNK):
        o_ref[c * CHUNK:(c + 1) * CHUNK, c * CHUNK:c * CHUNK + BAND] = b_ref[...]
    col = lax.broadcasted_iota(I32, o_ref.shape, 1)
    hidden = (BAND_VARIANTS - 1 - v) * BAND_TQ
    o_ref[...] = jnp.where(col >= hidden, o_ref[...], -jnp.inf)


def _band_tile_bias(bias):
    nh = bias.shape[0]
    return pl.pallas_call(
        _tile_bias_kernel,
        grid=(nh, BAND_VARIANTS),
        in_specs=[pl.BlockSpec((None, CHUNK, BAND), lambda h, v: (h, 0, 0))],
        out_specs=pl.BlockSpec((None, None, BAND_TQ, BAND_KEYS), lambda h, v: (h, v, 0, 0)),
        out_shape=jax.ShapeDtypeStruct((nh, BAND_VARIANTS, BAND_TQ, BAND_KEYS), F32),
        compiler_params=_params("parallel", "arbitrary"),
        name="band_tile_bias",
    )(bias)


def _band_tile_kernel(q_ref, *refs):
    n_kb = BAND_KEYS // BAND_TQ
    k_refs, v_refs = refs[:n_kb], refs[n_kb:2 * n_kb]
    bias_ref, o_ref = refs[2 * n_kb], refs[2 * n_kb + 1]
    for sl in range(BAND_SLABS):
        cols = slice(sl * LANES, (sl + 1) * LANES)
        q = q_ref[:, cols]
        kcat = jnp.concatenate([r[:, cols] for r in k_refs], axis=0)
        vcat = jnp.concatenate([r[:, cols] for r in v_refs], axis=0)
        outs = []
        for h in range(HEADS_PER_SLAB):
            qm = jnp.where(_head_mask(h), q, jnp.zeros_like(q))
            s = lax.dot_general(qm, kcat, (((1,), (1,)), ((), ())),
                                preferred_element_type=F32) + bias_ref[sl * HEADS_PER_SLAB + h]
            m = jnp.max(s, axis=-1, keepdims=True)
            p = jnp.exp(s - m)
            l = jnp.sum(p, axis=-1, keepdims=True)
            pv = jnp.dot(p.astype(BF16), vcat, preferred_element_type=F32)
            outs.append(pv / l)
        o_ref[:, cols] = jnp.where(_head_mask(0), outs[0], outs[1])


def _band_attention_prompt(qkv3, q_col0, k_col0, v_col0, tile_bias):
    b, t, _ = qkv3.shape
    nslab = tile_bias.shape[0] // HEADS_PER_SLAB
    n_kb = BAND_KEYS // BAND_TQ
    assert nslab % BAND_SLABS == 0 and all(c % BAND_SLABS == 0 for c in (q_col0, k_col0, v_col0))
    blk = (None, BAND_TQ, BAND_SLABS * LANES)
    group = lambda col0, g: col0 // BAND_SLABS + g

    def kv_spec(col0, back):
        return pl.BlockSpec(blk, lambda g, bi, j: (bi, jnp.maximum(j - back, 0), group(col0, g)))

    in_specs = [pl.BlockSpec(blk, lambda g, bi, j: (bi, j, group(q_col0, g)))]
    in_specs += [kv_spec(k_col0, n_kb - 1 - i) for i in range(n_kb)]
    in_specs += [kv_spec(v_col0, n_kb - 1 - i) for i in range(n_kb)]
    in_specs += [pl.BlockSpec((BAND_SLABS * HEADS_PER_SLAB, None, BAND_TQ, BAND_KEYS),
                              lambda g, bi, j: (g, jnp.minimum(j, BAND_VARIANTS - 1), 0, 0))]
    return pl.pallas_call(
        _band_tile_kernel,
        grid=(nslab // BAND_SLABS, b, t // BAND_TQ),
        in_specs=in_specs,
        out_specs=pl.BlockSpec(blk, lambda g, bi, j: (bi, j, g)),
        out_shape=jax.ShapeDtypeStruct((b, t, nslab * LANES), F32),
        compiler_params=_params("parallel", "parallel", "arbitrary"),
        name="band_attention_prompt",
    )(qkv3, *([qkv3] * (2 * n_kb)), tile_bias)


def _band_chunk_kernel(q_ref, kp_ref, kc_ref, vp_ref, vc_ref, bias_ref, o_ref):
    def cached(ref, sl):
        heads = [ref[:, sl * HEADS_PER_SLAB + h, :] for h in range(HEADS_PER_SLAB)]
        return jnp.concatenate(heads, axis=-1).astype(BF16)

    for sl in range(o_ref.shape[-1] // LANES):
        cols = slice(sl * LANES, (sl + 1) * LANES)
        qc = q_ref[:, cols]
        kwin = jnp.concatenate([cached(kp_ref, sl), kc_ref[:, cols]], axis=0)
        vwin = jnp.concatenate([cached(vp_ref, sl), vc_ref[:, cols]], axis=0)
        outs = []
        for h in range(HEADS_PER_SLAB):
            qm = jnp.where(_head_mask(h), qc, jnp.zeros_like(qc))
            s = lax.dot_general(qm, kwin, (((1,), (1,)), ((), ())),
                                preferred_element_type=F32) + bias_ref[sl * HEADS_PER_SLAB + h]
            m = jnp.max(s, axis=-1, keepdims=True)
            p = jnp.exp(s - m)
            l = jnp.sum(p, axis=-1, keepdims=True)
            pv = jnp.dot(p.astype(BF16), vwin, preferred_element_type=F32)
            outs.append(pv / l)
        o_ref[:, cols] = jnp.where(_head_mask(0), outs[0], outs[1])


def _band_attention_sample(qkv3, q_col0, k_col0, v_col0, cache_k, cache_v, bias):
    b, t, _ = qkv3.shape
    nh = bias.shape[0]
    d = nh // HEADS_PER_SLAB * LANES
    assert all(c * LANES % d == 0 for c in (q_col0, k_col0, v_col0))
    new = lambda col0: pl.BlockSpec((None, CHUNK, d), lambda bi: (bi, 0, col0 * LANES // d))
    old = pl.BlockSpec((None, BAND_WINDOW, nh, HEAD_DIM), lambda bi: (bi, 0, 0, 0))
    return pl.pallas_call(
        _band_chunk_kernel,
        grid=(b,),
        in_specs=[new(q_col0), old, new(k_col0), old, new(v_col0),
                  pl.BlockSpec((nh, CHUNK, BAND), lambda bi: (0, 0, 0))],
        out_specs=pl.BlockSpec((None, CHUNK, d), lambda bi: (bi, 0, 0)),
        out_shape=jax.ShapeDtypeStruct((b, t, d), F32),
        compiler_params=_params("parallel"),
        name="band_attention_sample",
    )(qkv3, cache_k, qkv3, cache_v, qkv3, bias)


def _sb_kernel(q_ref, kt_ref, vt_ref, kf_ref, vf_ref, o_ref,
               qm_ref, nu_ref, d_ref, tot_ref, acc_ref, run_ref, *, q_base):
    i = pl.program_id(2)
    tq = q_ref.shape[0]
    tile_keys = kt_ref.shape[0]
    tk = SB_TK
    qpos0 = q_base + i * tq
    n_full = qpos0 // tk
    n_masked = -(-tile_keys // tk)

    q = q_ref[...]
    for h in range(HEADS_PER_SLAB):
        qm_ref[h] = jnp.where(_head_mask(h), q, jnp.zeros_like(q))
    jj = lax.broadcasted_iota(I32, (tk, tk), 0)
    ss = lax.broadcasted_iota(I32, (tk, tk), 1)
    nu_ref[...] = jnp.where(jj >= ss, -1.0, 0.0).astype(BF16)
    acc_ref[...] = jnp.zeros_like(acc_ref)
    run_ref[...] = jnp.zeros_like(run_ref)

    def tile_block(ref, m):
        rows = min(tk, tile_keys - m * tk)
        blk = ref[m * tk:m * tk + rows, :].astype(BF16)
        if rows < tk:
            blk = jnp.concatenate([blk, jnp.zeros((tk - rows, LANES), BF16)], axis=0)
        return blk

    def full_block(ref, kb):
        k0 = pl.multiple_of(kb * tk, tk)
        return ref[pl.ds(k0, tk), :].astype(BF16)

    def stage_a(kblk, tile_m):
        if tile_m is not None:
            row = lax.broadcasted_iota(I32, (tq, tk), 0)
            col = lax.broadcasted_iota(I32, (tq, tk), 1)
            vis = tile_m * tk + col < row
        for h in range(HEADS_PER_SLAB):
            z = lax.dot_general(qm_ref[h], kblk, (((1,), (1,)), ((), ())),
                                preferred_element_type=F32)
            sp = jnp.maximum(z, 0.0) + jnp.log(1.0 + jnp.exp(-jnp.abs(z)))
            if tile_m is not None:
                sp = jnp.where(vis, sp, 0.0)
            after = jnp.dot(sp.astype(BF16), nu_ref[...], preferred_element_type=F32)
            d = z + after
            if tile_m is not None:
                d = jnp.where(vis, d, -jnp.inf)
            d_ref[h] = d
            tot_ref[h] = after[:, 0:1]

    def stage_b(vblk):
        for h in range(HEADS_PER_SLAB):
            run = run_ref[h]
            a = jnp.exp(d_ref[h] + run)
            acc_ref[h] += jnp.dot(a.astype(BF16), vblk, preferred_element_type=F32)
            run_ref[h] = run + tot_ref[h]

    def step(kb):
        stage_b(full_block(vf_ref, kb + 1))
        stage_a(full_block(kf_ref, kb), None)

    def alive():
        r = jnp.maximum(run_ref[0], run_ref[1])
        return (jnp.max(r) >= SB_DEAD).astype(I32)

    stage_a(tile_block(kt_ref, n_masked - 1), n_masked - 1)
    for m in range(n_masked - 2, -1, -1):
        stage_b(tile_block(vt_ref, m + 1))
        stage_a(tile_block(kt_ref, m), m)

    @pl.when(n_full == 0)
    def _():
        stage_b(tile_block(vt_ref, 0))

    @pl.when(n_full > 0)
    def _():
        stage_b(tile_block(vt_ref, 0))
        stage_a(full_block(kf_ref, n_full - 1), None)

        def cond(c):
            kb, live = c
            return (kb >= SB_UNROLL - 1) & (live > 0)

        def body(c):
            kb, _ = c
            for u in range(SB_UNROLL):
                step(kb - u)
            return kb - SB_UNROLL, alive()

        kb, live = lax.while_loop(cond, body, (n_full - 2, alive()))
        for _ in range(SB_UNROLL - 1):
            go = (kb >= 0) & (live > 0)

            @pl.when(go)
            def _():
                step(kb)

            kb = jnp.where(go, kb - 1, kb)
            live = alive()

        @pl.when(live > 0)
        def _():
            stage_b(full_block(vf_ref, kb + 1))

    o_ref[...] = jnp.where(_head_mask(0), acc_ref[0], acc_ref[1])


def _sb_attention(q_arr, q_col0, kt_arr, kt_col0, vt_arr, vt_col0,
                  kf_arr, kf_col0, vf_arr, vf_col0, nslab, tq, q_base):
    b, t, _ = q_arr.shape
    tf = kf_arr.shape[1]
    assert q_base % SB_TK == 0 and t % tq == 0 and tf % SB_TK == 0
    assert tq % SB_TK == 0 or t == tq
    assert q_base + t - tq <= tf
    tile = lambda col0: pl.BlockSpec((None, tq, LANES), lambda bi, hp, i: (bi, i, col0 + hp))
    full = lambda col0: pl.BlockSpec((None, tf, LANES), lambda bi, hp, i: (bi, 0, col0 + hp))
    return pl.pallas_call(
        functools.partial(_sb_kernel, q_base=q_base),
        grid=(b, nslab, t // tq),
        in_specs=[tile(q_col0), tile(kt_col0), tile(vt_col0), full(kf_col0), full(vf_col0)],
        out_specs=pl.BlockSpec((None, tq, LANES), lambda bi, hp, i: (bi, i, hp)),
        out_shape=jax.ShapeDtypeStruct((b, t, nslab * LANES), F32),
        scratch_shapes=[pltpu.VMEM((HEADS_PER_SLAB, tq, LANES), BF16),
                        pltpu.VMEM((SB_TK, SB_TK), BF16),
                        pltpu.VMEM((HEADS_PER_SLAB, tq, SB_TK), F32),
                        pltpu.VMEM((HEADS_PER_SLAB, tq, 1), F32),
                        pltpu.VMEM((HEADS_PER_SLAB, tq, LANES), F32),
                        pltpu.VMEM((HEADS_PER_SLAB, tq, 1), F32)],
        compiler_params=_params("parallel", "parallel", "arbitrary"),
        name="stick_breaking",
    )(q_arr, kt_arr, vt_arr, kf_arr, vf_arr)


def _split_bf16(x):
    hi = x.astype(BF16)
    return hi, (x - hi.astype(F32)).astype(BF16)


def _merge_kernel(xp_ref, xs_ref, oap_ref, oas_ref, obp_ref, obs_ref, ga_ref, gb_ref, wo_ref,
                  gf_ref, wr_ref, br_ref, x1_ref, xt_ref, idx_ref, w_ref, rank_ref, cnt_ref,
                  *, tiles_p):
    prompt = pl.program_id(0) < tiles_p
    pick = lambda p_ref, s_ref: jnp.where(prompt, p_ref[...], s_ref[...])
    o = jnp.concatenate([_rms(pick(oap_ref, oas_ref), ga_ref[...]),
                         _rms(pick(obp_ref, obs_ref), gb_ref[...])], axis=-1).astype(BF16)
    x1 = pick(xp_ref, xs_ref) + jnp.dot(o, wo_ref[...], preferred_element_type=F32)
    x1_ref[...] = x1
    xn = _rms(x1, gf_ref[...])
    xh, xl = _split_bf16(xn)
    wh, wl = _split_bf16(wr_ref[...])
    logits = (jnp.dot(xh, wh, preferred_element_type=F32)
              + jnp.dot(xl, wh, preferred_element_type=F32)
              + jnp.dot(xh, wl, preferred_element_type=F32)) + br_ref[...]
    tm, ne = logits.shape
    lane = lax.broadcasted_iota(I32, (tm, ne), 1)
    slot = lax.broadcasted_iota(I32, (tm, TOP_K), 1)
    work = logits
    top = None
    denom = None
    sels = []
    idx = jnp.zeros((tm, TOP_K), I32)
    wts = jnp.zeros((tm, TOP_K), F32)
    for r in range(TOP_K):
        m = jnp.max(work, axis=-1, keepdims=True)
        first = jnp.min(jnp.where(work == m, lane, ne), axis=-1, keepdims=True)
        sel = lane == first
        if top is None:
            top = m
        e = jnp.exp(m - top)
        denom = e if denom is None else denom + e
        idx = jnp.where(slot == r, first, idx)
        wts = jnp.where(slot == r, e, wts)
        sels.append(sel)
        work = jnp.where(sel, -jnp.inf, work)
    idx_ref[...] = idx
    w_ref[...] = wts / denom
    chosen = sels[0] | sels[1] | sels[2] | sels[3]
    onehot = jnp.where(chosen, 1.0, 0.0)
    rr = lax.broadcasted_iota(I32, (tm, tm), 0)
    cc = lax.broadcasted_iota(I32, (tm, tm), 1)
    lower = jnp.where(cc < rr, 1.0, 0.0).astype(BF16)
    before = jnp.dot(lower, onehot.astype(BF16), preferred_element_type=F32)
    rank = jnp.zeros((tm, TOP_K), I32)
    for r in range(TOP_K):
        rk = jnp.sum(jnp.where(sels[r], before, 0.0), axis=-1, keepdims=True)
        rank = jnp.where(slot == r, rk.astype(I32), rank)
    rank_ref[...] = rank
    cnt = jnp.sum(onehot, axis=0, keepdims=True).astype(I32)
    cnt_ref[...] = cnt
    chunks = lax.shift_right_logical(cnt + (SEG - 1), SEG.bit_length() - 1)
    ee = lax.broadcasted_iota(I32, (ne, ne), 0)
    ff = lax.broadcasted_iota(I32, (ne, ne), 1)
    earlier = jnp.where(ee < ff, 1.0, 0.0).astype(BF16)
    chunks8 = jnp.broadcast_to(chunks.astype(F32), (SEG, ne)).astype(BF16)
    seg_row = (jnp.dot(chunks8, earlier, preferred_element_type=F32)[0:1] * SEG).astype(I32)
    slots = _slots_of(idx, rank, seg_row)
    ts = xt_ref.shape[0]
    lane_s = lax.broadcasted_iota(I32, (tm, ts), 1)
    hit = ((lane_s == slots[0]) | (lane_s == slots[1]) | (lane_s == slots[2])
           | (lane_s == slots[3]))
    place = jnp.where(hit, 1.0, 0.0).astype(BF16)
    xt_ref[...] = lax.dot_general(place, xn.astype(BF16), (((0,), (0,)), ((), ())),
                                  preferred_element_type=F32)


def _merge(xp, xs, oa_p, oa_s, ob_p, ob_s, ga, gb, wo_bf, gf, w_router, b_router):
    (n_p, dm), n_s = xp.shape, xs.shape[0]
    n = n_p + n_s
    da = oa_p.shape[1]
    ne = w_router.shape[1]
    tm = ROW_TILE
    assert n_p % tm == 0 and n_s % tm == 0
    tiles_p = n_p // tm
    row = lambda i: (i, 0)
    fixed = lambda i: (0, 0)
    from_p = lambda i: (jnp.minimum(i, tiles_p - 1), 0)
    from_s = lambda i: (jnp.maximum(i - tiles_p, 0), 0)
    return pl.pallas_call(
        functools.partial(_merge_kernel, tiles_p=tiles_p),
        grid=(n // tm,),
        in_specs=[pl.BlockSpec((tm, dm), from_p),
                  pl.BlockSpec((tm, dm), from_s),
                  pl.BlockSpec((tm, da), from_p),
                  pl.BlockSpec((tm, da), from_s),
                  pl.BlockSpec((tm, da), from_p),
                  pl.BlockSpec((tm, da), from_s),
                  pl.BlockSpec((1, da), fixed),
                  pl.BlockSpec((1, da), fixed),
                  pl.BlockSpec((2 * da, dm), fixed),
                  pl.BlockSpec((1, dm), fixed),
                  pl.BlockSpec((dm, ne), fixed),
                  pl.BlockSpec((1, ne), fixed)],
        out_specs=[pl.BlockSpec((tm, dm), row),
                   pl.BlockSpec((_tile_slots(ne), dm), row),
                   pl.BlockSpec((tm, TOP_K), row),
                   pl.BlockSpec((tm, TOP_K), row),
                   pl.BlockSpec((tm, TOP_K), row),
                   pl.BlockSpec((None, 1, ne), lambda i: (i, 0, 0))],
        out_shape=[jax.ShapeDtypeStruct((n, dm), F32),
                   jax.ShapeDtypeStruct((n // tm * _tile_slots(ne), dm), F32),
                   jax.ShapeDtypeStruct((n, TOP_K), I32),
                   jax.ShapeDtypeStruct((n, TOP_K), F32),
                   jax.ShapeDtypeStruct((n, TOP_K), I32),
                   jax.ShapeDtypeStruct((n // tm, 1, ne), I32)],
        compiler_params=_params("parallel"),
        name="merge_router",
    )(xp, xs, oa_p, oa_s, ob_p, ob_s, ga.reshape(1, da), gb.reshape(1, da), wo_bf,
      gf.reshape(1, dm), w_router, b_router.reshape(1, ne))


def _tile_slots(ne):
    return ROW_TILE * TOP_K + ne * SEG


def _slots_of(idx, rank, seg_row):
    tm, ne = idx.shape[0], seg_row.shape[-1]
    lane = lax.broadcasted_iota(I32, (tm, ne), 1)
    return [jnp.sum(jnp.where(lane == idx[:, r:r + 1], seg_row, 0), axis=-1, keepdims=True)
            + rank[:, r:r + 1] for r in range(TOP_K)]


def _chunk_copy(src_ref, src_row, dst_ref, dst_row, sem):
    aligned = lambda row: row if isinstance(row, int) else pl.multiple_of(row, SEG)
    return pltpu.make_async_copy(src_ref.at[pl.ds(aligned(src_row), SEG), :],
                                 dst_ref.at[pl.ds(aligned(dst_row), SEG), :], sem)


def _moe_plan(cnt, n_rows):
    nt, ne = cnt.shape
    tm = MOE_TILE
    seg_len = (cnt + SEG - 1) // SEG * SEG
    seg_row = jnp.cumsum(seg_len, axis=1) - seg_len
    rows = jnp.sum(seg_len, axis=0)
    region = (rows + tm - 1) // tm * tm
    base = jnp.cumsum(region) - region
    dst_row = base[None, :] + jnp.cumsum(seg_len, axis=0) - seg_len
    n_tiles = (n_rows * TOP_K + ne * ((SEG - 1) * nt + tm - 1)) // tm
    tile_end = jnp.cumsum(region // tm)
    tiles = jnp.arange(n_tiles, dtype=I32)
    tile_expert = jnp.minimum(jnp.sum((tiles[:, None] >= tile_end[None, :]).astype(I32), axis=1),
                              ne - 1)
    ts = _tile_slots(ne)
    chunks = seg_len // SEG
    seg_start = ((base // SEG)[:, None] + jnp.cumsum(chunks.T, axis=1) - chunks.T).reshape(-1)
    seg_end = seg_start + chunks.T.reshape(-1)
    seg_src = (jnp.arange(nt, dtype=I32)[None, :] * ts + seg_row.T).reshape(-1)
    g = jnp.arange(n_tiles * tm // SEG, dtype=I32)[:, None]
    inside = (g >= seg_start[None, :]) & (g < seg_end[None, :])
    src = jnp.sum(jnp.where(inside, seg_src[None, :] + (g - seg_start[None, :]) * SEG, 0), axis=1)
    src = jnp.where(jnp.any(inside, axis=1), src, ts - SEG)
    flat = lambda a: a.reshape(-1).astype(I32)
    return dict(seg_row=seg_row.astype(I32).reshape(nt, 1, ne), seg_row_s=flat(seg_row),
                chunks_s=flat(chunks), dst_row_s=flat(dst_row), src_row_s=flat(src),
                tile_expert=tile_expert.astype(I32),
                tile_valid=(tiles < tile_end[ne - 1]).astype(I32), n_slots=n_tiles * tm)


def _ffn_kernel(te_ref, tv_ref, src_ref, xt_ref, wg_ref, bg_ref, wu_ref, bu_ref,
                wd_ref, bd_ref, y_ref, xbuf_ref, wgb_ref, wub_ref, wdb_ref, sem):
    t = pl.program_id(0)
    tm = y_ref.shape[0]
    n_chunks = tm // SEG
    cur = t % 2
    new_expert = (t == 0) | (te_ref[t] != te_ref[jnp.maximum(t - 1, 0)])

    def gather(tile, slot):
        def start(c, carry):
            _chunk_copy(xt_ref, src_ref[tile * n_chunks + c], xbuf_ref.at[slot], c * SEG,
                        sem.at[slot]).start()
            return carry

        lax.fori_loop(0, n_chunks, start, 0)

    @pl.when(t == 0)
    def _():
        gather(0, 0)

    nxt = jnp.minimum(t + 1, pl.num_programs(0) - 1)
    more = (t + 1 < pl.num_programs(0)) & (tv_ref[nxt] != 0)
    for slot in range(2):
        @pl.when(more & (cur != slot))
        def _():
            gather(t + 1, slot)

    @pl.when(new_expert)
    def _():
        wgb_ref[...] = wg_ref[...].astype(BF16)
        wub_ref[...] = wu_ref[...].astype(BF16)
        wdb_ref[...] = wd_ref[...].astype(BF16)

    @pl.when(tv_ref[t] != 0)
    def _():
        pltpu.make_async_copy(xt_ref.at[pl.ds(0, tm), :], xbuf_ref.at[cur], sem.at[cur]).wait()
        x = xbuf_ref[cur].astype(BF16)
        g = jnp.minimum(jnp.dot(x, wgb_ref[...], preferred_element_type=F32) + bg_ref[...],
                        SWIGLU_LIMIT)
        u = jnp.clip(jnp.dot(x, wub_ref[...], preferred_element_type=F32) + bu_ref[...],
                     -SWIGLU_LIMIT, SWIGLU_LIMIT)
        h = g * jax.nn.sigmoid(SWIGLU_ALPHA * g) * (u + 1.0)
        y_ref[...] = jnp.dot(h.astype(BF16), wdb_ref[...], preferred_element_type=F32) + bd_ref[...]

    @pl.when(tv_ref[t] == 0)
    def _():
        y_ref[...] = jnp.zeros_like(y_ref)


def _expert_ffn(plan, xt, wg, bg, wu, bu, wd, bd):
    n_slots, dm = plan["n_slots"], xt.shape[1]
    ne, _, dff = wg.shape
    tm = MOE_TILE
    row = lambda t, *_: (t, 0)
    exp3 = lambda t, te, *_: (te[t], 0, 0)
    return pl.pallas_call(
        _ffn_kernel,
        grid_spec=pltpu.PrefetchScalarGridSpec(
            num_scalar_prefetch=3,
            grid=(n_slots // tm,),
            in_specs=[pl.BlockSpec(memory_space=pl.ANY),
                      pl.BlockSpec((None, dm, dff), exp3),
                      pl.BlockSpec((None, 1, dff), exp3),
                      pl.BlockSpec((None, dm, dff), exp3),
                      pl.BlockSpec((None, 1, dff), exp3),
                      pl.BlockSpec((None, dff, dm), exp3),
                      pl.BlockSpec((None, 1, dm), exp3)],
            out_specs=pl.BlockSpec((tm, dm), row),
            scratch_shapes=[pltpu.VMEM((2, tm, dm), F32),
                            pltpu.VMEM((dm, dff), BF16),
                            pltpu.VMEM((dm, dff), BF16),
                            pltpu.VMEM((dff, dm), BF16),
                            pltpu.SemaphoreType.DMA((2,))]),
        out_shape=jax.ShapeDtypeStruct((n_slots, dm), F32),
        compiler_params=_params("arbitrary"),
        name="moe_ffn",
    )(plan["tile_expert"], plan["tile_valid"], plan["src_row_s"], xt,
      wg, bg.reshape(ne, 1, dff), wu, bu.reshape(ne, 1, dff), wd, bd.reshape(ne, 1, dm))


def _combine_kernel(seg_s, chunks_s, dst_s, ys_ref, idx_ref, rank_ref, w_ref, seg_ref, x1_ref,
                    g_ref, o_ref, buf_ref, sem, *, tile0, ne):
    i = pl.program_id(0)
    t = tile0 + i
    tm, ts = x1_ref.shape[0], buf_ref.shape[0]

    @pl.when(i == 0)
    def _():
        buf_ref[...] = jnp.zeros_like(buf_ref)

    def fetch(e, started):
        k = t * ne + e
        n_chunks = chunks_s[k]

        def start(c, carry):
            _chunk_copy(ys_ref, dst_s[k] + c * SEG, buf_ref, seg_s[k] + c * SEG, sem).start()
            return carry

        lax.fori_loop(0, n_chunks, start, 0)
        return started + n_chunks

    def wait(c, carry):
        _chunk_copy(ys_ref, 0, buf_ref, 0, sem).wait()
        return carry

    started = lax.fori_loop(0, ne, fetch, 0)

    slots = _slots_of(idx_ref[...], rank_ref[...], seg_ref[...])
    lane = lax.broadcasted_iota(I32, (tm, ts), 1)
    w = w_ref[...]
    gate = jnp.zeros((tm, ts), F32)
    for r in range(TOP_K):
        gate = jnp.where(lane == slots[r], w[:, r:r + 1], gate)
    gate = gate.astype(BF16)
    lax.fori_loop(0, started, wait, 0)
    y = jnp.dot(gate, buf_ref[...].astype(BF16), preferred_element_type=F32)
    o_ref[...] = _rms(x1_ref[...] + y, g_ref[...])


def _combine(plan, tile0, n, ys, idx, rank, wts, x1, gfin):
    dm = x1.shape[1]
    tm = ROW_TILE
    ne = plan["seg_row"].shape[-1]
    tile = lambda i, *_: (tile0 + i, 0)
    return pl.pallas_call(
        functools.partial(_combine_kernel, tile0=tile0, ne=ne),
        grid_spec=pltpu.PrefetchScalarGridSpec(
            num_scalar_prefetch=3,
            grid=(n // tm,),
            in_specs=[pl.BlockSpec(memory_space=pl.ANY),
                      pl.BlockSpec((tm, TOP_K), tile),
                      pl.BlockSpec((tm, TOP_K), tile),
                      pl.BlockSpec((tm, TOP_K), tile),
                      pl.BlockSpec((None, 1, ne), lambda i, *_: (tile0 + i, 0, 0)),
                      pl.BlockSpec((tm, dm), tile),
                      pl.BlockSpec((1, dm), lambda i, *_: (0, 0))],
            out_specs=pl.BlockSpec((tm, dm), lambda i, *_: (i, 0)),
            scratch_shapes=[pltpu.VMEM((_tile_slots(ne), dm), F32),
                            pltpu.SemaphoreType.DMA]),
        out_shape=jax.ShapeDtypeStruct((n, dm), F32),
        compiler_params=_params("arbitrary"),
        name="moe_combine",
    )(plan["seg_row_s"], plan["chunks_s"], plan["dst_row_s"], ys, idx, rank, wts,
      plan["seg_row"], x1, gfin.reshape(1, dm))


def kernel(x_prompt, x_sample, cache_a_k, cache_a_v, cache_b_k, cache_b_v, norm_attn, w_in,
           rel_bias, g_out_a, g_out_b, w_out, norm_ffn, w_router, b_router, w_gate, b_gate,
           w_up, b_up, w_down, b_down, norm_final):
    depth = w_in.shape[0]
    assert depth == 1, "the MoE combine applies the final norm: single layer only"
    bp, sp, dm = x_prompt.shape
    bs, ts, _ = x_sample.shape
    past = cache_b_k.shape[2]
    wa = cache_a_k.shape[2]
    nh_a, hd = cache_a_k.shape[3], cache_a_k.shape[4]
    nh_b = cache_b_k.shape[3]
    da, db = nh_a * hd, nh_b * hd
    ne = w_router.shape[-1]
    assert da == db and hd == HEAD_DIM and wa == BAND_WINDOW and ts == CHUNK
    slabs = da // LANES
    QA, KA, VA, QB, KB, VB = (c * slabs for c in range(6))
    l = 0
    n_p, n_s = bp * sp, bs * ts

    xp2 = x_prompt.reshape(n_p, dm)
    xs2 = x_sample.reshape(n_s, dm)
    w_in_bf = w_in[l].astype(BF16)
    wp = min(BAND_WINDOW, sp)
    assert sp % ROW_TILE == 0 and wp % ROW_TILE == 0
    qkv_p, ka_p, va_p, kb_p, vb_p = _qkv_proj(xp2, norm_attn[l], w_in_bf,
                                              seq_tiles=sp // ROW_TILE, band_tiles=wp // ROW_TILE)
    qkv_s, ka_s, va_s, kb_s, vb_s = _qkv_proj(xs2, norm_attn[l], w_in_bf)
    qkv_p3 = qkv_p.reshape(bp, sp, 6 * da)
    qkv_s3 = qkv_s.reshape(bs, ts, 6 * da)

    bias = _band_bias(rel_bias[l])
    tile_bias = _band_tile_bias(bias)

    oa_p = _band_attention_prompt(qkv_p3, QA, KA, VA, tile_bias)
    ob_p = _sb_attention(qkv_p3, QB, qkv_p3, KB, qkv_p3, VB, qkv_p3, KB, qkv_p3, VB,
                         slabs, tq=min(SB_TQ, sp), q_base=0)

    oa_s = _band_attention_sample(qkv_s3, QA, KA, VA, cache_a_k[l], cache_a_v[l], bias)
    ob_s = _sb_attention(qkv_s3, QB, qkv_s3, KB, qkv_s3, VB,
                         cache_b_k[l].reshape(bs, past, db), 0,
                         cache_b_v[l].reshape(bs, past, db), 0, slabs, tq=ts, q_base=past)

    w_out_bf = w_out[l].astype(BF16)
    x1, xt, idx, wts, rank, cnt = _merge(
        xp2, xs2, oa_p.reshape(n_p, da), oa_s.reshape(n_s, da), ob_p.reshape(n_p, db),
        ob_s.reshape(n_s, db), g_out_a[l], g_out_b[l], w_out_bf, norm_ffn[l], w_router[l],
        b_router[l])
    plan = _moe_plan(cnt.reshape(-1, ne), n_p + n_s)
    ys = _expert_ffn(plan, xt, w_gate[l], b_gate[l], w_up[l], b_up[l], w_down[l], b_down[l])
    combine = functools.partial(_combine, plan, ys=ys, idx=idx, rank=rank, wts=wts, x1=x1,
                                gfin=norm_final)
    y_prompt = combine(0, n_p).reshape(bp, sp, dm)
    y_sample = combine(n_p // ROW_TILE, n_s).reshape(bs, ts, dm)

    heads_a = lambda t, b_, n_: t.reshape(b_, n_, nh_a, hd)
    heads_b = lambda t, b_, n_: t.reshape(b_, n_, nh_b, hd)
    new_a_k_prompt = heads_a(ka_p, bp, wp)[None]
    new_a_v_prompt = heads_a(va_p, bp, wp)[None]
    new_b_k_prompt = heads_b(kb_p, bp, sp)[None]
    new_b_v_prompt = heads_b(vb_p, bp, sp)[None]
    new_a_k_sample = jnp.concatenate([cache_a_k[l][:, ts:], heads_a(ka_s, bs, ts)], axis=1)[None]
    new_a_v_sample = jnp.concatenate([cache_a_v[l][:, ts:], heads_a(va_s, bs, ts)], axis=1)[None]
    new_b_k_sample = heads_b(kb_s, bs, ts)[None]
    new_b_v_sample = heads_b(vb_s, bs, ts)[None]
    return (y_prompt, y_sample, new_a_k_prompt, new_a_v_prompt, new_b_k_prompt, new_b_v_prompt,
            new_a_k_sample, new_a_v_sample, new_b_k_sample, new_b_v_sample)
```

```python
import functools

import jax
import jax.numpy as jnp
from jax import lax
from jax.experimental import pallas as pl
from jax.experimental.pallas import tpu as pltpu

F32 = jnp.float32
BF16 = jnp.bfloat16
I32 = jnp.int32

CHUNK = 64
LEFT_CHUNKS = 8
BAND_WINDOW = LEFT_CHUNKS * CHUNK
BAND = BAND_WINDOW + CHUNK
HEAD_DIM = 64
HEADS_PER_SLAB = 2
LANES = 128
REL_CLIP = 128
TOP_K = 4
SWIGLU_LIMIT = 7.0
SWIGLU_ALPHA = 1.702
EPS = 1e-6
SCALE = HEAD_DIM ** -0.5

ROW_TILE = 512
BAND_TQ = 256
BAND_KEYS = BAND_TQ + BAND_WINDOW
BAND_VARIANTS = BAND_WINDOW // BAND_TQ + 1
BAND_SLABS = 4
SB_TQ = 512
SB_TK = 256
SB_UNROLL = 1
SB_DEAD = -104.0
MOE_TILE = 512
SEG = 8
VMEM_LIMIT = 56 * 1024 * 1024


def _params(*sem):
    return pltpu.CompilerParams(dimension_semantics=sem, vmem_limit_bytes=VMEM_LIMIT)


def _rms(x, g):
    ms = jnp.mean(x * x, axis=-1, keepdims=True)
    return (x * lax.rsqrt(ms + EPS)) * g


def _head_mask(h):
    lane = lax.broadcasted_iota(I32, (1, LANES), 1)
    return (lane >= h * HEAD_DIM) & (lane < (h + 1) * HEAD_DIM)


def _qkv_kernel(x_ref, g_ref, w_ref, qkv_ref, ka_ref, va_ref, kb_ref, vb_ref):
    xb = _rms(x_ref[...], g_ref[...]).astype(BF16)
    d = ka_ref.shape[-1]
    f32_out = {1: ka_ref, 2: va_ref, 4: kb_ref, 5: vb_ref}
    for c in range(6):
        r = jnp.dot(xb, w_ref[:, c * d:(c + 1) * d], preferred_element_type=F32)
        if c in f32_out:
            f32_out[c][...] = r
        else:
            r = r * SCALE
        qkv_ref[:, c * d:(c + 1) * d] = r.astype(BF16)


def _qkv_proj(x2d, g, w_bf, seq_tiles=1, band_tiles=1):
    n, dm = x2d.shape
    d3 = w_bf.shape[1]
    d = d3 // 6
    tm = min(ROW_TILE, n)
    assert (n // tm) % seq_tiles == 0 and band_tiles <= seq_tiles
    row = lambda i: (i, 0)
    fixed = lambda i: (0, 0)
    skip = seq_tiles - band_tiles
    band_row = lambda i: ((i // seq_tiles) * band_tiles + jnp.maximum(i % seq_tiles - skip, 0), 0)
    n_band = n // seq_tiles * band_tiles
    return pl.pallas_call(
        _qkv_kernel,
        grid=(n // tm,),
        in_specs=[pl.BlockSpec((tm, dm), row),
                  pl.BlockSpec((1, dm), fixed),
                  pl.BlockSpec((dm, d3), fixed)],
        out_specs=[pl.BlockSpec((tm, d3), row)] + [pl.BlockSpec((tm, d), band_row)] * 2
        + [pl.BlockSpec((tm, d), row)] * 2,
        out_shape=[jax.ShapeDtypeStruct((n, d3), BF16)]
        + [jax.ShapeDtypeStruct((n_band, d), F32)] * 2
        + [jax.ShapeDtypeStruct((n, d), F32)] * 2,
        compiler_params=_params("arbitrary"),
        name="qkv_proj",
    )(x2d, g.reshape(1, dm), w_bf)


def _bias_kernel(tab_ref, o_ref):
    h = pl.program_id(0)
    i = lax.broadcasted_iota(I32, (CHUNK, BAND), 0)
    j = lax.broadcasted_iota(I32, (CHUNK, BAND), 1)
    idx = jnp.clip(BAND_WINDOW + i - j, -REL_CLIP, REL_CLIP) + REL_CLIP
    n_tab = 2 * REL_CLIP + 1

    def body(t, acc):
        return jnp.where(idx == t, tab_ref[h, t], acc)

    init = jnp.full((CHUNK, BAND), tab_ref[h, n_tab - 1], F32)
    o_ref[...] = lax.fori_loop(0, n_tab - 1, body, init)


def _band_bias(table):
    nh = table.shape[0]
    return pl.pallas_call(
        _bias_kernel,
        grid=(nh,),
        in_specs=[pl.BlockSpec(memory_space=pltpu.SMEM)],
        out_specs=pl.BlockSpec((None, CHUNK, BAND), lambda h: (h, 0, 0)),
        out_shape=jax.ShapeDtypeStruct((nh, CHUNK, BAND), F32),
        compiler_params=_params("arbitrary"),
        name="band_bias",
    )(table)


def _tile_bias_kernel(b_ref, o_ref):
    v = pl.program_id(1)
    o_ref[...] = jnp.full(o_ref.shape, -jnp.inf, F32)
    for c in range(BAND_TQ // CHUNK):
        o_ref[c * CHUNK:(c + 1) * CHUNK, c * CHUNK:c * CHUNK + BAND] = b_ref[...]
    col = lax.broadcasted_iota(I32, o_ref.shape, 1)
    hidden = (BAND_VARIANTS - 1 - v) * BAND_TQ
    o_ref[...] = jnp.where(col >= hidden, o_ref[...], -jnp.inf)


def _band_tile_bias(bias):
    nh = bias.shape[0]
    return pl.pallas_call(
        _tile_bias_kernel,
        grid=(nh, BAND_VARIANTS),
        in_specs=[pl.BlockSpec((None, CHUNK, BAND), lambda h, v: (h, 0, 0))],
        out_specs=pl.BlockSpec((None, None, BAND_TQ, BAND_KEYS), lambda h, v: (h, v, 0, 0)),
        out_shape=jax.ShapeDtypeStruct((nh, BAND_VARIANTS, BAND_TQ, BAND_KEYS), F32),
        compiler_params=_params("parallel", "arbitrary"),
        name="band_tile_bias",
    )(bias)


def _band_tile_kernel(q_ref, *refs):
    n_kb = BAND_KEYS // BAND_TQ
    k_refs, v_refs = refs[:n_kb], refs[n_kb:2 * n_kb]
    bias_ref, o_ref = refs[2 * n_kb], refs[2 * n_kb + 1]
    for sl in range(BAND_SLABS):
        cols = slice(sl * LANES, (sl + 1) * LANES)
        q = q_ref[:, cols]
        kcat = jnp.concatenate([r[:, cols] for r in k_refs], axis=0)
        vcat = jnp.concatenate([r[:, cols] for r in v_refs], axis=0)
        outs = []
        for h in range(HEADS_PER_SLAB):
            qm = jnp.where(_head_mask(h), q, jnp.zeros_like(q))
            s = lax.dot_general(qm, kcat, (((1,), (1,)), ((), ())),
                                preferred_element_type=F32) + bias_ref[sl * HEADS_PER_SLAB + h]
            m = jnp.max(s, axis=-1, keepdims=True)
            p = jnp.exp(s - m)
            l = jnp.sum(p, axis=-1, keepdims=True)
            pv = jnp.dot(p.astype(BF16), vcat, preferred_element_type=F32)
            outs.append(pv / l)
        o_ref[:, cols] = jnp.where(_head_mask(0), outs[0], outs[1])


def _band_attention_prompt(qkv3, q_col0, k_col0, v_col0, tile_bias):
    b, t, _ = qkv3.shape
    nslab = tile_bias.shape[0] // HEADS_PER_SLAB
    n_kb = BAND_KEYS // BAND_TQ
    assert nslab % BAND_SLABS == 0 and all(c % BAND_SLABS == 0 for c in (q_col0, k_col0, v_col0))
    blk = (None, BAND_TQ, BAND_SLABS * LANES)
    group = lambda col0, g: col0 // BAND_SLABS + g

    def kv_spec(col0, back):
        return pl.BlockSpec(blk, lambda g, bi, j: (bi, jnp.maximum(j - back, 0), group(col0, g)))

    in_specs = [pl.BlockSpec(blk, lambda g, bi, j: (bi, j, group(q_col0, g)))]
    in_specs += [kv_spec(k_col0, n_kb - 1 - i) for i in range(n_kb)]
    in_specs += [kv_spec(v_col0, n_kb - 1 - i) for i in range(n_kb)]
    in_specs += [pl.BlockSpec((BAND_SLABS * HEADS_PER_SLAB, None, BAND_TQ, BAND_KEYS),
                              lambda g, bi, j: (g, jnp.minimum(j, BAND_VARIANTS - 1), 0, 0))]
    return pl.pallas_call(
        _band_tile_kernel,
        grid=(nslab // BAND_SLABS, b, t // BAND_TQ),
        in_specs=in_specs,
        out_specs=pl.BlockSpec(blk, lambda g, bi, j: (bi, j, g)),
        out_shape=jax.ShapeDtypeStruct((b, t, nslab * LANES), F32),
        compiler_params=_params("parallel", "parallel", "arbitrary"),
        name="band_attention_prompt",
    )(qkv3, *([qkv3] * (2 * n_kb)), tile_bias)


def _band_chunk_kernel(q_ref, kp_ref, kc_ref, vp_ref, vc_ref, bias_ref, o_ref):
    def cached(ref, sl):
        heads = [ref[:, sl * HEADS_PER_SLAB + h, :] for h in range(HEADS_PER_SLAB)]
        return jnp.concatenate(heads, axis=-1).astype(BF16)

    for sl in range(o_ref.shape[-1] // LANES):
        cols = slice(sl * LANES, (sl + 1) * LANES)
        qc = q_ref[:, cols]
        kwin = jnp.concatenate([cached(kp_ref, sl), kc_ref[:, cols]], axis=0)
        vwin = jnp.concatenate([cached(vp_ref, sl), vc_ref[:, cols]], axis=0)
        outs = []
        for h in range(HEADS_PER_SLAB):
            qm = jnp.where(_head_mask(h), qc, jnp.zeros_like(qc))
            s = lax.dot_general(qm, kwin, (((1,), (1,)), ((), ())),
                                preferred_element_type=F32) + bias_ref[sl * HEADS_PER_SLAB + h]
            m = jnp.max(s, axis=-1, keepdims=True)
            p = jnp.exp(s - m)
            l = jnp.sum(p, axis=-1, keepdims=True)
            pv = jnp.dot(p.astype(BF16), vwin, preferred_element_type=F32)
            outs.append(pv / l)
        o_ref[:, cols] = jnp.where(_head_mask(0), outs[0], outs[1])


def _band_attention_sample(qkv3, q_col0, k_col0, v_col0, cache_k, cache_v, bias):
    b, t, _ = qkv3.shape
    nh = bias.shape[0]
    d = nh // HEADS_PER_SLAB * LANES
    assert all(c * LANES % d == 0 for c in (q_col0, k_col0, v_col0))
    new = lambda col0: pl.BlockSpec((None, CHUNK, d), lambda bi: (bi, 0, col0 * LANES // d))
    old = pl.BlockSpec((None, BAND_WINDOW, nh, HEAD_DIM), lambda bi: (bi, 0, 0, 0))
    return pl.pallas_call(
        _band_chunk_kernel,
        grid=(b,),
        in_specs=[new(q_col0), old, new(k_col0), old, new(v_col0),
                  pl.BlockSpec((nh, CHUNK, BAND), lambda bi: (0, 0, 0))],
        out_specs=pl.BlockSpec((None, CHUNK, d), lambda bi: (bi, 0, 0)),
        out_shape=jax.ShapeDtypeStruct((b, t, d), F32),
        compiler_params=_params("parallel"),
        name="band_attention_sample",
    )(qkv3, cache_k, qkv3, cache_v, qkv3, bias)


def _sb_kernel(q_ref, kt_ref, vt_ref, kf_ref, vf_ref, o_ref,
               qm_ref, nu_ref, d_ref, tot_ref, acc_ref, run_ref, *, q_base):
    i = pl.program_id(2)
    tq = q_ref.shape[0]
    tile_keys = kt_ref.shape[0]
    tk = SB_TK
    qpos0 = q_base + i * tq
    n_full = qpos0 // tk
    n_masked = -(-tile_keys // tk)

    q = q_ref[...]
    for h in range(HEADS_PER_SLAB):
        qm_ref[h] = jnp.where(_head_mask(h), q, jnp.zeros_like(q))
    jj = lax.broadcasted_iota(I32, (tk, tk), 0)
    ss = lax.broadcasted_iota(I32, (tk, tk), 1)
    nu_ref[...] = jnp.where(jj >= ss, -1.0, 0.0).astype(BF16)
    acc_ref[...] = jnp.zeros_like(acc_ref)
    run_ref[...] = jnp.zeros_like(run_ref)

    def tile_block(ref, m):
        rows = min(tk, tile_keys - m * tk)
        blk = ref[m * tk:m * tk + rows, :].astype(BF16)
        if rows < tk:
            blk = jnp.concatenate([blk, jnp.zeros((tk - rows, LANES), BF16)], axis=0)
        return blk

    def full_block(ref, kb):
        k0 = pl.multiple_of(kb * tk, tk)
        return ref[pl.ds(k0, tk), :].astype(BF16)

    def stage_a(kblk, tile_m):
        if tile_m is not None:
            row = lax.broadcasted_iota(I32, (tq, tk), 0)
            col = lax.broadcasted_iota(I32, (tq, tk), 1)
            vis = tile_m * tk + col < row
        for h in range(HEADS_PER_SLAB):
            z = lax.dot_general(qm_ref[h], kblk, (((1,), (1,)), ((), ())),
                                preferred_element_type=F32)
            sp = jnp.maximum(z, 0.0) + jnp.log(1.0 + jnp.exp(-jnp.abs(z)))
            if tile_m is not None:
                sp = jnp.where(vis, sp, 0.0)
            after = jnp.dot(sp.astype(BF16), nu_ref[...], preferred_element_type=F32)
            d = z + after
            if tile_m is not None:
                d = jnp.where(vis, d, -jnp.inf)
            d_ref[h] = d
            tot_ref[h] = after[:, 0:1]

    def stage_b(vblk):
        for h in range(HEADS_PER_SLAB):
            run = run_ref[h]
            a = jnp.exp(d_ref[h] + run)
            acc_ref[h] += jnp.dot(a.astype(BF16), vblk, preferred_element_type=F32)
            run_ref[h] = run + tot_ref[h]

    def step(kb):
        stage_b(full_block(vf_ref, kb + 1))
        stage_a(full_block(kf_ref, kb), None)

    def alive():
        r = jnp.maximum(run_ref[0], run_ref[1])
        return (jnp.max(r) >= SB_DEAD).astype(I32)

    stage_a(tile_block(kt_ref, n_masked - 1), n_masked - 1)
    for m in range(n_masked - 2, -1, -1):
        stage_b(tile_block(vt_ref, m + 1))
        stage_a(tile_block(kt_ref, m), m)

    @pl.when(n_full == 0)
    def _():
        stage_b(tile_block(vt_ref, 0))

    @pl.when(n_full > 0)
    def _():
        stage_b(tile_block(vt_ref, 0))
        stage_a(full_block(kf_ref, n_full - 1), None)

        def cond(c):
            kb, live = c
            return (kb >= SB_UNROLL - 1) & (live > 0)

        def body(c):
            kb, _ = c
            for u in range(SB_UNROLL):
                step(kb - u)
            return kb - SB_UNROLL, alive()

        kb, live = lax.while_loop(cond, body, (n_full - 2, alive()))
        for _ in range(SB_UNROLL - 1):
            go = (kb >= 0) & (live > 0)

            @pl.when(go)
            def _():
                step(kb)

            kb = jnp.where(go, kb - 1, kb)
            live = alive()

        @pl.when(live > 0)
        def _():
            stage_b(full_block(vf_ref, kb + 1))

    o_ref[...] = jnp.where(_head_mask(0), acc_ref[0], acc_ref[1])


def _sb_attention(q_arr, q_col0, kt_arr, kt_col0, vt_arr, vt_col0,
                  kf_arr, kf_col0, vf_arr, vf_col0, nslab, tq, q_base):
    b, t, _ = q_arr.shape
    tf = kf_arr.shape[1]
    assert q_base % SB_TK == 0 and t % tq == 0 and tf % SB_TK == 0
    assert tq % SB_TK == 0 or t == tq
    assert q_base + t - tq <= tf
    tile = lambda col0: pl.BlockSpec((None, tq, LANES), lambda bi, hp, i: (bi, i, col0 + hp))
    full = lambda col0: pl.BlockSpec((None, tf, LANES), lambda bi, hp, i: (bi, 0, col0 + hp))
    return pl.pallas_call(
        functools.partial(_sb_kernel, q_base=q_base),
        grid=(b, nslab, t // tq),
        in_specs=[tile(q_col0), tile(kt_col0), tile(vt_col0), full(kf_col0), full(vf_col0)],
        out_specs=pl.BlockSpec((None, tq, LANES), lambda bi, hp, i: (bi, i, hp)),
        out_shape=jax.ShapeDtypeStruct((b, t, nslab * LANES), F32),
        scratch_shapes=[pltpu.VMEM((HEADS_PER_SLAB, tq, LANES), BF16),
                        pltpu.VMEM((SB_TK, SB_TK), BF16),
                        pltpu.VMEM((HEADS_PER_SLAB, tq, SB_TK), F32),
                        pltpu.VMEM((HEADS_PER_SLAB, tq, 1), F32),
                        pltpu.VMEM((HEADS_PER_SLAB, tq, LANES), F32),
                        pltpu.VMEM((HEADS_PER_SLAB, tq, 1), F32)],
        compiler_params=_params("parallel", "parallel", "arbitrary"),
        name="stick_breaking",
    )(q_arr, kt_arr, vt_arr, kf_arr, vf_arr)


def _split_bf16(x):
    hi = x.astype(BF16)
    return hi, (x - hi.astype(F32)).astype(BF16)


def _merge_kernel(xp_ref, xs_ref, oap_ref, oas_ref, obp_ref, obs_ref, ga_ref, gb_ref, wo_ref,
                  gf_ref, wr_ref, br_ref, x1_ref, xt_ref, idx_ref, w_ref, rank_ref, cnt_ref,
                  *, tiles_p):
    prompt = pl.program_id(0) < tiles_p
    pick = lambda p_ref, s_ref: jnp.where(prompt, p_ref[...], s_ref[...])
    o = jnp.concatenate([_rms(pick(oap_ref, oas_ref), ga_ref[...]),
                         _rms(pick(obp_ref, obs_ref), gb_ref[...])], axis=-1).astype(BF16)
    x1 = pick(xp_ref, xs_ref) + jnp.dot(o, wo_ref[...], preferred_element_type=F32)
    x1_ref[...] = x1
    xn = _rms(x1, gf_ref[...])
    xh, xl = _split_bf16(xn)
    wh, wl = _split_bf16(wr_ref[...])
    logits = (jnp.dot(xh, wh, preferred_element_type=F32)
              + jnp.dot(xl, wh, preferred_element_type=F32)
              + jnp.dot(xh, wl, preferred_element_type=F32)) + br_ref[...]
    tm, ne = logits.shape
    lane = lax.broadcasted_iota(I32, (tm, ne), 1)
    slot = lax.broadcasted_iota(I32, (tm, TOP_K), 1)
    work = logits
    top = None
    denom = None
    sels = []
    idx = jnp.zeros((tm, TOP_K), I32)
    wts = jnp.zeros((tm, TOP_K), F32)
    for r in range(TOP_K):
        m = jnp.max(work, axis=-1, keepdims=True)
        first = jnp.min(jnp.where(work == m, lane, ne), axis=-1, keepdims=True)
        sel = lane == first
        if top is None:
            top = m
        e = jnp.exp(m - top)
        denom = e if denom is None else denom + e
        idx = jnp.where(slot == r, first, idx)
        wts = jnp.where(slot == r, e, wts)
        sels.append(sel)
        work = jnp.where(sel, -jnp.inf, work)
    idx_ref[...] = idx
    w_ref[...] = wts / denom
    chosen = sels[0] | sels[1] | sels[2] | sels[3]
    onehot = jnp.where(chosen, 1.0, 0.0)
    rr = lax.broadcasted_iota(I32, (tm, tm), 0)
    cc = lax.broadcasted_iota(I32, (tm, tm), 1)
    lower = jnp.where(cc < rr, 1.0, 0.0).astype(BF16)
    before = jnp.dot(lower, onehot.astype(BF16), preferred_element_type=F32)
    rank = jnp.zeros((tm, TOP_K), I32)
    for r in range(TOP_K):
        rk = jnp.sum(jnp.where(sels[r], before, 0.0), axis=-1, keepdims=True)
        rank = jnp.where(slot == r, rk.astype(I32), rank)
    rank_ref[...] = rank
    cnt = jnp.sum(onehot, axis=0, keepdims=True).astype(I32)
    cnt_ref[...] = cnt
    chunks = lax.shift_right_logical(cnt + (SEG - 1), SEG.bit_length() - 1)
    ee = lax.broadcasted_iota(I32, (ne, ne), 0)
    ff = lax.broadcasted_iota(I32, (ne, ne), 1)
    earlier = jnp.where(ee < ff, 1.0, 0.0).astype(BF16)
    chunks8 = jnp.broadcast_to(chunks.astype(F32), (SEG, ne)).astype(BF16)
    seg_row = (jnp.dot(chunks8, earlier, preferred_element_type=F32)[0:1] * SEG).astype(I32)
    slots = _slots_of(idx, rank, seg_row)
    ts = xt_ref.shape[0]
    lane_s = lax.broadcasted_iota(I32, (tm, ts), 1)
    hit = ((lane_s == slots[0]) | (lane_s == slots[1]) | (lane_s == slots[2])
           | (lane_s == slots[3]))
    place = jnp.where(hit, 1.0, 0.0).astype(BF16)
    xt_ref[...] = lax.dot_general(place, xn.astype(BF16), (((0,), (0,)), ((), ())),
                                  preferred_element_type=F32)


def _merge(xp, xs, oa_p, oa_s, ob_p, ob_s, ga, gb, wo_bf, gf, w_router, b_router):
    (n_p, dm), n_s = xp.shape, xs.shape[0]
    n = n_p + n_s
    da = oa_p.shape[1]
    ne = w_router.shape[1]
    tm = ROW_TILE
    assert n_p % tm == 0 and n_s % tm == 0
    tiles_p = n_p // tm
    row = lambda i: (i, 0)
    fixed = lambda i: (0, 0)
    from_p = lambda i: (jnp.minimum(i, tiles_p - 1), 0)
    from_s = lambda i: (jnp.maximum(i - tiles_p, 0), 0)
    return pl.pallas_call(
        functools.partial(_merge_kernel, tiles_p=tiles_p),
        grid=(n // tm,),
        in_specs=[pl.BlockSpec((tm, dm), from_p),
                  pl.BlockSpec((tm, dm), from_s),
                  pl.BlockSpec((tm, da), from_p),
                  pl.BlockSpec((tm, da), from_s),
                  pl.BlockSpec((tm, da), from_p),
                  pl.BlockSpec((tm, da), from_s),
                  pl.BlockSpec((1, da), fixed),
                  pl.BlockSpec((1, da), fixed),
                  pl.BlockSpec((2 * da, dm), fixed),
                  pl.BlockSpec((1, dm), fixed),
                  pl.BlockSpec((dm, ne), fixed),
                  pl.BlockSpec((1, ne), fixed)],
        out_specs=[pl.BlockSpec((tm, dm), row),
                   pl.BlockSpec((_tile_slots(ne), dm), row),
                   pl.BlockSpec((tm, TOP_K), row),
                   pl.BlockSpec((tm, TOP_K), row),
                   pl.BlockSpec((tm, TOP_K), row),
                   pl.BlockSpec((None, 1, ne), lambda i: (i, 0, 0))],
        out_shape=[jax.ShapeDtypeStruct((n, dm), F32),
                   jax.ShapeDtypeStruct((n // tm * _tile_slots(ne), dm), F32),
                   jax.ShapeDtypeStruct((n, TOP_K), I32),
                   jax.ShapeDtypeStruct((n, TOP_K), F32),
                   jax.ShapeDtypeStruct((n, TOP_K), I32),
                   jax.ShapeDtypeStruct((n // tm, 1, ne), I32)],
        compiler_params=_params("parallel"),
        name="merge_router",
    )(xp, xs, oa_p, oa_s, ob_p, ob_s, ga.reshape(1, da), gb.reshape(1, da), wo_bf,
      gf.reshape(1, dm), w_router, b_router.reshape(1, ne))


def _tile_slots(ne):
    return ROW_TILE * TOP_K + ne * SEG


def _slots_of(idx, rank, seg_row):
    tm, ne = idx.shape[0], seg_row.shape[-1]
    lane = lax.broadcasted_iota(I32, (tm, ne), 1)
    return [jnp.sum(jnp.where(lane == idx[:, r:r + 1], seg_row, 0), axis=-1, keepdims=True)
            + rank[:, r:r + 1] for r in range(TOP_K)]


def _chunk_copy(src_ref, src_row, dst_ref, dst_row, sem):
    aligned = lambda row: row if isinstance(row, int) else pl.multiple_of(row, SEG)
    return pltpu.make_async_copy(src_ref.at[pl.ds(aligned(src_row), SEG), :],
                                 dst_ref.at[pl.ds(aligned(dst_row), SEG), :], sem)


def _moe_plan(cnt, n_rows):
    nt, ne = cnt.shape
    tm = MOE_TILE
    seg_len = (cnt + SEG - 1) // SEG * SEG
    seg_row = jnp.cumsum(seg_len, axis=1) - seg_len
    rows = jnp.sum(seg_len, axis=0)
    region = (rows + tm - 1) // tm * tm
    base = jnp.cumsum(region) - region
    dst_row = base[None, :] + jnp.cumsum(seg_len, axis=0) - seg_len
    n_tiles = (n_rows * TOP_K + ne * ((SEG - 1) * nt + tm - 1)) // tm
    tile_end = jnp.cumsum(region // tm)
    tiles = jnp.arange(n_tiles, dtype=I32)
    tile_expert = jnp.minimum(jnp.sum((tiles[:, None] >= tile_end[None, :]).astype(I32), axis=1),
                              ne - 1)
    ts = _tile_slots(ne)
    chunks = seg_len // SEG
    seg_start = ((base // SEG)[:, None] + jnp.cumsum(chunks.T, axis=1) - chunks.T).reshape(-1)
    seg_end = seg_start + chunks.T.reshape(-1)
    seg_src = (jnp.arange(nt, dtype=I32)[None, :] * ts + seg_row.T).reshape(-1)
    g = jnp.arange(n_tiles * tm // SEG, dtype=I32)[:, None]
    inside = (g >= seg_start[None, :]) & (g < seg_end[None, :])
    src = jnp.sum(jnp.where(inside, seg_src[None, :] + (g - seg_start[None, :]) * SEG, 0), axis=1)
    src = jnp.where(jnp.any(inside, axis=1), src, ts - SEG)
    flat = lambda a: a.reshape(-1).astype(I32)
    return dict(seg_row=seg_row.astype(I32).reshape(nt, 1, ne), seg_row_s=flat(seg_row),
                chunks_s=flat(chunks), dst_row_s=flat(dst_row), src_row_s=flat(src),
                tile_expert=tile_expert.astype(I32),
                tile_valid=(tiles < tile_end[ne - 1]).astype(I32), n_slots=n_tiles * tm)


def _ffn_kernel(te_ref, tv_ref, src_ref, xt_ref, wg_ref, bg_ref, wu_ref, bu_ref,
                wd_ref, bd_ref, y_ref, xbuf_ref, wgb_ref, wub_ref, wdb_ref, sem):
    t = pl.program_id(0)
    tm = y_ref.shape[0]
    n_chunks = tm // SEG
    cur = t % 2
    new_expert = (t == 0) | (te_ref[t] != te_ref[jnp.maximum(t - 1, 0)])

    def gather(tile, slot):
        def start(c, carry):
            _chunk_copy(xt_ref, src_ref[tile * n_chunks + c], xbuf_ref.at[slot], c * SEG,
                        sem.at[slot]).start()
            return carry

        lax.fori_loop(0, n_chunks, start, 0)

    @pl.when(t == 0)
    def _():
        gather(0, 0)

    nxt = jnp.minimum(t + 1, pl.num_programs(0) - 1)
    more = (t + 1 < pl.num_programs(0)) & (tv_ref[nxt] != 0)
    for slot in range(2):
        @pl.when(more & (cur != slot))
        def _():
            gather(t + 1, slot)

    @pl.when(new_expert)
    def _():
        wgb_ref[...] = wg_ref[...].astype(BF16)
        wub_ref[...] = wu_ref[...].astype(BF16)
        wdb_ref[...] = wd_ref[...].astype(BF16)

    @pl.when(tv_ref[t] != 0)
    def _():
        pltpu.make_async_copy(xt_ref.at[pl.ds(0, tm), :], xbuf_ref.at[cur], sem.at[cur]).wait()
        x = xbuf_ref[cur].astype(BF16)
        g = jnp.minimum(jnp.dot(x, wgb_ref[...], preferred_element_type=F32) + bg_ref[...],
                        SWIGLU_LIMIT)
        u = jnp.clip(jnp.dot(x, wub_ref[...], preferred_element_type=F32) + bu_ref[...],
                     -SWIGLU_LIMIT, SWIGLU_LIMIT)
        h = g * jax.nn.sigmoid(SWIGLU_ALPHA * g) * (u + 1.0)
        y_ref[...] = jnp.dot(h.astype(BF16), wdb_ref[...], preferred_element_type=F32) + bd_ref[...]

    @pl.when(tv_ref[t] == 0)
    def _():
        y_ref[...] = jnp.zeros_like(y_ref)


def _expert_ffn(plan, xt, wg, bg, wu, bu, wd, bd):
    n_slots, dm = plan["n_slots"], xt.shape[1]
    ne, _, dff = wg.shape
    tm = MOE_TILE
    row = lambda t, *_: (t, 0)
    exp3 = lambda t, te, *_: (te[t], 0, 0)
    return pl.pallas_call(
        _ffn_kernel,
        grid_spec=pltpu.PrefetchScalarGridSpec(
            num_scalar_prefetch=3,
            grid=(n_slots // tm,),
            in_specs=[pl.BlockSpec(memory_space=pl.ANY),
                      pl.BlockSpec((None, dm, dff), exp3),
                      pl.BlockSpec((None, 1, dff), exp3),
                      pl.BlockSpec((None, dm, dff), exp3),
                      pl.BlockSpec((None, 1, dff), exp3),
                      pl.BlockSpec((None, dff, dm), exp3),
                      pl.BlockSpec((None, 1, dm), exp3)],
            out_specs=pl.BlockSpec((tm, dm), row),
            scratch_shapes=[pltpu.VMEM((2, tm, dm), F32),
                            pltpu.VMEM((dm, dff), BF16),
                            pltpu.VMEM((dm, dff), BF16),
                            pltpu.VMEM((dff, dm), BF16),
                            pltpu.SemaphoreType.DMA((2,))]),
        out_shape=jax.ShapeDtypeStruct((n_slots, dm), F32),
        compiler_params=_params("arbitrary"),
        name="moe_ffn",
    )(plan["tile_expert"], plan["tile_valid"], plan["src_row_s"], xt,
      wg, bg.reshape(ne, 1, dff), wu, bu.reshape(ne, 1, dff), wd, bd.reshape(ne, 1, dm))


def _combine_kernel(seg_s, chunks_s, dst_s, ys_ref, idx_ref, rank_ref, w_ref, seg_ref, x1_ref,
                    g_ref, o_ref, buf_ref, sem, *, tile0, ne):
    i = pl.program_id(0)
    t = tile0 + i
    tm, ts = x1_ref.shape[0], buf_ref.shape[1]
    cur = i % 2

    def fetch(tile, slot):
        def expert(e, carry):
            k = tile * ne + e

            def start(c, inner):
                _chunk_copy(ys_ref, dst_s[k] + c * SEG, buf_ref.at[slot], seg_s[k] + c * SEG,
                            sem.at[slot]).start()
                return inner

            lax.fori_loop(0, chunks_s[k], start, 0)
            return carry

        lax.fori_loop(0, ne, expert, 0)

    @pl.when(i == 0)
    def _():
        buf_ref[...] = jnp.zeros_like(buf_ref)
        fetch(t, 0)

    more = i + 1 < pl.num_programs(0)
    for slot in range(2):
        @pl.when(more & (cur != slot))
        def _():
            fetch(t + 1, slot)

    slots = _slots_of(idx_ref[...], rank_ref[...], seg_ref[...])
    lane = lax.broadcasted_iota(I32, (tm, ts), 1)
    w = w_ref[...]
    gate = jnp.zeros((tm, ts), F32)
    for r in range(TOP_K):
        gate = jnp.where(lane == slots[r], w[:, r:r + 1], gate)
    gate = gate.astype(BF16)

    def wait(c, carry):
        _chunk_copy(ys_ref, 0, buf_ref.at[cur], 0, sem.at[cur]).wait()
        return carry

    lax.fori_loop(0, lax.fori_loop(0, ne, lambda e, n: n + chunks_s[t * ne + e], 0), wait, 0)
    y = jnp.dot(gate, buf_ref[cur].astype(BF16), preferred_element_type=F32)
    o_ref[...] = _rms(x1_ref[...] + y, g_ref[...])


def _combine(plan, tile0, n, ys, idx, rank, wts, x1, gfin):
    dm = x1.shape[1]
    tm = ROW_TILE
    ne = plan["seg_row"].shape[-1]
    tile = lambda i, *_: (tile0 + i, 0)
    return pl.pallas_call(
        functools.partial(_combine_kernel, tile0=tile0, ne=ne),
        grid_spec=pltpu.PrefetchScalarGridSpec(
            num_scalar_prefetch=3,
            grid=(n // tm,),
            in_specs=[pl.BlockSpec(memory_space=pl.ANY),
                      pl.BlockSpec((tm, TOP_K), tile),
                      pl.BlockSpec((tm, TOP_K), tile),
                      pl.BlockSpec((tm, TOP_K), tile),
                      pl.BlockSpec((None, 1, ne), lambda i, *_: (tile0 + i, 0, 0)),
                      pl.BlockSpec((tm, dm), tile),
                      pl.BlockSpec((1, dm), lambda i, *_: (0, 0))],
            out_specs=pl.BlockSpec((tm, dm), lambda i, *_: (i, 0)),
            scratch_shapes=[pltpu.VMEM((2, _tile_slots(ne), dm), F32),
                            pltpu.SemaphoreType.DMA((2,))]),
        out_shape=jax.ShapeDtypeStruct((n, dm), F32),
        compiler_params=_params("arbitrary"),
        name="moe_combine",
    )(plan["seg_row_s"], plan["chunks_s"], plan["dst_row_s"], ys, idx, rank, wts,
      plan["seg_row"], x1, gfin.reshape(1, dm))


def kernel(x_prompt, x_sample, cache_a_k, cache_a_v, cache_b_k, cache_b_v, norm_attn, w_in,
           rel_bias, g_out_a, g_out_b, w_out, norm_ffn, w_router, b_router, w_gate, b_gate,
           w_up, b_up, w_down, b_down, norm_final):
    depth = w_in.shape[0]
    assert depth == 1, "the MoE combine applies the final norm: single layer only"
    bp, sp, dm = x_prompt.shape
    bs, ts, _ = x_sample.shape
    past = cache_b_k.shape[2]
    wa = cache_a_k.shape[2]
    nh_a, hd = cache_a_k.shape[3], cache_a_k.shape[4]
    nh_b = cache_b_k.shape[3]
    da, db = nh_a * hd, nh_b * hd
    ne = w_router.shape[-1]
    assert da == db and hd == HEAD_DIM and wa == BAND_WINDOW and ts == CHUNK
    slabs = da // LANES
    QA, KA, VA, QB, KB, VB = (c * slabs for c in range(6))
    l = 0
    n_p, n_s = bp * sp, bs * ts

    xp2 = x_prompt.reshape(n_p, dm)
    xs2 = x_sample.reshape(n_s, dm)
    w_in_bf = w_in[l].astype(BF16)
    wp = min(BAND_WINDOW, sp)
    assert sp % ROW_TILE == 0 and wp % ROW_TILE == 0
    qkv_p, ka_p, va_p, kb_p, vb_p = _qkv_proj(xp2, norm_attn[l], w_in_bf,
                                              seq_tiles=sp // ROW_TILE, band_tiles=wp // ROW_TILE)
    qkv_s, ka_s, va_s, kb_s, vb_s = _qkv_proj(xs2, norm_attn[l], w_in_bf)
    qkv_p3 = qkv_p.reshape(bp, sp, 6 * da)
    qkv_s3 = qkv_s.reshape(bs, ts, 6 * da)

    bias = _band_bias(rel_bias[l])
    tile_bias = _band_tile_bias(bias)

    oa_p = _band_attention_prompt(qkv_p3, QA, KA, VA, tile_bias)
    ob_p = _sb_attention(qkv_p3, QB, qkv_p3, KB, qkv_p3, VB, qkv_p3, KB, qkv_p3, VB,
                         slabs, tq=min(SB_TQ, sp), q_base=0)

    oa_s = _band_attention_sample(qkv_s3, QA, KA, VA, cache_a_k[l], cache_a_v[l], bias)
    ob_s = _sb_attention(qkv_s3, QB, qkv_s3, KB, qkv_s3, VB,
                         cache_b_k[l].reshape(bs, past, db), 0,
                         cache_b_v[l].reshape(bs, past, db), 0, slabs, tq=ts, q_base=past)

    w_out_bf = w_out[l].astype(BF16)
    x1, xt, idx, wts, rank, cnt = _merge(
        xp2, xs2, oa_p.reshape(n_p, da), oa_s.reshape(n_s, da), ob_p.reshape(n_p, db),
        ob_s.reshape(n_s, db), g_out_a[l], g_out_b[l], w_out_bf, norm_ffn[l], w_router[l],
        b_router[l])
    plan = _moe_plan(cnt.reshape(-1, ne), n_p + n_s)
    ys = _expert_ffn(plan, xt, w_gate[l], b_gate[l], w_up[l], b_up[l], w_down[l], b_down[l])
    combine = functools.partial(_combine, plan, ys=ys, idx=idx, rank=rank, wts=wts, x1=x1,
                                gfin=norm_final)
    y_prompt = combine(0, n_p).reshape(bp, sp, dm)
    y_sample = combine(n_p // ROW_TILE, n_s).reshape(bs, ts, dm)

    heads_a = lambda t, b_, n_: t.reshape(b_, n_, nh_a, hd)
    heads_b = lambda t, b_, n_: t.reshape(b_, n_, nh_b, hd)
    new_a_k_prompt = heads_a(ka_p, bp, wp)[None]
    new_a_v_prompt = heads_a(va_p, bp, wp)[None]
    new_b_k_prompt = heads_b(kb_p, bp, sp)[None]
    new_b_v_prompt = heads_b(vb_p, bp, sp)[None]
    new_a_k_sample = jnp.concatenate([cache_a_k[l][:, ts:], heads_a(ka_s, bs, ts)], axis=1)[None]
    new_a_v_sample = jnp.concatenate([cache_a_v[l][:, ts:], heads_a(va_s, bs, ts)], axis=1)[None]
    new_b_k_sample = heads_b(kb_s, bs, ts)[None]
    new_b_v_sample = heads_b(vb_s, bs, ts)[None]
    return (y_prompt, y_sample, new_a_k_prompt, new_a_v_prompt, new_b_k_prompt, new_b_v_prompt,
            new_a_k_sample, new_a_v_sample, new_b_k_sample, new_b_v_sample)
```

```python
import functools

import jax
import jax.numpy as jnp
from jax import lax
from jax.experimental import pallas as pl
from jax.experimental.pallas import tpu as pltpu

F32 = jnp.float32
BF16 = jnp.bfloat16
I32 = jnp.int32

CHUNK = 64
LEFT_CHUNKS = 8
BAND_WINDOW = LEFT_CHUNKS * CHUNK
BAND = BAND_WINDOW + CHUNK
HEAD_DIM = 64
HEADS_PER_SLAB = 2
LANES = 128
REL_CLIP = 128
TOP_K = 4
SWIGLU_LIMIT = 7.0
SWIGLU_ALPHA = 1.702
EPS = 1e-6
SCALE = HEAD_DIM ** -0.5

ROW_TILE = 512
BAND_TQ = 256
BAND_KEYS = BAND_TQ + BAND_WINDOW
BAND_VARIANTS = BAND_WINDOW // BAND_TQ + 1
BAND_SLABS = 4
SB_TQ = 512
SB_TK = 256
SB_UNROLL = 1
SB_DEAD = -104.0
MOE_TILE = 512
SEG = 8
VMEM_LIMIT = 56 * 1024 * 1024


def _params(*sem):
    return pltpu.CompilerParams(dimension_semantics=sem, vmem_limit_bytes=VMEM_LIMIT)


def _rms(x, g):
    ms = jnp.mean(x * x, axis=-1, keepdims=True)
    return (x * lax.rsqrt(ms + EPS)) * g


def _head_mask(h):
    lane = lax.broadcasted_iota(I32, (1, LANES), 1)
    return (lane >= h * HEAD_DIM) & (lane < (h + 1) * HEAD_DIM)


def _qkv_kernel(x_ref, g_ref, w_ref, qkv_ref, ka_ref, va_ref, kb_ref, vb_ref):
    xb = _rms(x_ref[...], g_ref[...]).astype(BF16)
    d = ka_ref.shape[-1]
    f32_out = {1: ka_ref, 2: va_ref, 4: kb_ref, 5: vb_ref}
    for c in range(6):
        r = jnp.dot(xb, w_ref[:, c * d:(c + 1) * d], preferred_element_type=F32)
        if c in f32_out:
            f32_out[c][...] = r
        else:
            r = r * SCALE
        qkv_ref[:, c * d:(c + 1) * d] = r.astype(BF16)


def _qkv_proj(x2d, g, w_bf, seq_tiles=1, band_tiles=1):
    n, dm = x2d.shape
    d3 = w_bf.shape[1]
    d = d3 // 6
    tm = min(ROW_TILE, n)
    assert (n // tm) % seq_tiles == 0 and band_tiles <= seq_tiles
    row = lambda i: (i, 0)
    fixed = lambda i: (0, 0)
    skip = seq_tiles - band_tiles
    band_row = lambda i: ((i // seq_tiles) * band_tiles + jnp.maximum(i % seq_tiles - skip, 0), 0)
    n_band = n // seq_tiles * band_tiles
    return pl.pallas_call(
        _qkv_kernel,
        grid=(n // tm,),
        in_specs=[pl.BlockSpec((tm, dm), row),
                  pl.BlockSpec((1, dm), fixed),
                  pl.BlockSpec((dm, d3), fixed)],
        out_specs=[pl.BlockSpec((tm, d3), row)] + [pl.BlockSpec((tm, d), band_row)] * 2
        + [pl.BlockSpec((tm, d), row)] * 2,
        out_shape=[jax.ShapeDtypeStruct((n, d3), BF16)]
        + [jax.ShapeDtypeStruct((n_band, d), F32)] * 2
        + [jax.ShapeDtypeStruct((n, d), F32)] * 2,
        compiler_params=_params("arbitrary"),
        name="qkv_proj",
    )(x2d, g.reshape(1, dm), w_bf)


def _bias_kernel(tab_ref, o_ref):
    h = pl.program_id(0)
    i = lax.broadcasted_iota(I32, (CHUNK, BAND), 0)
    j = lax.broadcasted_iota(I32, (CHUNK, BAND), 1)
    idx = jnp.clip(BAND_WINDOW + i - j, -REL_CLIP, REL_CLIP) + REL_CLIP
    n_tab = 2 * REL_CLIP + 1

    def body(t, acc):
        return jnp.where(idx == t, tab_ref[h, t], acc)

    init = jnp.full((CHUNK, BAND), tab_ref[h, n_tab - 1], F32)
    o_ref[...] = lax.fori_loop(0, n_tab - 1, body, init)


def _band_bias(table):
    nh = table.shape[0]
    return pl.pallas_call(
        _bias_kernel,
        grid=(nh,),
        in_specs=[pl.BlockSpec(memory_space=pltpu.SMEM)],
        out_specs=pl.BlockSpec((None, CHUNK, BAND), lambda h: (h, 0, 0)),
        out_shape=jax.ShapeDtypeStruct((nh, CHUNK, BAND), F32),
        compiler_params=_params("arbitrary"),
        name="band_bias",
    )(table)


def _tile_bias_kernel(b_ref, o_ref):
    v = pl.program_id(1)
    o_ref[...] = jnp.full(o_ref.shape, -jnp.inf, F32)
    for c in range(BAND_TQ // CHUNK):
        o_ref[c * CHUNK:(c + 1) * CHUNK, c * CHUNK:c * CHUNK + BAND] = b_ref[...]
    col = lax.broadcasted_iota(I32, o_ref.shape, 1)
    hidden = (BAND_VARIANTS - 1 - v) * BAND_TQ
    o_ref[...] = jnp.where(col >= hidden, o_ref[...], -jnp.inf)


def _band_tile_bias(bias):
    nh = bias.shape[0]
    return pl.pallas_call(
        _tile_bias_kernel,
        grid=(nh, BAND_VARIANTS),
        in_specs=[pl.BlockSpec((None, CHUNK, BAND), lambda h, v: (h, 0, 0))],
        out_specs=pl.BlockSpec((None, None, BAND_TQ, BAND_KEYS), lambda h, v: (h, v, 0, 0)),
        out_shape=jax.ShapeDtypeStruct((nh, BAND_VARIANTS, BAND_TQ, BAND_KEYS), F32),
        compiler_params=_params("parallel", "arbitrary"),
        name="band_tile_bias",
    )(bias)


def _band_tile_kernel(q_ref, *refs):
    n_kb = BAND_KEYS // BAND_TQ
    k_refs, v_refs = refs[:n_kb], refs[n_kb:2 * n_kb]
    bias_ref, o_ref = refs[2 * n_kb], refs[2 * n_kb + 1]
    for sl in range(BAND_SLABS):
        cols = slice(sl * LANES, (sl + 1) * LANES)
        q = q_ref[:, cols]
        kcat = jnp.concatenate([r[:, cols] for r in k_refs], axis=0)
        vcat = jnp.concatenate([r[:, cols] for r in v_refs], axis=0)
        outs = []
        for h in range(HEADS_PER_SLAB):
            qm = jnp.where(_head_mask(h), q, jnp.zeros_like(q))
            s = lax.dot_general(qm, kcat, (((1,), (1,)), ((), ())),
                                preferred_element_type=F32) + bias_ref[sl * HEADS_PER_SLAB + h]
            m = jnp.max(s, axis=-1, keepdims=True)
            p = jnp.exp(s - m)
            l = jnp.sum(p, axis=-1, keepdims=True)
            pv = jnp.dot(p.astype(BF16), vcat, preferred_element_type=F32)
            outs.append(pv / l)
        o_ref[:, cols] = jnp.where(_head_mask(0), outs[0], outs[1])


def _band_attention_prompt(qkv3, q_col0, k_col0, v_col0, tile_bias):
    b, t, _ = qkv3.shape
    nslab = tile_bias.shape[0] // HEADS_PER_SLAB
    n_kb = BAND_KEYS // BAND_TQ
    assert nslab % BAND_SLABS == 0 and all(c % BAND_SLABS == 0 for c in (q_col0, k_col0, v_col0))
    blk = (None, BAND_TQ, BAND_SLABS * LANES)
    group = lambda col0, g: col0 // BAND_SLABS + g

    def kv_spec(col0, back):
        return pl.BlockSpec(blk, lambda g, bi, j: (bi, jnp.maximum(j - back, 0), group(col0, g)))

    in_specs = [pl.BlockSpec(blk, lambda g, bi, j: (bi, j, group(q_col0, g)))]
    in_specs += [kv_spec(k_col0, n_kb - 1 - i) for i in range(n_kb)]
    in_specs += [kv_spec(v_col0, n_kb - 1 - i) for i in range(n_kb)]
    in_specs += [pl.BlockSpec((BAND_SLABS * HEADS_PER_SLAB, None, BAND_TQ, BAND_KEYS),
                              lambda g, bi, j: (g, jnp.minimum(j, BAND_VARIANTS - 1), 0, 0))]
    return pl.pallas_call(
        _band_tile_kernel,
        grid=(nslab // BAND_SLABS, b, t // BAND_TQ),
        in_specs=in_specs,
        out_specs=pl.BlockSpec(blk, lambda g, bi, j: (bi, j, g)),
        out_shape=jax.ShapeDtypeStruct((b, t, nslab * LANES), F32),
        compiler_params=_params("parallel", "parallel", "arbitrary"),
        name="band_attention_prompt",
    )(qkv3, *([qkv3] * (2 * n_kb)), tile_bias)


def _band_chunk_kernel(q_ref, kp_ref, kc_ref, vp_ref, vc_ref, bias_ref, o_ref):
    def cached(ref, sl):
        heads = [ref[:, sl * HEADS_PER_SLAB + h, :] for h in range(HEADS_PER_SLAB)]
        return jnp.concatenate(heads, axis=-1).astype(BF16)

    for sl in range(o_ref.shape[-1] // LANES):
        cols = slice(sl * LANES, (sl + 1) * LANES)
        qc = q_ref[:, cols]
        kwin = jnp.concatenate([cached(kp_ref, sl), kc_ref[:, cols]], axis=0)
        vwin = jnp.concatenate([cached(vp_ref, sl), vc_ref[:, cols]], axis=0)
        outs = []
        for h in range(HEADS_PER_SLAB):
            qm = jnp.where(_head_mask(h), qc, jnp.zeros_like(qc))
            s = lax.dot_general(qm, kwin, (((1,), (1,)), ((), ())),
                                preferred_element_type=F32) + bias_ref[sl * HEADS_PER_SLAB + h]
            m = jnp.max(s, axis=-1, keepdims=True)
            p = jnp.exp(s - m)
            l = jnp.sum(p, axis=-1, keepdims=True)
            pv = jnp.dot(p.astype(BF16), vwin, preferred_element_type=F32)
            outs.append(pv / l)
        o_ref[:, cols] = jnp.where(_head_mask(0), outs[0], outs[1])


def _band_attention_sample(qkv3, q_col0, k_col0, v_col0, cache_k, cache_v, bias):
    b, t, _ = qkv3.shape
    nh = bias.shape[0]
    d = nh // HEADS_PER_SLAB * LANES
    assert all(c * LANES % d == 0 for c in (q_col0, k_col0, v_col0))
    new = lambda col0: pl.BlockSpec((None, CHUNK, d), lambda bi: (bi, 0, col0 * LANES // d))
    old = pl.BlockSpec((None, BAND_WINDOW, nh, HEAD_DIM), lambda bi: (bi, 0, 0, 0))
    return pl.pallas_call(
        _band_chunk_kernel,
        grid=(b,),
        in_specs=[new(q_col0), old, new(k_col0), old, new(v_col0),
                  pl.BlockSpec((nh, CHUNK, BAND), lambda bi: (0, 0, 0))],
        out_specs=pl.BlockSpec((None, CHUNK, d), lambda bi: (bi, 0, 0)),
        out_shape=jax.ShapeDtypeStruct((b, t, d), F32),
        compiler_params=_params("parallel"),
        name="band_attention_sample",
    )(qkv3, cache_k, qkv3, cache_v, qkv3, bias)


def _sb_kernel(q_ref, kt_ref, vt_ref, kf_ref, vf_ref, o_ref,
               qm_ref, nu_ref, d_ref, tot_ref, acc_ref, run_ref, *, q_base):
    i = pl.program_id(2)
    tq = q_ref.shape[0]
    tile_keys = kt_ref.shape[0]
    tk = SB_TK
    qpos0 = q_base + i * tq
    n_full = qpos0 // tk
    n_masked = -(-tile_keys // tk)

    q = q_ref[...]
    for h in range(HEADS_PER_SLAB):
        qm_ref[h] = jnp.where(_head_mask(h), q, jnp.zeros_like(q))
    jj = lax.broadcasted_iota(I32, (tk, tk), 0)
    ss = lax.broadcasted_iota(I32, (tk, tk), 1)
    nu_ref[...] = jnp.where(jj >= ss, -1.0, 0.0).astype(BF16)
    acc_ref[...] = jnp.zeros_like(acc_ref)
    run_ref[...] = jnp.zeros_like(run_ref)

    def tile_block(ref, m):
        rows = min(tk, tile_keys - m * tk)
        blk = ref[m * tk:m * tk + rows, :].astype(BF16)
        if rows < tk:
            blk = jnp.concatenate([blk, jnp.zeros((tk - rows, LANES), BF16)], axis=0)
        return blk

    def full_block(ref, kb):
        k0 = pl.multiple_of(kb * tk, tk)
        return ref[pl.ds(k0, tk), :].astype(BF16)

    def stage_a(kblk, tile_m):
        if tile_m is not None:
            row = lax.broadcasted_iota(I32, (tq, tk), 0)
            col = lax.broadcasted_iota(I32, (tq, tk), 1)
            vis = tile_m * tk + col < row
        for h in range(HEADS_PER_SLAB):
            z = lax.dot_general(qm_ref[h], kblk, (((1,), (1,)), ((), ())),
                                preferred_element_type=F32)
            sp = jnp.maximum(z, 0.0) + jnp.log(1.0 + jnp.exp(-jnp.abs(z)))
            if tile_m is not None:
                sp = jnp.where(vis, sp, 0.0)
            after = jnp.dot(sp.astype(BF16), nu_ref[...], preferred_element_type=F32)
            d = z + after
            if tile_m is not None:
                d = jnp.where(vis, d, -jnp.inf)
            d_ref[h] = d
            tot_ref[h] = after[:, 0:1]

    def stage_b(vblk):
        for h in range(HEADS_PER_SLAB):
            run = run_ref[h]
            a = jnp.exp(d_ref[h] + run)
            acc_ref[h] += jnp.dot(a.astype(BF16), vblk, preferred_element_type=F32)
            run_ref[h] = run + tot_ref[h]

    def step(kb):
        stage_b(full_block(vf_ref, kb + 1))
        stage_a(full_block(kf_ref, kb), None)

    def alive():
        r = jnp.maximum(run_ref[0], run_ref[1])
        return (jnp.max(r) >= SB_DEAD).astype(I32)

    stage_a(tile_block(kt_ref, n_masked - 1), n_masked - 1)
    for m in range(n_masked - 2, -1, -1):
        stage_b(tile_block(vt_ref, m + 1))
        stage_a(tile_block(kt_ref, m), m)

    @pl.when(n_full == 0)
    def _():
        stage_b(tile_block(vt_ref, 0))

    @pl.when(n_full > 0)
    def _():
        stage_b(tile_block(vt_ref, 0))
        stage_a(full_block(kf_ref, n_full - 1), None)

        def cond(c):
            kb, live = c
            return (kb >= SB_UNROLL - 1) & (live > 0)

        def body(c):
            kb, _ = c
            for u in range(SB_UNROLL):
                step(kb - u)
            return kb - SB_UNROLL, alive()

        kb, live = lax.while_loop(cond, body, (n_full - 2, alive()))
        for _ in range(SB_UNROLL - 1):
            go = (kb >= 0) & (live > 0)

            @pl.when(go)
            def _():
                step(kb)

            kb = jnp.where(go, kb - 1, kb)
            live = alive()

        @pl.when(live > 0)
        def _():
            stage_b(full_block(vf_ref, kb + 1))

    o_ref[...] = jnp.where(_head_mask(0), acc_ref[0], acc_ref[1])


def _sb_attention(q_arr, q_col0, kt_arr, kt_col0, vt_arr, vt_col0,
                  kf_arr, kf_col0, vf_arr, vf_col0, nslab, tq, q_base):
    b, t, _ = q_arr.shape
    tf = kf_arr.shape[1]
    assert q_base % SB_TK == 0 and t % tq == 0 and tf % SB_TK == 0
    assert tq % SB_TK == 0 or t == tq
    assert q_base + t - tq <= tf
    tile = lambda col0: pl.BlockSpec((None, tq, LANES), lambda bi, hp, i: (bi, i, col0 + hp))
    full = lambda col0: pl.BlockSpec((None, tf, LANES), lambda bi, hp, i: (bi, 0, col0 + hp))
    return pl.pallas_call(
        functools.partial(_sb_kernel, q_base=q_base),
        grid=(b, nslab, t // tq),
        in_specs=[tile(q_col0), tile(kt_col0), tile(vt_col0), full(kf_col0), full(vf_col0)],
        out_specs=pl.BlockSpec((None, tq, LANES), lambda bi, hp, i: (bi, i, hp)),
        out_shape=jax.ShapeDtypeStruct((b, t, nslab * LANES), F32),
        scratch_shapes=[pltpu.VMEM((HEADS_PER_SLAB, tq, LANES), BF16),
                        pltpu.VMEM((SB_TK, SB_TK), BF16),
                        pltpu.VMEM((HEADS_PER_SLAB, tq, SB_TK), F32),
                        pltpu.VMEM((HEADS_PER_SLAB, tq, 1), F32),
                        pltpu.VMEM((HEADS_PER_SLAB, tq, LANES), F32),
                        pltpu.VMEM((HEADS_PER_SLAB, tq, 1), F32)],
        compiler_params=_params("parallel", "parallel", "arbitrary"),
        name="stick_breaking",
    )(q_arr, kt_arr, vt_arr, kf_arr, vf_arr)


def _split_bf16(x):
    hi = x.astype(BF16)
    return hi, (x - hi.astype(F32)).astype(BF16)


def _merge_kernel(xp_ref, xs_ref, oap_ref, oas_ref, obp_ref, obs_ref, ga_ref, gb_ref, wo_ref,
                  gf_ref, wr_ref, br_ref, x1_ref, xt_ref, idx_ref, w_ref, rank_ref, cnt_ref,
                  *, tiles_p):
    prompt = pl.program_id(0) < tiles_p
    pick = lambda p_ref, s_ref: jnp.where(prompt, p_ref[...], s_ref[...])
    o = jnp.concatenate([_rms(pick(oap_ref, oas_ref), ga_ref[...]),
                         _rms(pick(obp_ref, obs_ref), gb_ref[...])], axis=-1).astype(BF16)
    x1 = pick(xp_ref, xs_ref) + jnp.dot(o, wo_ref[...], preferred_element_type=F32)
    x1_ref[...] = x1
    xn = _rms(x1, gf_ref[...])
    xh, xl = _split_bf16(xn)
    wh, wl = _split_bf16(wr_ref[...])
    logits = (jnp.dot(xh, wh, preferred_element_type=F32)
              + jnp.dot(xl, wh, preferred_element_type=F32)
              + jnp.dot(xh, wl, preferred_element_type=F32)) + br_ref[...]
    tm, ne = logits.shape
    lane = lax.broadcasted_iota(I32, (tm, ne), 1)
    slot = lax.broadcasted_iota(I32, (tm, TOP_K), 1)
    work = logits
    top = None
    denom = None
    sels = []
    idx = jnp.zeros((tm, TOP_K), I32)
    wts = jnp.zeros((tm, TOP_K), F32)
    for r in range(TOP_K):
        m = jnp.max(work, axis=-1, keepdims=True)
        first = jnp.min(jnp.where(work == m, lane, ne), axis=-1, keepdims=True)
        sel = lane == first
        if top is None:
            top = m
        e = jnp.exp(m - top)
        denom = e if denom is None else denom + e
        idx = jnp.where(slot == r, first, idx)
        wts = jnp.where(slot == r, e, wts)
        sels.append(sel)
        work = jnp.where(sel, -jnp.inf, work)
    idx_ref[...] = idx
    w_ref[...] = wts / denom
    chosen = sels[0] | sels[1] | sels[2] | sels[3]
    onehot = jnp.where(chosen, 1.0, 0.0)
    rr = lax.broadcasted_iota(I32, (tm, tm), 0)
    cc = lax.broadcasted_iota(I32, (tm, tm), 1)
    lower = jnp.where(cc < rr, 1.0, 0.0).astype(BF16)
    before = jnp.dot(lower, onehot.astype(BF16), preferred_element_type=F32)
    rank = jnp.zeros((tm, TOP_K), I32)
    for r in range(TOP_K):
        rk = jnp.sum(jnp.where(sels[r], before, 0.0), axis=-1, keepdims=True)
        rank = jnp.where(slot == r, rk.astype(I32), rank)
    rank_ref[...] = rank
    cnt = jnp.sum(onehot, axis=0, keepdims=True).astype(I32)
    cnt_ref[...] = cnt
    chunks = lax.shift_right_logical(cnt + (SEG - 1), SEG.bit_length() - 1)
    ee = lax.broadcasted_iota(I32, (ne, ne), 0)
    ff = lax.broadcasted_iota(I32, (ne, ne), 1)
    earlier = jnp.where(ee < ff, 1.0, 0.0).astype(BF16)
    chunks8 = jnp.broadcast_to(chunks.astype(F32), (SEG, ne)).astype(BF16)
    seg_row = (jnp.dot(chunks8, earlier, preferred_element_type=F32)[0:1] * SEG).astype(I32)
    slots = _slots_of(idx, rank, seg_row)
    ts = xt_ref.shape[0]
    lane_s = lax.broadcasted_iota(I32, (tm, ts), 1)
    hit = ((lane_s == slots[0]) | (lane_s == slots[1]) | (lane_s == slots[2])
           | (lane_s == slots[3]))
    place = jnp.where(hit, 1.0, 0.0).astype(BF16)
    xt_ref[...] = lax.dot_general(place, xn.astype(BF16), (((0,), (0,)), ((), ())),
                                  preferred_element_type=F32)


def _merge(xp, xs, oa_p, oa_s, ob_p, ob_s, ga, gb, wo_bf, gf, w_router, b_router):
    (n_p, dm), n_s = xp.shape, xs.shape[0]
    n = n_p + n_s
    da = oa_p.shape[1]
    ne = w_router.shape[1]
    tm = ROW_TILE
    assert n_p % tm == 0 and n_s % tm == 0
    tiles_p = n_p // tm
    row = lambda i: (i, 0)
    fixed = lambda i: (0, 0)
    from_p = lambda i: (jnp.minimum(i, tiles_p - 1), 0)
    from_s = lambda i: (jnp.maximum(i - tiles_p, 0), 0)
    return pl.pallas_call(
        functools.partial(_merge_kernel, tiles_p=tiles_p),
        grid=(n // tm,),
        in_specs=[pl.BlockSpec((tm, dm), from_p),
                  pl.BlockSpec((tm, dm), from_s),
                  pl.BlockSpec((tm, da), from_p),
                  pl.BlockSpec((tm, da), from_s),
                  pl.BlockSpec((tm, da), from_p),
                  pl.BlockSpec((tm, da), from_s),
                  pl.BlockSpec((1, da), fixed),
                  pl.BlockSpec((1, da), fixed),
                  pl.BlockSpec((2 * da, dm), fixed),
                  pl.BlockSpec((1, dm), fixed),
                  pl.BlockSpec((dm, ne), fixed),
                  pl.BlockSpec((1, ne), fixed)],
        out_specs=[pl.BlockSpec((tm, dm), row),
                   pl.BlockSpec((_tile_slots(ne), dm), row),
                   pl.BlockSpec((tm, TOP_K), row),
                   pl.BlockSpec((tm, TOP_K), row),
                   pl.BlockSpec((tm, TOP_K), row),
                   pl.BlockSpec((None, 1, ne), lambda i: (i, 0, 0))],
        out_shape=[jax.ShapeDtypeStruct((n, dm), F32),
                   jax.ShapeDtypeStruct((n // tm * _tile_slots(ne), dm), F32),
                   jax.ShapeDtypeStruct((n, TOP_K), I32),
                   jax.ShapeDtypeStruct((n, TOP_K), F32),
                   jax.ShapeDtypeStruct((n, TOP_K), I32),
                   jax.ShapeDtypeStruct((n // tm, 1, ne), I32)],
        compiler_params=_params("parallel"),
        name="merge_router",
    )(xp, xs, oa_p, oa_s, ob_p, ob_s, ga.reshape(1, da), gb.reshape(1, da), wo_bf,
      gf.reshape(1, dm), w_router, b_router.reshape(1, ne))


def _tile_slots(ne):
    return ROW_TILE * TOP_K + ne * SEG


def _slots_of(idx, rank, seg_row):
    tm, ne = idx.shape[0], seg_row.shape[-1]
    lane = lax.broadcasted_iota(I32, (tm, ne), 1)
    return [jnp.sum(jnp.where(lane == idx[:, r:r + 1], seg_row, 0), axis=-1, keepdims=True)
            + rank[:, r:r + 1] for r in range(TOP_K)]


def _chunk_copy(src_ref, src_row, dst_ref, dst_row, sem):
    aligned = lambda row: row if isinstance(row, int) else pl.multiple_of(row, SEG)
    return pltpu.make_async_copy(src_ref.at[pl.ds(aligned(src_row), SEG), :],
                                 dst_ref.at[pl.ds(aligned(dst_row), SEG), :], sem)


def _moe_plan(cnt, n_rows):
    nt, ne = cnt.shape
    tm = MOE_TILE
    seg_len = (cnt + SEG - 1) // SEG * SEG
    seg_row = jnp.cumsum(seg_len, axis=1) - seg_len
    rows = jnp.sum(seg_len, axis=0)
    region = (rows + tm - 1) // tm * tm
    base = jnp.cumsum(region) - region
    dst_row = base[None, :] + jnp.cumsum(seg_len, axis=0) - seg_len
    n_tiles = (n_rows * TOP_K + ne * ((SEG - 1) * nt + tm - 1)) // tm
    tile_end = jnp.cumsum(region // tm)
    tiles = jnp.arange(n_tiles, dtype=I32)
    tile_expert = jnp.minimum(jnp.sum((tiles[:, None] >= tile_end[None, :]).astype(I32), axis=1),
                              ne - 1)
    ts = _tile_slots(ne)
    chunks = seg_len // SEG
    seg_start = ((base // SEG)[:, None] + jnp.cumsum(chunks.T, axis=1) - chunks.T).reshape(-1)
    seg_end = seg_start + chunks.T.reshape(-1)
    seg_src = (jnp.arange(nt, dtype=I32)[None, :] * ts + seg_row.T).reshape(-1)
    g = jnp.arange(n_tiles * tm // SEG, dtype=I32)[:, None]
    inside = (g >= seg_start[None, :]) & (g < seg_end[None, :])
    src = jnp.sum(jnp.where(inside, seg_src[None, :] + (g - seg_start[None, :]) * SEG, 0), axis=1)
    src = jnp.where(jnp.any(inside, axis=1), src, ts - SEG)
    flat = lambda a: a.reshape(-1).astype(I32)
    return dict(seg_row=seg_row.astype(I32).reshape(nt, 1, ne), seg_row_s=flat(seg_row),
                chunks_s=flat(chunks), dst_row_s=flat(dst_row), src_row_s=flat(src),
                tile_expert=tile_expert.astype(I32),
                tile_valid=(tiles < tile_end[ne - 1]).astype(I32), n_slots=n_tiles * tm)


def _ffn_kernel(te_ref, tv_ref, src_ref, xt_ref, wg_ref, bg_ref, wu_ref, bu_ref,
                wd_ref, bd_ref, y_ref, xbuf_ref, wgb_ref, wub_ref, wdb_ref, sem):
    t = pl.program_id(0)
    tm = y_ref.shape[0]
    n_chunks = tm // SEG
    cur = t % 2
    new_expert = (t == 0) | (te_ref[t] != te_ref[jnp.maximum(t - 1, 0)])

    def gather(tile, slot):
        def start(c, carry):
            _chunk_copy(xt_ref, src_ref[tile * n_chunks + c], xbuf_ref.at[slot], c * SEG,
                        sem.at[slot]).start()
            return carry

        lax.fori_loop(0, n_chunks, start, 0)

    @pl.when(t == 0)
    def _():
        gather(0, 0)

    nxt = jnp.minimum(t + 1, pl.num_programs(0) - 1)
    more = (t + 1 < pl.num_programs(0)) & (tv_ref[nxt] != 0)
    for slot in range(2):
        @pl.when(more & (cur != slot))
        def _():
            gather(t + 1, slot)

    @pl.when(new_expert)
    def _():
        wgb_ref[...] = wg_ref[...].astype(BF16)
        wub_ref[...] = wu_ref[...].astype(BF16)
        wdb_ref[...] = wd_ref[...].astype(BF16)

    @pl.when(tv_ref[t] != 0)
    def _():
        pltpu.make_async_copy(xt_ref.at[pl.ds(0, tm), :], xbuf_ref.at[cur], sem.at[cur]).wait()
        x = xbuf_ref[cur].astype(BF16)
        g = jnp.minimum(jnp.dot(x, wgb_ref[...], preferred_element_type=F32) + bg_ref[...],
                        SWIGLU_LIMIT)
        u = jnp.clip(jnp.dot(x, wub_ref[...], preferred_element_type=F32) + bu_ref[...],
                     -SWIGLU_LIMIT, SWIGLU_LIMIT)
        h = g * jax.nn.sigmoid(SWIGLU_ALPHA * g) * (u + 1.0)
        y_ref[...] = jnp.dot(h.astype(BF16), wdb_ref[...], preferred_element_type=F32) + bd_ref[...]

    @pl.when(tv_ref[t] == 0)
    def _():
        y_ref[...] = jnp.zeros_like(y_ref)


def _expert_ffn(plan, xt, wg, bg, wu, bu, wd, bd):
    n_slots, dm = plan["n_slots"], xt.shape[1]
    ne, _, dff = wg.shape
    tm = MOE_TILE
    row = lambda t, *_: (t, 0)
    exp3 = lambda t, te, *_: (te[t], 0, 0)
    return pl.pallas_call(
        _ffn_kernel,
        grid_spec=pltpu.PrefetchScalarGridSpec(
            num_scalar_prefetch=3,
            grid=(n_slots // tm,),
            in_specs=[pl.BlockSpec(memory_space=pl.ANY),
                      pl.BlockSpec((None, dm, dff), exp3),
                      pl.BlockSpec((None, 1, dff), exp3),
                      pl.BlockSpec((None, dm, dff), exp3),
                      pl.BlockSpec((None, 1, dff), exp3),
                      pl.BlockSpec((None, dff, dm), exp3),
                      pl.BlockSpec((None, 1, dm), exp3)],
            out_specs=pl.BlockSpec((tm, dm), row),
            scratch_shapes=[pltpu.VMEM((2, tm, dm), F32),
                            pltpu.VMEM((dm, dff), BF16),
                            pltpu.VMEM((dm, dff), BF16),
                            pltpu.VMEM((dff, dm), BF16),
                            pltpu.SemaphoreType.DMA((2,))]),
        out_shape=jax.ShapeDtypeStruct((n_slots, dm), F32),
        compiler_params=_params("arbitrary"),
        name="moe_ffn",
    )(plan["tile_expert"], plan["tile_valid"], plan["src_row_s"], xt,
      wg, bg.reshape(ne, 1, dff), wu, bu.reshape(ne, 1, dff), wd, bd.reshape(ne, 1, dm))


def _combine_kernel(seg_s, chunks_s, dst_s, ys_ref, idx_ref, rank_ref, w_ref, seg_ref, x1_ref,
                    g_ref, o_ref, buf_ref, sem, *, tile0, ne):
    i = pl.program_id(0)
    t = tile0 + i
    tm, ts = x1_ref.shape[0], buf_ref.shape[0]

    @pl.when(i == 0)
    def _():
        buf_ref[...] = jnp.zeros_like(buf_ref)

    def fetch(e, started):
        k = t * ne + e
        n_chunks = chunks_s[k]

        def start(c, carry):
            _chunk_copy(ys_ref, dst_s[k] + c * SEG, buf_ref, seg_s[k] + c * SEG, sem).start()
            return carry

        lax.fori_loop(0, n_chunks, start, 0)
        return started + n_chunks

    def wait(c, carry):
        _chunk_copy(ys_ref, 0, buf_ref, 0, sem).wait()
        return carry

    started = lax.fori_loop(0, ne, fetch, 0)

    slots = _slots_of(idx_ref[...], rank_ref[...], seg_ref[...])
    lane = lax.broadcasted_iota(I32, (tm, ts), 1)
    w = w_ref[...]
    gate = jnp.zeros((tm, ts), F32)
    for r in range(TOP_K):
        gate = jnp.where(lane == slots[r], w[:, r:r + 1], gate)
    gate = gate.astype(BF16)
    lax.fori_loop(0, started, wait, 0)
    y = jnp.dot(gate, buf_ref[...].astype(BF16), preferred_element_type=F32)
    o_ref[...] = _rms(x1_ref[...] + y, g_ref[...])


def _combine(plan, tile0, n, ys, idx, rank, wts, x1, gfin):
    dm = x1.shape[1]
    tm = ROW_TILE
    ne = plan["seg_row"].shape[-1]
    tile = lambda i, *_: (tile0 + i, 0)
    return pl.pallas_call(
        functools.partial(_combine_kernel, tile0=tile0, ne=ne),
        grid_spec=pltpu.PrefetchScalarGridSpec(
            num_scalar_prefetch=3,
            grid=(n // tm,),
            in_specs=[pl.BlockSpec(memory_space=pl.ANY),
                      pl.BlockSpec((tm, TOP_K), tile),
                      pl.BlockSpec((tm, TOP_K), tile),
                      pl.BlockSpec((tm, TOP_K), tile),
                      pl.BlockSpec((None, 1, ne), lambda i, *_: (tile0 + i, 0, 0)),
                      pl.BlockSpec((tm, dm), tile),
                      pl.BlockSpec((1, dm), lambda i, *_: (0, 0))],
            out_specs=pl.BlockSpec((tm, dm), lambda i, *_: (i, 0)),
            scratch_shapes=[pltpu.VMEM((_tile_slots(ne), dm), F32),
                            pltpu.SemaphoreType.DMA]),
        out_shape=jax.ShapeDtypeStruct((n, dm), F32),
        compiler_params=_params("arbitrary"),
        name="moe_combine",
    )(plan["seg_row_s"], plan["chunks_s"], plan["dst_row_s"], ys, idx, rank, wts,
      plan["seg_row"], x1, gfin.reshape(1, dm))


def kernel(x_prompt, x_sample, cache_a_k, cache_a_v, cache_b_k, cache_b_v, norm_attn, w_in,
           rel_bias, g_out_a, g_out_b, w_out, norm_ffn, w_router, b_router, w_gate, b_gate,
           w_up, b_up, w_down, b_down, norm_final):
    depth = w_in.shape[0]
    assert depth == 1, "the MoE combine applies the final norm: single layer only"
    bp, sp, dm = x_prompt.shape
    bs, ts, _ = x_sample.shape
    past = cache_b_k.shape[2]
    wa = cache_a_k.shape[2]
    nh_a, hd = cache_a_k.shape[3], cache_a_k.shape[4]
    nh_b = cache_b_k.shape[3]
    da, db = nh_a * hd, nh_b * hd
    ne = w_router.shape[-1]
    assert da == db and hd == HEAD_DIM and wa == BAND_WINDOW and ts == CHUNK
    slabs = da // LANES
    QA, KA, VA, QB, KB, VB = (c * slabs for c in range(6))
    l = 0
    n_p, n_s = bp * sp, bs * ts

    xp2 = x_prompt.reshape(n_p, dm)
    xs2 = x_sample.reshape(n_s, dm)
    w_in_bf = w_in[l].astype(BF16)
    wp = min(BAND_WINDOW, sp)
    assert sp % ROW_TILE == 0 and wp % ROW_TILE == 0
    qkv_p, ka_p, va_p, kb_p, vb_p = _qkv_proj(xp2, norm_attn[l], w_in_bf,
                                              seq_tiles=sp // ROW_TILE, band_tiles=wp // ROW_TILE)
    qkv_s, ka_s, va_s, kb_s, vb_s = _qkv_proj(xs2, norm_attn[l], w_in_bf)
    qkv_p3 = qkv_p.reshape(bp, sp, 6 * da)
    qkv_s3 = qkv_s.reshape(bs, ts, 6 * da)

    bias = _band_bias(rel_bias[l])
    tile_bias = _band_tile_bias(bias)

    oa_p = _band_attention_prompt(qkv_p3, QA, KA, VA, tile_bias)
    ob_p = _sb_attention(qkv_p3, QB, qkv_p3, KB, qkv_p3, VB, qkv_p3, KB, qkv_p3, VB,
                         slabs, tq=min(SB_TQ, sp), q_base=0)

    oa_s = _band_attention_sample(qkv_s3, QA, KA, VA, cache_a_k[l], cache_a_v[l], bias)
    ob_s = _sb_attention(qkv_s3, QB, qkv_s3, KB, qkv_s3, VB,
                         cache_b_k[l].reshape(bs, past, db), 0,
                         cache_b_v[l].reshape(bs, past, db), 0, slabs, tq=ts, q_base=past)

    w_out_bf = w_out[l].astype(BF16)
    x1, xt, idx, wts, rank, cnt = _merge(
        xp2, xs2, oa_p.reshape(n_p, da), oa_s.reshape(n_s, da), ob_p.reshape(n_p, db),
        ob_s.reshape(n_s, db), g_out_a[l], g_out_b[l], w_out_bf, norm_ffn[l], w_router[l],
        b_router[l])
    plan = _moe_plan(cnt.reshape(-1, ne), n_p + n_s)
    ys = _expert_ffn(plan, xt, w_gate[l], b_gate[l], w_up[l], b_up[l], w_down[l], b_down[l])
    combine = functools.partial(_combine, plan, ys=ys, idx=idx, rank=rank, wts=wts, x1=x1,
                                gfin=norm_final)
    y_prompt = combine(0, n_p).reshape(bp, sp, dm)
    y_sample = combine(n_p // ROW_TILE, n_s).reshape(bs, ts, dm)

    heads_a = lambda t, b_, n_: t.reshape(b_, n_, nh_a, hd)
    heads_b = lambda t, b_, n_: t.reshape(b_, n_, nh_b, hd)
    new_a_k_prompt = heads_a(ka_p, bp, wp)[None]
    new_a_v_prompt = heads_a(va_p, bp, wp)[None]
    new_b_k_prompt = heads_b(kb_p, bp, sp)[None]
    new_b_v_prompt = heads_b(vb_p, bp, sp)[None]
    new_a_k_sample = jnp.concatenate([cache_a_k[l][:, ts:], heads_a(ka_s, bs, ts)], axis=1)[None]
    new_a_v_sample = jnp.concatenate([cache_a_v[l][:, ts:], heads_a(va_s, bs, ts)], axis=1)[None]
    new_b_k_sample = heads_b(kb_s, bs, ts)[None]
    new_b_v_sample = heads_b(vb_s, bs, ts)[None]
    return (y_prompt, y_sample, new_a_k_prompt, new_a_v_prompt, new_b_k_prompt, new_b_v_prompt,
            new_a_k_sample, new_a_v_sample, new_b_k_sample, new_b_v_sample)
```
